```python
import math
import jax, jax.numpy as jnp
from jax import lax
import numpy as np

D_MODEL = 1024
BATCH = 2
SEQ = 16384
DEPTH = 4

A_HEADS = 4
A_HEAD_DIM = 64
A_V_DIM = 2 * A_HEAD_DIM
B_HEADS = 4
B_HEAD_DIM = 128
CONV_WIDTH = 4
DELTA_CHUNK = 64
C_HEADS = 16
C_HEAD_DIM = 64
T5_BUCKETS = 32
T5_MAX_DISTANCE = 2048
Q_BLOCK = 128
N_EXPERTS = 32
TOP_K = 4
D_FF = D_MODEL
SWIGLU_LIMIT = 7.0
SWIGLU_ALPHA = 1.702
MOE_BLOCK = 256
DEEPNORM_ALPHA = (2 * DEPTH) ** 0.25
DEEPNORM_BETA = (8 * DEPTH) ** -0.25
LN_EPS = 1e-5
RMS_EPS = 1e-6
N_EVEN = (DEPTH + 1) // 2
N_ODD = DEPTH // 2

A_Q = A_HEADS * 2 * A_HEAD_DIM
A_K = A_HEADS * 2 * A_HEAD_DIM
A_V = A_HEADS * A_V_DIM
B_W = B_HEADS * B_HEAD_DIM
EVEN_SPLITS = [A_Q, A_Q + A_K, A_Q + A_K + A_V, A_Q + A_K + A_V + 3 * B_W,
               A_Q + A_K + A_V + 4 * B_W, A_Q + A_K + A_V + 4 * B_W + B_HEADS]
EVEN_IN = A_Q + A_K + A_V + 4 * B_W + 2 * B_HEADS
EVEN_OUT = A_V + B_W
C_W = C_HEADS * C_HEAD_DIM
ODD_SPLITS = [C_W, 2 * C_W, 3 * C_W, 4 * C_W]
ODD_IN = 4 * C_W + C_HEADS
ODD_OUT = C_W

kernel_name = "hybrid_diffattn_gdn_fox_moe_deepnorm"


def layer_norm(x, g, b):
    xf = x.astype(jnp.float32)
    mu = jnp.mean(xf, -1, keepdims=True)
    var = jnp.mean(jnp.square(xf - mu), -1, keepdims=True)
    return ((xf - mu) * lax.rsqrt(var + LN_EPS) * g.astype(jnp.float32) + b.astype(jnp.float32)).astype(x.dtype)


def rms_norm(x, w):
    xf = x.astype(jnp.float32)
    return (xf * lax.rsqrt(jnp.mean(xf * xf, -1, keepdims=True) + RMS_EPS) * w.astype(jnp.float32)).astype(x.dtype)


def l2_norm(x):
    xf = x.astype(jnp.float32)
    return (xf * lax.rsqrt(jnp.sum(xf * xf, -1, keepdims=True) + RMS_EPS)).astype(x.dtype)


def t5_bucket(rel):
    n = jnp.maximum(rel, 0)
    max_exact = T5_BUCKETS // 2
    nf = jnp.maximum(n, 1).astype(jnp.float32)
    large = max_exact + (jnp.log(nf / max_exact) / math.log(T5_MAX_DISTANCE / max_exact)
                         * (T5_BUCKETS - max_exact)).astype(jnp.int32)
    large = jnp.minimum(large, T5_BUCKETS - 1)
    return jnp.where(n < max_exact, n, large)


def diff_attention(q, k, v, lam, bias_table):
    Bsz, H, _, T, d = q.shape
    scale = d ** -0.5
    k_pos = jnp.arange(T)
    table = bias_table.astype(jnp.float32).T

    def block(i):
        start = i * Q_BLOCK
        qb = lax.dynamic_slice_in_dim(q, start, Q_BLOCK, axis=3)
        rel = (start + jnp.arange(Q_BLOCK))[:, None] - k_pos[None, :]
        bias = table[:, t5_bucket(rel)]
        s = jnp.einsum('bhmqd,bhmkd->bhmqk', qb, k).astype(jnp.float32) * scale + bias[None, :, None]
        p = jax.nn.softmax(jnp.where(rel >= 0, s, -jnp.inf), axis=-1)
        a = p[:, :, 0] - lam * p[:, :, 1]
        return jnp.einsum('bhqk,bhkd->bhqd', a.astype(v.dtype), v)

    out = lax.map(block, jnp.arange(T // Q_BLOCK))
    return out.transpose(1, 2, 0, 3, 4).reshape(Bsz, H, T, v.shape[-1])


def forgetting_attention(q, k, v, logf):
    Bsz, H, T, d = q.shape
    scale = d ** -0.5
    cum = jnp.cumsum(logf.astype(jnp.float32), axis=-1)
    k_pos = jnp.arange(T)

    def block(i):
        start = i * Q_BLOCK
        qb = lax.dynamic_slice_in_dim(q, start, Q_BLOCK, axis=2)
        cq = lax.dynamic_slice_in_dim(cum, start, Q_BLOCK, axis=2)
        causal = (start + jnp.arange(Q_BLOCK))[:, None] >= k_pos[None, :]
        s = jnp.einsum('bhqd,bhkd->bhqk', qb, k).astype(jnp.float32) * scale + cq[..., :, None] - cum[..., None, :]
        p = jax.nn.softmax(jnp.where(causal, s, -jnp.inf), axis=-1)
        return jnp.einsum('bhqk,bhkd->bhqd', p.astype(v.dtype), v)

    out = lax.map(block, jnp.arange(T // Q_BLOCK))
    return out.transpose(1, 2, 0, 3, 4).reshape(Bsz, H, T, d)


def gated_delta_rule(q, k, v, g, beta):
    f32 = jnp.float32
    Bsz, H, T, dk = q.shape
    dv = v.shape[-1]
    C = DELTA_CHUNK
    N = T // C
    q = (q.astype(f32) * dk ** -0.5).reshape(Bsz, H, N, C, dk)
    k = k.astype(f32).reshape(Bsz, H, N, C, dk)
    v = v.astype(f32).reshape(Bsz, H, N, C, dv)
    beta = beta.astype(f32).reshape(Bsz, H, N, C, 1)
    gc = jnp.cumsum(g.astype(f32).reshape(Bsz, H, N, C), axis=-1)
    idx = jnp.arange(C)
    incl = idx[:, None] >= idx[None, :]
    strict = (idx[:, None] > idx[None, :]).astype(f32)
    decay = jnp.exp(jnp.where(incl, gc[..., :, None] - gc[..., None, :], -jnp.inf))
    kb = k * beta
    lower = jnp.einsum('bhncd,bhnmd->bhncm', kb, k) * decay * strict
    rhs = jnp.concatenate([v * beta, kb * jnp.exp(gc)[..., None]], axis=-1)
    sol = lax.linalg.triangular_solve(jnp.eye(C, dtype=f32) + lower, rhs,
                                      left_side=True, lower=True, unit_diagonal=True)
    u, w = sol[..., :dv], sol[..., dv:]
    qk_intra = jnp.einsum('bhncd,bhnmd->bhncm', q, k) * decay
    q_dec = q * jnp.exp(gc)[..., None]
    k_dec = k * jnp.exp(gc[..., -1:] - gc)[..., None]
    g_last = jnp.exp(gc[..., -1])

    def step(S, xs):
        u_n, w_n, qk_n, qd_n, kd_n, gl_n = xs
        v_new = u_n - jnp.einsum('bhck,bhkv->bhcv', w_n, S)
        o_n = jnp.einsum('bhck,bhkv->bhcv', qd_n, S) + jnp.einsum('bhcm,bhmv->bhcv', qk_n, v_new)
        S = S * gl_n[..., None, None] + jnp.einsum('bhck,bhcv->bhkv', kd_n, v_new)
        return S, o_n

    xs = tuple(jnp.moveaxis(a, 2, 0) for a in (u, w, qk_intra, q_dec, k_dec, g_last))
    _, o = lax.scan(step, jnp.zeros((Bsz, H, dk, dv), f32), xs)
    return jnp.moveaxis(o, 0, 2).reshape(Bsz, H, T, dv)


def causal_conv_silu(x, w):
    K, C = w.shape
    y = lax.conv_general_dilated(x, w[:, None, :].astype(x.dtype), window_strides=(1,),
                                 padding=[(K - 1, 0)], dimension_numbers=('NWC', 'WIO', 'NWC'),
                                 feature_group_count=C)
    return jax.nn.silu(y)


def even_mixer(x, w_in, w_out, lam_params, subln_w, conv_w, a_log, dt_bias, gdn_norm_w, t5_bias, lam_init):
    Bsz, T, _ = x.shape
    proj = x @ w_in
    aq, ak, av, bqkv, bz, bb, ba = jnp.split(proj, EVEN_SPLITS, axis=-1)
    aq = aq.reshape(Bsz, T, A_HEADS, 2, A_HEAD_DIM).transpose(0, 2, 3, 1, 4)
    ak = ak.reshape(Bsz, T, A_HEADS, 2, A_HEAD_DIM).transpose(0, 2, 3, 1, 4)
    av = av.reshape(Bsz, T, A_HEADS, A_V_DIM).transpose(0, 2, 1, 3)
    lp = lam_params.astype(jnp.float32)
    lam = jnp.exp(jnp.sum(lp[0] * lp[1])) - jnp.exp(jnp.sum(lp[2] * lp[3])) + lam_init
    ao = diff_attention(aq, ak, av, lam, t5_bias)
    ao = rms_norm(ao, subln_w) * (1.0 - lam_init)
    ao = ao.transpose(0, 2, 1, 3).reshape(Bsz, T, A_V)
    bqkv = causal_conv_silu(bqkv, conv_w)
    bq, bk, bv = jnp.split(bqkv, 3, axis=-1)
    heads = lambda t: t.reshape(Bsz, T, B_HEADS, B_HEAD_DIM).transpose(0, 2, 1, 3)
    bq, bk, bv = l2_norm(heads(bq)), l2_norm(heads(bk)), heads(bv)
    beta = jax.nn.sigmoid(bb.astype(jnp.float32)).transpose(0, 2, 1)
    g = (-jnp.exp(a_log.astype(jnp.float32))
         * jax.nn.softplus(ba.astype(jnp.float32) + dt_bias.astype(jnp.float32))).transpose(0, 2, 1)
    bo = gated_delta_rule(bq, bk, bv, g, beta).astype(x.dtype)
    bo = rms_norm(bo, gdn_norm_w) * jax.nn.silu(heads(bz))
    bo = bo.transpose(0, 2, 1, 3).reshape(Bsz, T, B_W)
    return jnp.concatenate([ao, bo], axis=-1) @ w_out


def odd_mixer(x, w_in, w_out, qk_norm_w, forget_b):
    Bsz, T, _ = x.shape
    proj = x @ w_in
    q, k, v, gate, f = jnp.split(proj, ODD_SPLITS, axis=-1)
    heads = lambda t: t.reshape(Bsz, T, C_HEADS, C_HEAD_DIM).transpose(0, 2, 1, 3)
    q = rms_norm(heads(q), qk_norm_w[0])
    k = rms_norm(heads(k), qk_norm_w[1])
    logf = jax.nn.log_sigmoid(f.astype(jnp.float32) + forget_b.astype(jnp.float32)).transpose(0, 2, 1)
    o = forgetting_attention(q, k, heads(v), logf)
    o = o.transpose(0, 2, 1, 3).reshape(Bsz, T, C_W) * jax.nn.sigmoid(gate)
    return o @ w_out


def moe_ffn(x, router_w, router_b, w_gu, b_gu, w_down, b_down):
    Bsz, T, D = x.shape
    xt = x.reshape(-1, D)
    n_tok = xt.shape[0]
    logits = (xt @ router_w + router_b).astype(jnp.float32)
    top_val, top_idx = lax.top_k(logits, TOP_K)
    gates = jax.nn.softmax(top_val, axis=-1)
    n_assign = n_tok * TOP_K
    flat_e = top_idx.reshape(-1)
    flat_tok = jnp.repeat(jnp.arange(n_tok, dtype=jnp.int32), TOP_K)
    order = jnp.argsort(flat_e)
    e_sorted, tok_sorted, gate_sorted = flat_e[order], flat_tok[order], gates.reshape(-1)[order]
    counts = jnp.bincount(flat_e, length=N_EXPERTS)
    padded = (counts + MOE_BLOCK - 1) // MOE_BLOCK * MOE_BLOCK
    start = jnp.cumsum(counts) - counts
    pad_end = jnp.cumsum(padded)
    pad_start = pad_end - padded
    dest = pad_start[e_sorted] + jnp.arange(n_assign) - start[e_sorted]
    n_blocks = -(-n_assign // MOE_BLOCK) + N_EXPERTS
    n_rows = n_blocks * MOE_BLOCK
    row_tok = jnp.full((n_rows,), n_tok, jnp.int32).at[dest].set(tok_sorted)
    block_expert = jnp.minimum(jnp.searchsorted(pad_end, jnp.arange(n_blocks) * MOE_BLOCK, side='right'),
                               N_EXPERTS - 1)
    x_pad = jnp.concatenate([xt, jnp.zeros((1, D), xt.dtype)], axis=0)
    xb = x_pad[row_tok].reshape(n_blocks, MOE_BLOCK, D)

    def expert_block(args):
        xblk, e = args
        h = xblk @ w_gu[e] + b_gu[e]
        g, u = jnp.split(h, 2, axis=-1)
        g = jnp.minimum(g, SWIGLU_LIMIT)
        u = jnp.clip(u, -SWIGLU_LIMIT, SWIGLU_LIMIT)
        act = g * jax.nn.sigmoid(SWIGLU_ALPHA * g) * (u + 1.0)
        return act @ w_down[e] + b_down[e]

    yb = lax.map(expert_block, (xb, block_expert)).reshape(n_rows, D)
    y_assign = yb[dest] * gate_sorted[:, None].astype(yb.dtype)
    out = jnp.zeros_like(xt).at[tok_sorted].add(y_assign)
    return out.reshape(Bsz, T, D)


def setup_inputs(seed: int = 0) -> dict:
    key = jax.random.key(seed)
    ks = jax.random.split(key, 24)
    f32 = jnp.float32
    nrm = lambda k, shape, s: jax.random.normal(k, shape, f32) * s
    even_scale = jnp.ones((EVEN_IN,), f32)
    even_scale = even_scale.at[A_Q + A_K:A_Q + A_K + A_V].set(DEEPNORM_BETA)
    bv0 = A_Q + A_K + A_V + 2 * B_W
    even_scale = even_scale.at[bv0:bv0 + B_W].set(DEEPNORM_BETA)
    odd_scale = jnp.ones((ODD_IN,), f32).at[2 * C_W:3 * C_W].set(DEEPNORM_BETA)
    dt = jnp.exp(jax.random.uniform(ks[8], (N_EVEN, B_HEADS), f32, math.log(1e-3), math.log(1e-1)))
    return {
        'x': jax.random.normal(ks[0], (BATCH, SEQ, D_MODEL), f32),
        't5_bias': nrm(ks[1], (T5_BUCKETS, A_HEADS), 0.5),
        'even_w_in': nrm(ks[2], (N_EVEN, D_MODEL, EVEN_IN), D_MODEL ** -0.5) * even_scale,
        'even_w_out': nrm(ks[3], (N_EVEN, EVEN_OUT, D_MODEL), EVEN_OUT ** -0.5 * DEEPNORM_BETA),
        'diff_lambda': nrm(ks[4], (N_EVEN, 4, A_HEAD_DIM), 0.1),
        'diff_subln_w': 1.0 + nrm(ks[5], (N_EVEN, A_V_DIM), 0.02),
        'gdn_conv_w': nrm(ks[6], (N_EVEN, CONV_WIDTH, 3 * B_W), CONV_WIDTH ** -0.5),
        'gdn_a_log': jnp.log(jax.random.uniform(ks[7], (N_EVEN, B_HEADS), f32, 1.0, 16.0)),
        'gdn_dt_bias': dt + jnp.log(-jnp.expm1(-dt)),
        'gdn_norm_w': 1.0 + nrm(ks[9], (N_EVEN, B_HEAD_DIM), 0.02),
        'odd_w_in': nrm(ks[10], (N_ODD, D_MODEL, ODD_IN), D_MODEL ** -0.5) * odd_scale,
        'odd_w_out': nrm(ks[11], (N_ODD, ODD_OUT, D_MODEL), ODD_OUT ** -0.5 * DEEPNORM_BETA),
        'fox_qk_norm_w': 1.0 + nrm(ks[12], (N_ODD, 2, C_HEAD_DIM), 0.02),
        'fox_forget_b': 2.0 + nrm(ks[13], (N_ODD, C_HEADS), 0.1),
        'router_w': nrm(ks[14], (DEPTH, D_MODEL, N_EXPERTS), D_MODEL ** -0.5),
        'router_b': nrm(ks[15], (DEPTH, N_EXPERTS), 0.01),
        'moe_w_gate_up': nrm(ks[16], (DEPTH, N_EXPERTS, D_MODEL, 2 * D_FF), D_MODEL ** -0.5),
        'moe_b_gate_up': nrm(ks[17], (DEPTH, N_EXPERTS, 2 * D_FF), 0.01),
        'moe_w_down': nrm(ks[18], (DEPTH, N_EXPERTS, D_FF, D_MODEL), D_FF ** -0.5 * DEEPNORM_BETA),
        'moe_b_down': nrm(ks[19], (DEPTH, N_EXPERTS, D_MODEL), 0.01),
        'ln_mix_g': 1.0 + nrm(ks[20], (DEPTH, D_MODEL), 0.02),
        'ln_mix_b': nrm(ks[21], (DEPTH, D_MODEL), 0.02),
        'ln_ffn_g': 1.0 + nrm(ks[22], (DEPTH, D_MODEL), 0.02),
        'ln_ffn_b': nrm(ks[23], (DEPTH, D_MODEL), 0.02),
    }


def reference(x, t5_bias, even_w_in, even_w_out, diff_lambda, diff_subln_w, gdn_conv_w, gdn_a_log,
              gdn_dt_bias, gdn_norm_w, odd_w_in, odd_w_out, fox_qk_norm_w, fox_forget_b,
              router_w, router_b, moe_w_gate_up, moe_b_gate_up, moe_w_down, moe_b_down,
              ln_mix_g, ln_mix_b, ln_ffn_g, ln_ffn_b):
    for layer in range(DEPTH):
        i = layer // 2
        if layer % 2 == 0:
            lam_init = 0.8 - 0.6 * math.exp(-0.3 * layer)
            h = even_mixer(x, even_w_in[i], even_w_out[i], diff_lambda[i], diff_subln_w[i],
                           gdn_conv_w[i], gdn_a_log[i], gdn_dt_bias[i], gdn_norm_w[i], t5_bias, lam_init)
        else:
            h = odd_mixer(x, odd_w_in[i], odd_w_out[i], fox_qk_norm_w[i], fox_forget_b[i])
        x = layer_norm(DEEPNORM_ALPHA * x + h, ln_mix_g[layer], ln_mix_b[layer])
        h = moe_ffn(x, router_w[layer], router_b[layer], moe_w_gate_up[layer], moe_b_gate_up[layer],
                    moe_w_down[layer], moe_b_down[layer])
        x = layer_norm(DEEPNORM_ALPHA * x + h, ln_ffn_g[layer], ln_ffn_b[layer])
    return x
```

```python
import functools
import math

import numpy as np
import jax
import jax.numpy as jnp
from jax import lax
from jax.experimental import pallas as pl
from jax.experimental.pallas import tpu as pltpu

F32 = jnp.float32
BF16 = jnp.bfloat16
I32 = jnp.int32
HIGHEST = lax.Precision.HIGHEST

D_MODEL = 1024
DEPTH = 4
A_HEADS = 4
A_HEAD_DIM = 64
B_HEADS = 4
B_HEAD_DIM = 128
CONV_WIDTH = 4
C_HEADS = 16
C_HEAD_DIM = 64
T5_BUCKETS = 32
T5_MAX_DISTANCE = 2048
N_EXPERTS = 32
TOP_K = 4
D_FF = D_MODEL
SWIGLU_LIMIT = 7.0
SWIGLU_ALPHA = 1.702
DEEPNORM_ALPHA = (2 * DEPTH) ** 0.25
LN_EPS = 1e-5
RMS_EPS = 1e-6

LANES = 128
SUBLANES = 8
VMEM_LIMIT = 56 * 1024 * 1024

ATTN_TILE = 512
GDN_CHUNK = 128
GDN_TILE = 512
MOE_ROWS = 256
NEG_INF = float("-inf")


def _cparams(sem, vmem=VMEM_LIMIT):
    return pltpu.CompilerParams(dimension_semantics=sem, vmem_limit_bytes=vmem)


def _dot(a, b, **kw):
    return jnp.dot(a, b, preferred_element_type=F32, **kw)


def _dot_nt(a, b, **kw):
    return lax.dot_general(a, b, (((1,), (1,)), ((), ())), preferred_element_type=F32, **kw)


def _dot_tn(a, b, **kw):
    return lax.dot_general(a, b, (((0,), (0,)), ((), ())), preferred_element_type=F32, **kw)


def _sigmoid(x):
    return 1.0 / (1.0 + jnp.exp(-x))


def _softplus(x):
    return jnp.maximum(x, 0.0) + jnp.log(1.0 + jnp.exp(-jnp.abs(x)))


def _layer_norm(xf, g, b):
    mu = jnp.mean(xf, axis=-1, keepdims=True)
    xc = xf - mu
    var = jnp.mean(xc * xc, axis=-1, keepdims=True)
    return xc * lax.rsqrt(var + LN_EPS) * g + b


def _proj_body(x_ref, w_ref, o_ref, xb_ref):
    @pl.when(pl.program_id(1) == 0)
    def _():
        xb_ref[...] = x_ref[...].astype(BF16)

    o_ref[...] = _dot(xb_ref[...], w_ref[...].astype(BF16)).astype(o_ref.dtype)


def _proj(x, w, tm=1024, tn=512):
    n, k = x.shape
    m = w.shape[1]
    tm = min(tm, n)
    return pl.pallas_call(
        _proj_body,
        grid=(n // tm, m // tn),
        in_specs=[pl.BlockSpec((tm, k), lambda i, j: (i, 0)),
                  pl.BlockSpec((k, tn), lambda i, j: (0, j))],
        out_specs=pl.BlockSpec((tm, tn), lambda i, j: (i, j)),
        out_shape=jax.ShapeDtypeStruct((n, m), BF16),
        scratch_shapes=[pltpu.VMEM((tm, k), BF16)],
        compiler_params=_cparams(("arbitrary", "arbitrary")),
        name="proj",
    )(x, w)


def _gates_body(mode, tm, steps_per_seq, x_ref, w_ref, p0_ref, p1_ref, o_ref, carry_ref):
    z = _dot(x_ref[...], w_ref[...], precision=HIGHEST)
    ri = lax.broadcasted_iota(I32, (tm, tm), 0)
    ci = lax.broadcasted_iota(I32, (tm, tm), 1)
    if mode == "even":
        lane = lax.broadcasted_iota(I32, (tm, LANES), 1)
        beta = _sigmoid(z)
        g = -jnp.exp(p0_ref[...]) * _softplus(z + p1_ref[...])
        shift = int(math.log2(GDN_CHUNK))
        same_chunk = jnp.right_shift(ri, shift) == jnp.right_shift(ci, shift)
        tri = jnp.where(jnp.logical_and(same_chunk, ci <= ri), 1.0, 0.0)
        gc = _dot(tri, g, precision=HIGHEST)
        o_ref[...] = jnp.where(lane < B_HEADS, beta, gc)
    else:
        @pl.when(pl.program_id(0) % steps_per_seq == 0)
        def _():
            carry_ref[...] = jnp.zeros_like(carry_ref)

        logf = -_softplus(-(z + p0_ref[...]))
        tri = jnp.where(ci <= ri, 1.0, 0.0)
        cum = _dot(tri, logf, precision=HIGHEST) + carry_ref[0:1, :]
        o_ref[...] = cum
        carry_ref[...] = jnp.broadcast_to(cum[tm - 1:tm, :], carry_ref.shape)


def _gates(mode, x, w_small, p0, p1, seq_len, tm=512):
    n, k = x.shape
    tm = min(tm, seq_len)
    row = pl.BlockSpec((1, LANES), lambda i: (0, 0))
    return pl.pallas_call(
        functools.partial(_gates_body, mode, tm, seq_len // tm),
        grid=(n // tm,),
        in_specs=[pl.BlockSpec((tm, k), lambda i: (i, 0)),
                  pl.BlockSpec((k, LANES), lambda i: (0, 0)), row, row],
        out_specs=pl.BlockSpec((tm, LANES), lambda i: (i, 0)),
        out_shape=jax.ShapeDtypeStruct((n, LANES), F32),
        scratch_shapes=[pltpu.VMEM((SUBLANES, LANES), F32)],
        compiler_params=_cparams(("arbitrary",)),
        name="gates_" + mode,
    )(x, w_small, p0, p1)


def _pad_lanes(a):
    return jnp.pad(a, [(0, 0)] * (a.ndim - 1) + [(0, LANES - a.shape[-1])])


def _qknorm_body(x_ref, w_ref, o_ref):
    x = x_ref[...].astype(F32)
    lane = lax.broadcasted_iota(I32, x.shape, 1)
    lo = lane < C_HEAD_DIM
    ss = x * x
    s_lo = jnp.sum(jnp.where(lo, ss, 0.0), axis=-1, keepdims=True)
    s_hi = jnp.sum(jnp.where(lo, 0.0, ss), axis=-1, keepdims=True)
    inv = jnp.where(lo, lax.rsqrt(s_lo / C_HEAD_DIM + RMS_EPS), lax.rsqrt(s_hi / C_HEAD_DIM + RMS_EPS))
    o_ref[...] = (x * inv * w_ref[...]).astype(o_ref.dtype)


def _qknorm(proj, w2, tm=1024):
    n = proj.shape[0]
    tm = min(tm, n)
    nblk = 2 * C_HEADS * C_HEAD_DIM // LANES
    return pl.pallas_call(
        _qknorm_body,
        grid=(n // tm, nblk),
        in_specs=[pl.BlockSpec((tm, LANES), lambda i, c: (i, c)),
                  pl.BlockSpec((None, 1, LANES), lambda i, c: (c // (nblk // 2), 0, 0))],
        out_specs=pl.BlockSpec((tm, LANES), lambda i, c: (i, c)),
        out_shape=jax.ShapeDtypeStruct((n, nblk * LANES), BF16),
        compiler_params=_cparams(("parallel", "parallel")),
        name="fox_qknorm",
    )(proj, w2)


def _attn_body(mode, t, lam_init, ii_ref, jj_ref, q_ref, k_ref, v_ref, *rest):
    if mode == "diff":
        strip_ref, far_ref, lam_ref, subln_ref, o_ref, qs_ref, m_ref, l_ref, acc_ref = rest
    else:
        ck_ref, c0_ref, gate_ref, o_ref, qs_ref, m_ref, l_ref, acc_ref = rest
    p = pl.program_id(2)
    i = ii_ref[p]
    j = jj_ref[p]

    @pl.when(j == 0)
    def _():
        q = q_ref[...]
        if mode == "diff":
            q = q * (A_HEAD_DIM ** -0.5)
        lane = lax.broadcasted_iota(I32, q.shape, 1)
        zero = jnp.zeros_like(q)
        qs_ref[0:t, :] = jnp.where(lane < 64, q, zero)
        qs_ref[t:2 * t, :] = jnp.where(lane < 64, zero, q)
        m_ref[...] = jnp.full(m_ref.shape, NEG_INF, F32)
        l_ref[...] = jnp.zeros_like(l_ref)
        acc_ref[...] = jnp.zeros_like(acc_ref)

    def scores():
        return _dot_nt(qs_ref[...], k_ref[...])

    def update(s):
        m_prev = m_ref[...]
        m_new = jnp.maximum(m_prev, jnp.max(s, axis=1, keepdims=True))
        alpha = jnp.exp(m_prev - m_new)
        pr = jnp.exp(s - m_new)
        l_ref[...] = alpha * l_ref[...] + jnp.sum(pr, axis=1, keepdims=True)
        acc_ref[...] = alpha * acc_ref[...] + _dot(pr.astype(BF16), v_ref[...])
        m_ref[...] = m_new

    def causal(s):
        r = lax.broadcasted_iota(I32, (2 * t, t), 0)
        c = lax.broadcasted_iota(I32, (2 * t, t), 1)
        r = jnp.where(r >= t, r - t, r)
        return jnp.where(r >= c, s, NEG_INF)

    if mode == "diff":
        def toeplitz():
            x = jnp.broadcast_to(strip_ref[...], (t, 2 * t))
            tile = pltpu.roll(x, t + 1, 1, stride=1, stride_axis=0)[:, :t]
            return jnp.concatenate([tile, tile], axis=0)

        d = i - j
        near = d * t - (t - 1) < T5_MAX_DISTANCE

        @pl.when(d == 0)
        def _():
            update(causal(scores() + toeplitz()))

        @pl.when(jnp.logical_and(d > 0, near))
        def _():
            update(scores() + toeplitz())

        @pl.when(jnp.logical_not(near))
        def _():
            update(scores() + far_ref[0:1, 0:1])
    else:
        def biased():
            c0 = c0_ref[:, 0:1]
            b = c0 - ck_ref[...]
            s = scores()
            return jnp.concatenate([s[0:t] + b[0:1], s[t:2 * t] + b[1:2]], axis=0)

        @pl.when(j == i)
        def _():
            update(causal(biased()))

        @pl.when(j != i)
        def _():
            update(biased())

    @pl.when(j == i)
    def _():
        o = acc_ref[...] * (1.0 / l_ref[...])
        if mode == "diff":
            lp = lam_ref[...]
            lam = (jnp.exp(jnp.sum(lp[0:1] * lp[1:2], axis=-1, keepdims=True))
                   - jnp.exp(jnp.sum(lp[2:3] * lp[3:4], axis=-1, keepdims=True)) + lam_init)
            dlt = o[0:t] - lam * o[t:2 * t]
            ms = jnp.mean(dlt * dlt, axis=-1, keepdims=True)
            out = dlt * lax.rsqrt(ms + RMS_EPS) * subln_ref[...] * (1.0 - lam_init)
        else:
            lane = lax.broadcasted_iota(I32, (t, LANES), 1)
            out = jnp.where(lane < 64, o[0:t], o[t:2 * t]) * _sigmoid(gate_ref[...].astype(F32))
        o_ref[...] = out.astype(o_ref.dtype)


def _causal_pairs(nq):
    ii, jj = [], []
    for i in range(nq):
        for j in range(i + 1):
            ii.append(i)
            jj.append(j)
    return jnp.asarray(np.array(ii, np.int32)), jnp.asarray(np.array(jj, np.int32))


def _attn_scratch(t):
    return [pltpu.VMEM((2 * t, LANES), BF16), pltpu.VMEM((2 * t, 1), F32),
            pltpu.VMEM((2 * t, 1), F32), pltpu.VMEM((2 * t, LANES), F32)]


def _diff_attention(proj, bsz, seq, strips, far, lam_params, subln_w, lam_init, t):
    nq = seq // t
    ii, jj = _causal_pairs(nq)
    nd = strips.shape[1]
    h_ = A_HEADS
    grid_spec = pltpu.PrefetchScalarGridSpec(
        num_scalar_prefetch=2,
        grid=(bsz, h_, int(ii.shape[0])),
        in_specs=[
            pl.BlockSpec((t, LANES), lambda b, h, p, ii, jj: (b * nq + ii[p], h)),
            pl.BlockSpec((t, LANES), lambda b, h, p, ii, jj: (b * nq + jj[p], h_ + h)),
            pl.BlockSpec((t, LANES), lambda b, h, p, ii, jj: (b * nq + jj[p], 2 * h_ + h)),
            pl.BlockSpec((None, None, 1, 2 * t),
                         lambda b, h, p, ii, jj: (h, jnp.minimum(ii[p] - jj[p], nd - 1), 0, 0)),
            pl.BlockSpec((None, 1, LANES), lambda b, h, p, ii, jj: (h, 0, 0)),
            pl.BlockSpec(lam_params.shape, lambda b, h, p, ii, jj: (0, 0)),
            pl.BlockSpec((1, LANES), lambda b, h, p, ii, jj: (0, 0)),
        ],
        out_specs=pl.BlockSpec((t, LANES), lambda b, h, p, ii, jj: (b * nq + ii[p], h)),
        scratch_shapes=_attn_scratch(t),
    )
    return pl.pallas_call(
        functools.partial(_attn_body, "diff", t, lam_init),
        grid_spec=grid_spec,
        out_shape=jax.ShapeDtypeStruct((bsz * seq, h_ * LANES), BF16),
        compiler_params=_cparams(("parallel", "parallel", "arbitrary")),
        name="diff_attn",
    )(ii, jj, proj, proj, proj, strips, far, lam_params, subln_w)


def _fox_attention(qk, proj, cum_t, bsz, seq, t):
    nq = seq // t
    ii, jj = _causal_pairs(nq)
    hp = C_HEADS // 2
    grid_spec = pltpu.PrefetchScalarGridSpec(
        num_scalar_prefetch=2,
        grid=(bsz, hp, int(ii.shape[0])),
        in_specs=[
            pl.BlockSpec((t, LANES), lambda b, h, p, ii, jj: (b * nq + ii[p], h)),
            pl.BlockSpec((t, LANES), lambda b, h, p, ii, jj: (b * nq + jj[p], hp + h)),
            pl.BlockSpec((t, LANES), lambda b, h, p, ii, jj: (b * nq + jj[p], 2 * hp + h)),
            pl.BlockSpec((None, None, 2, t), lambda b, h, p, ii, jj: (b, h, 0, jj[p])),
            pl.BlockSpec((None, None, 2, t), lambda b, h, p, ii, jj: (b, h, 0, ii[p])),
            pl.BlockSpec((t, LANES), lambda b, h, p, ii, jj: (b * nq + ii[p], 3 * hp + h)),
        ],
        out_specs=pl.BlockSpec((t, LANES), lambda b, h, p, ii, jj: (b * nq + ii[p], h)),
        scratch_shapes=_attn_scratch(t),
    )
    return pl.pallas_call(
        functools.partial(_attn_body, "fox", t, 0.0),
        grid_spec=grid_spec,
        out_shape=jax.ShapeDtypeStruct((bsz * seq, hp * LANES), BF16),
        compiler_params=_cparams(("parallel", "parallel", "arbitrary")),
        name="fox_attn",
    )(ii, jj, qk, qk, proj, cum_t, cum_t, proj)


def _t5_bias_by_distance(t5_bias, seq):
    n = jnp.arange(seq, dtype=I32)
    max_exact = T5_BUCKETS // 2
    nf = jnp.maximum(n, 1).astype(F32)
    large = max_exact + (jnp.log(nf / max_exact) / math.log(T5_MAX_DISTANCE / max_exact)
                         * (T5_BUCKETS - max_exact)).astype(I32)
    large = jnp.minimum(large, T5_BUCKETS - 1)
    bucket = jnp.where(n < max_exact, n, large)
    return t5_bias.astype(F32).T[:, bucket]


def _t5_strips(t5_bias, seq, t):
    vec = _t5_bias_by_distance(t5_bias, seq)
    nd = 1
    while nd * t - (t - 1) < T5_MAX_DISTANCE and nd < seq // t:
        nd += 1
    d = np.arange(nd)[:, None]
    c = np.arange(2 * t)[None, :]
    rel = d * t + t - 1 - c
    rel = np.clip(rel, 0, seq - 1)
    strips = vec[:, jnp.asarray(rel.astype(np.int32))]
    far = jnp.broadcast_to(t5_bias.astype(F32).T[:, T5_BUCKETS - 1][:, None, None], (t5_bias.shape[1], 1, LANES))
    return strips[:, :, None, :], far


def _gdn_body(tb, xq_ref, xk_ref, xv_ref, z_ref, gcol_ref, grow_ref, cw_ref, nw_ref, o_ref, s_ref, carry_ref):
    h = pl.program_id(1)
    c_ = GDN_CHUNK

    @pl.when(pl.program_id(2) == 0)
    def _():
        s_ref[...] = jnp.zeros_like(s_ref)
        carry_ref[...] = jnp.zeros_like(carry_ref)

    cw = cw_ref[...]

    def conv_silu(idx, x_ref):
        x = x_ref[...].astype(F32)
        xe = jnp.concatenate([carry_ref[idx], x], axis=0)
        y = cw[4 * idx + 3:4 * idx + 4] * x
        for tap in range(CONV_WIDTH - 1):
            shift = CONV_WIDTH - 1 - tap
            y = y + cw[4 * idx + tap:4 * idx + tap + 1] * pltpu.roll(xe, shift, 0)[SUBLANES:]
        carry_ref[idx] = x[tb - SUBLANES:]
        return y * _sigmoid(y)

    q = conv_silu(0, xq_ref)
    k = conv_silu(1, xk_ref)
    v = conv_silu(2, xv_ref)
    q = q * lax.rsqrt(jnp.sum(q * q, axis=-1, keepdims=True) + RMS_EPS) * (B_HEAD_DIM ** -0.5)
    k = k * lax.rsqrt(jnp.sum(k * k, axis=-1, keepdims=True) + RMS_EPS)

    gates = gcol_ref[...]
    lane = lax.broadcasted_iota(I32, gates.shape, 1)
    beta = jnp.sum(jnp.where(lane == h, gates, 0.0), axis=1, keepdims=True)
    gc = jnp.sum(jnp.where(lane == B_HEADS + h, gates, 0.0), axis=1, keepdims=True)
    gc_row = grow_ref[1:2, :]
    egc = jnp.exp(gc)
    kb = k * beta
    rhs = jnp.concatenate([v * beta, kb * egc], axis=1)
    qd = q * egc

    ri = lax.broadcasted_iota(I32, (c_, c_), 0)
    ci = lax.broadcasted_iota(I32, (c_, c_), 1)
    state = s_ref[...]
    outs = []
    for c in range(tb // c_):
        sl = slice(c * c_, (c + 1) * c_)
        gcc = gc[sl]
        decay = jnp.exp(jnp.where(ri >= ci, gcc - gc_row[:, sl], NEG_INF))
        kc = k[sl].astype(BF16)
        kk = _dot_nt(kb[sl].astype(BF16), kc)
        qk = _dot_nt(q[sl].astype(BF16), kc) * decay
        x = -(kk * jnp.where(ri > ci, decay, 0.0))
        r = x
        pw = x
        for _ in range(int(math.log2(c_)) - 1):
            pwb = pw.astype(BF16)
            pw = _dot(pwb, pwb)
            r = r + pw + _dot(r.astype(BF16), pw.astype(BF16))
        sol = rhs[sl] + _dot(r.astype(BF16), rhs[sl].astype(BF16))
        u = sol[:, :B_HEAD_DIM]
        w = sol[:, B_HEAD_DIM:]
        g_last = gcc[c_ - 1:c_]
        kd = k[sl] * jnp.exp(g_last - gcc)
        sb = state.astype(BF16)
        v_new = u - _dot(w.astype(BF16), sb)
        vnb = v_new.astype(BF16)
        outs.append(_dot(qd[sl].astype(BF16), sb) + _dot(qk.astype(BF16), vnb))
        state = state * jnp.exp(g_last) + _dot_tn(kd.astype(BF16), vnb)
    s_ref[...] = state
    o = jnp.concatenate(outs, axis=0)
    o = o * lax.rsqrt(jnp.mean(o * o, axis=-1, keepdims=True) + RMS_EPS) * nw_ref[...]
    z = z_ref[...].astype(F32)
    o_ref[...] = (o * (z * _sigmoid(z))).astype(o_ref.dtype)


def _gdn(proj, gates, gates_row, conv_w, norm_w, bsz, seq, tb):
    nt = seq // tb
    h_ = B_HEADS
    col0 = 3 * A_HEADS
    blk = lambda off: pl.BlockSpec((tb, LANES), lambda b, h, s: (b * nt + s, col0 + off * h_ + h))
    return pl.pallas_call(
        functools.partial(_gdn_body, tb),
        grid=(bsz, h_, nt),
        in_specs=[blk(0), blk(1), blk(2), blk(3),
                  pl.BlockSpec((tb, LANES), lambda b, h, s: (b * nt + s, 0)),
                  pl.BlockSpec((None, 2, tb), lambda b, h, s: (b * h_ + h, 0, s)),
                  pl.BlockSpec((None, 3 * CONV_WIDTH, LANES), lambda b, h, s: (h, 0, 0)),
                  pl.BlockSpec((1, LANES), lambda b, h, s: (0, 0))],
        out_specs=pl.BlockSpec((tb, LANES), lambda b, h, s: (b * nt + s, h)),
        out_shape=jax.ShapeDtypeStruct((bsz * seq, h_ * LANES), BF16),
        scratch_shapes=[pltpu.VMEM((B_HEAD_DIM, B_HEAD_DIM), F32),
                        pltpu.VMEM((3, SUBLANES, LANES), F32)],
        compiler_params=_cparams(("parallel", "parallel", "arbitrary")),
        name="gdn",
    )(proj, proj, proj, proj, gates, gates_row, conv_w, norm_w)


def _outproj_ln_body(a_ref, b_ref, w_ref, x_ref, g_ref, bb_ref, o_ref, wb_ref):
    @pl.when(pl.program_id(0) == 0)
    def _():
        wb_ref[...] = w_ref[...].astype(BF16)

    half = a_ref.shape[1]
    hmix = _dot(a_ref[...], wb_ref[0:half, :]) + _dot(b_ref[...], wb_ref[half:, :])
    o_ref[...] = _layer_norm(DEEPNORM_ALPHA * x_ref[...] + hmix, g_ref[...], bb_ref[...])


def _outproj_ln(a, a_blk, b, b_blk, w, x, g, bb, tm=512):
    n, d = x.shape
    tm = min(tm, n)
    half = d // 2
    row = pl.BlockSpec((1, d), lambda i: (0, 0))
    return pl.pallas_call(
        _outproj_ln_body,
        grid=(n // tm,),
        in_specs=[pl.BlockSpec((tm, half), lambda i: (i, a_blk)),
                  pl.BlockSpec((tm, half), lambda i: (i, b_blk)),
                  pl.BlockSpec((d, d), lambda i: (0, 0)),
                  pl.BlockSpec((tm, d), lambda i: (i, 0)), row, row],
        out_specs=pl.BlockSpec((tm, d), lambda i: (i, 0)),
        out_shape=jax.ShapeDtypeStruct((n, d), F32),
        scratch_shapes=[pltpu.VMEM((d, d), BF16)],
        compiler_params=_cparams(("arbitrary",)),
        name="outproj_ln",
    )(a, b, w, x, g, bb)


def _router_body(tm, x_ref, w_ref, b_ref, idx_ref, gate_ref, rank_ref, cnt_ref, carry_ref):
    @pl.when(pl.program_id(0) == 0)
    def _():
        carry_ref[...] = jnp.zeros_like(carry_ref)

    logits = _dot(x_ref[...], w_ref[...], precision=HIGHEST) + b_ref[...]
    lg = jnp.transpose(logits)[0:N_EXPERTS, :]
    e_iota = lax.broadcasted_iota(I32, (N_EXPERTS, tm), 0).astype(F32)
    vals, idxs, hots = [], [], []
    for _ in range(TOP_K):
        m = jnp.max(lg, axis=0, keepdims=True)
        idx = jnp.min(jnp.where(lg == m, e_iota, float(N_EXPERTS)), axis=0, keepdims=True)
        hot = e_iota == idx
        lg = jnp.where(hot, NEG_INF, lg)
        vals.append(m)
        idxs.append(idx)
        hots.append(hot)
    es = [jnp.exp(v - vals[0]) for v in vals]
    den = es[0] + es[1] + es[2] + es[3]
    sel = jnp.zeros((N_EXPERTS, tm), F32)
    for hot in hots:
        sel = sel + jnp.where(hot, 1.0, 0.0)
    before = jnp.where(lax.broadcasted_iota(I32, (tm, tm), 0) < lax.broadcasted_iota(I32, (tm, tm), 1),
                       1.0, 0.0).astype(BF16)
    cum = _dot(sel.astype(BF16), before) + carry_ref[:, 0:1]
    ranks = [jnp.sum(jnp.where(hot, cum, 0.0), axis=0, keepdims=True) for hot in hots]
    total = carry_ref[...] + jnp.sum(sel, axis=1, keepdims=True)
    carry_ref[...] = total
    idx_ref[...] = jnp.concatenate(idxs, axis=0).astype(I32)
    gate_ref[...] = jnp.concatenate([e / den for e in es], axis=0)
    rank_ref[...] = jnp.concatenate(ranks, axis=0).astype(I32)
    cnt_ref[...] = total


def _router(x, w_pad, b_pad, tm=512):
    n, d = x.shape
    tm = min(tm, n)
    out4 = lambda dt: jax.ShapeDtypeStruct((TOP_K, n), dt)
    blk4 = pl.BlockSpec((TOP_K, tm), lambda i: (0, i))
    return pl.pallas_call(
        functools.partial(_router_body, tm),
        grid=(n // tm,),
        in_specs=[pl.BlockSpec((tm, d), lambda i: (i, 0)),
                  pl.BlockSpec((d, LANES), lambda i: (0, 0)),
                  pl.BlockSpec((1, LANES), lambda i: (0, 0))],
        out_specs=[blk4, blk4, blk4, pl.BlockSpec((N_EXPERTS, LANES), lambda i: (0, 0))],
        out_shape=[out4(I32), out4(F32), out4(I32), jax.ShapeDtypeStruct((N_EXPERTS, LANES), F32)],
        scratch_shapes=[pltpu.VMEM((N_EXPERTS, LANES), F32)],
        compiler_params=_cparams(("arbitrary",)),
        name="moe_router",
    )(x, w_pad, b_pad)


def _dispatch_body(td, dest_ref, x_hbm, xg_in, xg_out, sem):
    del xg_in
    base = pl.program_id(0) * td

    def row_copy(src_row, dst_row):
        return pltpu.make_async_copy(x_hbm.at[pl.ds(src_row, 1)], xg_out.at[pl.ds(dst_row, 1)], sem)

    def issue(r, carry):
        for k in range(TOP_K):
            row_copy(base + r, dest_ref[k, r]).start()
        return carry

    def drain(r, carry):
        for k in range(TOP_K):
            row_copy(base + r, dest_ref[k, r]).wait()
        return carry

    lax.fori_loop(0, td, issue, 0)
    lax.fori_loop(0, td, drain, 0)


def _dispatch(x, dest, n_rows, td=1024):
    n, d = x.shape
    td = min(td, n)
    xg0 = jnp.zeros((n_rows, d), x.dtype)
    return pl.pallas_call(
        functools.partial(_dispatch_body, td),
        grid=(n // td,),
        in_specs=[pl.BlockSpec((TOP_K, td), lambda i: (0, i), memory_space=pltpu.SMEM),
                  pl.BlockSpec(memory_space=pl.ANY),
                  pl.BlockSpec(memory_space=pl.ANY)],
        out_specs=pl.BlockSpec(memory_space=pl.ANY),
        out_shape=jax.ShapeDtypeStruct((n_rows, d), x.dtype),
        scratch_shapes=[pltpu.SemaphoreType.DMA(())],
        input_output_aliases={2: 0},
        compiler_params=_cparams(("arbitrary",)),
        name="moe_dispatch",
    )(dest, x, xg0)


def _expert_body(be_ref, nu_ref, x_ref, wgu_ref, bgu_ref, wd_ref, bd_ref, o_ref, wgu_b, wd_b):
    i = pl.program_id(0)
    prev = be_ref[jnp.maximum(i - 1, 0)]
    fresh = jnp.logical_or(i == 0, be_ref[i] != prev)

    @pl.when(jnp.logical_and(i < nu_ref[0], fresh))
    def _():
        wgu_b[...] = wgu_ref[...].astype(BF16)
        wd_b[...] = wd_ref[...].astype(BF16)

    @pl.when(i < nu_ref[0])
    def _():
        hcat = _dot(x_ref[...].astype(BF16), wgu_b[...]) + bgu_ref[...]
        g = jnp.minimum(hcat[:, :D_FF], SWIGLU_LIMIT)
        u = jnp.clip(hcat[:, D_FF:], -SWIGLU_LIMIT, SWIGLU_LIMIT)
        act = g * _sigmoid(SWIGLU_ALPHA * g) * (u + 1.0)
        o_ref[...] = _dot(act.astype(BF16), wd_b[...]) + bd_ref[...]

    @pl.when(i >= nu_ref[0])
    def _():
        o_ref[...] = jnp.zeros_like(o_ref)


def _experts(xg, block_expert, n_used, w_gu, b_gu, w_down, b_down):
    n_rows, d = xg.shape
    nb = n_rows // MOE_ROWS
    grid_spec = pltpu.PrefetchScalarGridSpec(
        num_scalar_prefetch=2,
        grid=(nb,),
        in_specs=[
            pl.BlockSpec((MOE_ROWS, d), lambda i, be, nu: (i, 0)),
            pl.BlockSpec((None, d, 2 * D_FF), lambda i, be, nu: (be[i], 0, 0)),
            pl.BlockSpec((None, 1, 2 * D_FF), lambda i, be, nu: (be[i], 0, 0)),
            pl.BlockSpec((None, D_FF, d), lambda i, be, nu: (be[i], 0, 0)),
            pl.BlockSpec((None, 1, d), lambda i, be, nu: (be[i], 0, 0)),
        ],
        out_specs=pl.BlockSpec((MOE_ROWS, d), lambda i, be, nu: (i, 0)),
        scratch_shapes=[pltpu.VMEM((d, 2 * D_FF), BF16), pltpu.VMEM((D_FF, d), BF16)],
    )
    return pl.pallas_call(
        _expert_body,
        grid_spec=grid_spec,
        out_shape=jax.ShapeDtypeStruct((n_rows, d), F32),
        compiler_params=_cparams(("arbitrary",)),
        name="moe_experts",
    )(block_expert, n_used, xg, w_gu, b_gu[:, None, :], w_down, b_down[:, None, :])


def _combine_ln_body(tc, dest_ref, yg_hbm, gate_ref, x_ref, g_ref, b_ref, o_ref, ybuf, sem):
    def row_copy(k, r):
        return pltpu.make_async_copy(yg_hbm.at[pl.ds(dest_ref[k, r], 1)], ybuf.at[k, pl.ds(r, 1)], sem)

    def issue(r, carry):
        for k in range(TOP_K):
            row_copy(k, r).start()
        return carry

    def drain(r, carry):
        for k in range(TOP_K):
            row_copy(k, r).wait()
        return carry

    lax.fori_loop(0, tc, issue, 0)
    lax.fori_loop(0, tc, drain, 0)
    gate = gate_ref[...]
    hmoe = gate[:, 0:1] * ybuf[0]
    for k in range(1, TOP_K):
        hmoe = hmoe + gate[:, k:k + 1] * ybuf[k]
    o_ref[...] = _layer_norm(DEEPNORM_ALPHA * x_ref[...] + hmoe, g_ref[...], b_ref[...])


def _combine_ln(yg, dest, gate_t, x, g, b, tc=256):
    n, d = x.shape
    tc = min(tc, n)
    row = pl.BlockSpec((1, d), lambda i: (0, 0))
    return pl.pallas_call(
        functools.partial(_combine_ln_body, tc),
        grid=(n // tc,),
        in_specs=[pl.BlockSpec((TOP_K, tc), lambda i: (0, i), memory_space=pltpu.SMEM),
                  pl.BlockSpec(memory_space=pl.ANY),
                  pl.BlockSpec((tc, TOP_K), lambda i: (i, 0)),
                  pl.BlockSpec((tc, d), lambda i: (i, 0)), row, row],
        out_specs=pl.BlockSpec((tc, d), lambda i: (i, 0)),
        out_shape=jax.ShapeDtypeStruct((n, d), F32),
        scratch_shapes=[pltpu.VMEM((TOP_K, tc, d), F32), pltpu.SemaphoreType.DMA(())],
        compiler_params=_cparams(("arbitrary",)),
        name="moe_combine_ln",
    )(dest, yg, gate_t, x, g, b)


def _moe_ln(x, router_w, router_b, w_gu, b_gu, w_down, b_down, ln_g, ln_b):
    n, d = x.shape
    idx, gate, rank, cnt = _router(x, _pad_lanes(router_w), _pad_lanes(router_b[None, :]))
    counts = cnt[:, 0].astype(I32)
    padded = (counts + MOE_ROWS - 1) // MOE_ROWS * MOE_ROWS
    pad_end = jnp.cumsum(padded)
    pad_start = pad_end - padded
    dest = jnp.take(pad_start, idx) + rank
    n_blocks = -(-(n * TOP_K) // MOE_ROWS) + N_EXPERTS
    block_expert = jnp.minimum(
        jnp.searchsorted(pad_end, jnp.arange(n_blocks, dtype=I32) * MOE_ROWS, side="right"),
        N_EXPERTS - 1).astype(I32)
    n_used = (pad_end[-1:] // MOE_ROWS).astype(I32)
    xg = _dispatch(x, dest, n_blocks * MOE_ROWS)
    yg = _experts(xg, block_expert, n_used, w_gu, b_gu, w_down, b_down)
    return _combine_ln(yg, dest, gate.T, x, ln_g[None, :], ln_b[None, :])


def _even_mixer_ln(x, bsz, seq, w_in, w_out, lam_params, subln_w, conv_w, a_log, dt_bias, gdn_norm_w,
                   t5_bias, lam_init, ln_g, ln_b):
    main = 3 * A_HEADS * LANES + 4 * B_HEADS * B_HEAD_DIM
    proj = _proj(x, w_in[:, :main])
    zeros4 = jnp.zeros((B_HEADS,), F32)
    p0 = _pad_lanes(jnp.concatenate([zeros4, a_log.astype(F32)])[None, :])
    p1 = _pad_lanes(jnp.concatenate([zeros4, dt_bias.astype(F32)])[None, :])
    gates = _gates("even", x, _pad_lanes(w_in[:, main:]), p0, p1, seq)
    t = min(ATTN_TILE, seq)
    strips, far = _t5_strips(t5_bias, seq, t)
    ao = _diff_attention(proj, bsz, seq, strips, far, lam_params.astype(F32), subln_w[None, :].astype(F32),
                         lam_init, t)
    g8 = gates[:, :2 * B_HEADS].reshape(bsz, seq, 2, B_HEADS)
    gates_row = g8.transpose(0, 3, 2, 1).reshape(bsz * B_HEADS, 2, seq)
    cw = conv_w.astype(F32).reshape(CONV_WIDTH, 3, B_HEADS, B_HEAD_DIM).transpose(2, 1, 0, 3)
    cw = cw.reshape(B_HEADS, 3 * CONV_WIDTH, B_HEAD_DIM)
    bo = _gdn(proj, gates, gates_row, cw, gdn_norm_w[None, :].astype(F32), bsz, seq, min(GDN_TILE, seq))
    return _outproj_ln(ao, 0, bo, 0, w_out, x, ln_g[None, :], ln_b[None, :])


def _odd_mixer_ln(x, bsz, seq, w_in, w_out, qk_norm_w, forget_b, ln_g, ln_b):
    main = 4 * C_HEADS * C_HEAD_DIM
    proj = _proj(x, w_in[:, :main])
    fb = _pad_lanes(forget_b.astype(F32)[None, :])
    cum = _gates("odd", x, _pad_lanes(w_in[:, main:]), fb, fb, seq)
    cum_t = cum[:, :C_HEADS].reshape(bsz, seq, C_HEADS // 2, 2).transpose(0, 2, 3, 1)
    scale = jnp.asarray([C_HEAD_DIM ** -0.5, 1.0], F32)[:, None]
    w2 = jnp.tile(qk_norm_w.astype(F32) * scale, (1, 2))[:, None, :]
    qk = _qknorm(proj, w2)
    o = _fox_attention(qk, proj, cum_t, bsz, seq, min(ATTN_TILE, seq))
    return _outproj_ln(o, 0, o, 1, w_out, x, ln_g[None, :], ln_b[None, :])


def kernel(x, t5_bias, even_w_in, even_w_out, diff_lambda, diff_subln_w, gdn_conv_w, gdn_a_log, gdn_dt_bias, gdn_norm_w, odd_w_in, odd_w_out, fox_qk_norm_w, fox_forget_b, router_w, router_b, moe_w_gate_up, moe_b_gate_up, moe_w_down, moe_b_down, ln_mix_g, ln_mix_b, ln_ffn_g, ln_ffn_b):
    bsz, seq, d = x.shape
    xf = x.reshape(bsz * seq, d)
    for layer in range(DEPTH):
        i = layer // 2
        if layer % 2 == 0:
            lam_init = 0.8 - 0.6 * math.exp(-0.3 * layer)
            xf = _even_mixer_ln(xf, bsz, seq, even_w_in[i], even_w_out[i], diff_lambda[i], diff_subln_w[i],
                                gdn_conv_w[i], gdn_a_log[i], gdn_dt_bias[i], gdn_norm_w[i], t5_bias, lam_init,
                                ln_mix_g[layer], ln_mix_b[layer])
        else:
            xf = _odd_mixer_ln(xf, bsz, seq, odd_w_in[i], odd_w_out[i], fox_qk_norm_w[i], fox_forget_b[i],
                               ln_mix_g[layer], ln_mix_b[layer])
        xf = _moe_ln(xf, router_w[layer], router_b[layer], moe_w_gate_up[layer], moe_b_gate_up[layer],
                     moe_w_down[layer], moe_b_down[layer], ln_ffn_g[layer], ln_ffn_b[layer])
    return xf.reshape(bsz, seq, d)
```

```python
import functools
import math

import numpy as np
import jax
import jax.numpy as jnp
from jax import lax
from jax.experimental import pallas as pl
from jax.experimental.pallas import tpu as pltpu

F32 = jnp.float32
BF16 = jnp.bfloat16
I32 = jnp.int32
HIGHEST = lax.Precision.HIGHEST

D_MODEL = 1024
DEPTH = 4
A_HEADS = 4
A_HEAD_DIM = 64
B_HEADS = 4
B_HEAD_DIM = 128
CONV_WIDTH = 4
C_HEADS = 16
C_HEAD_DIM = 64
T5_BUCKETS = 32
T5_MAX_DISTANCE = 2048
N_EXPERTS = 32
TOP_K = 4
D_FF = D_MODEL
SWIGLU_LIMIT = 7.0
SWIGLU_ALPHA = 1.702
DEEPNORM_ALPHA = (2 * DEPTH) ** 0.25
LN_EPS = 1e-5
RMS_EPS = 1e-6

LANES = 128
SUBLANES = 8
VMEM_LIMIT = 56 * 1024 * 1024

ATTN_TILE = 1024
ATTN_ROWS = 1024
GDN_CHUNK = 128
GDN_TILE = 512
MOE_ROWS = 256
NEG_INF = float("-inf")
LOG2E = math.log2(math.e)


def _cparams(sem, vmem=VMEM_LIMIT):
    return pltpu.CompilerParams(dimension_semantics=sem, vmem_limit_bytes=vmem)


def _dot(a, b, **kw):
    return jnp.dot(a, b, preferred_element_type=F32, **kw)


def _dot_nt(a, b, **kw):
    return lax.dot_general(a, b, (((1,), (1,)), ((), ())), preferred_element_type=F32, **kw)


def _dot_tn(a, b, **kw):
    return lax.dot_general(a, b, (((0,), (0,)), ((), ())), preferred_element_type=F32, **kw)


def _sigmoid(x):
    return 1.0 / (1.0 + jnp.exp(-x))


def _softplus(x):
    return jnp.maximum(x, 0.0) + jnp.log(1.0 + jnp.exp(-jnp.abs(x)))


def _layer_norm(xf, g, b):
    mu = jnp.mean(xf, axis=-1, keepdims=True)
    xc = xf - mu
    var = jnp.mean(xc * xc, axis=-1, keepdims=True)
    return xc * lax.rsqrt(var + LN_EPS) * g + b


def _proj_body(x_ref, w_ref, cs_ref, o_ref, xb_ref):
    @pl.when(pl.program_id(1) == 0)
    def _():
        xb_ref[...] = x_ref[...].astype(BF16)

    o_ref[...] = (_dot(xb_ref[...], w_ref[...].astype(BF16)) * cs_ref[...]).astype(o_ref.dtype)


def _proj(x, w, col_scale, tm=1024, tn=512):
    n, k = x.shape
    m = w.shape[1]
    tm = min(tm, n)
    return pl.pallas_call(
        _proj_body,
        grid=(n // tm, m // tn),
        in_specs=[pl.BlockSpec((tm, k), lambda i, j: (i, 0)),
                  pl.BlockSpec((k, tn), lambda i, j: (0, j)),
                  pl.BlockSpec((1, tn), lambda i, j: (0, j))],
        out_specs=pl.BlockSpec((tm, tn), lambda i, j: (i, j)),
        out_shape=jax.ShapeDtypeStruct((n, m), BF16),
        scratch_shapes=[pltpu.VMEM((tm, k), BF16)],
        compiler_params=_cparams(("arbitrary", "arbitrary")),
        name="proj",
    )(x, w, col_scale)


def _gates_body(mode, tm, steps_per_seq, x_ref, w_ref, p0_ref, p1_ref, o_ref, carry_ref):
    z = _dot(x_ref[...], w_ref[...], precision=HIGHEST)
    ri = lax.broadcasted_iota(I32, (tm, tm), 0)
    ci = lax.broadcasted_iota(I32, (tm, tm), 1)
    if mode == "even":
        lane = lax.broadcasted_iota(I32, (tm, LANES), 1)
        beta = _sigmoid(z)
        g = -jnp.exp(p0_ref[...]) * _softplus(z + p1_ref[...])
        shift = int(math.log2(GDN_CHUNK))
        same_chunk = jnp.right_shift(ri, shift) == jnp.right_shift(ci, shift)
        tri = jnp.where(jnp.logical_and(same_chunk, ci <= ri), 1.0, 0.0)
        gc = _dot(tri, g, precision=HIGHEST)
        o_ref[...] = jnp.where(lane < B_HEADS, beta, gc)
    else:
        @pl.when(pl.program_id(0) % steps_per_seq == 0)
        def _():
            carry_ref[...] = jnp.zeros_like(carry_ref)

        logf = -_softplus(-(z + p0_ref[...]))
        tri = jnp.where(ci <= ri, 1.0, 0.0)
        cum = _dot(tri, logf, precision=HIGHEST) + carry_ref[0:1, :]
        o_ref[...] = cum
        carry_ref[...] = jnp.broadcast_to(cum[tm - 1:tm, :], carry_ref.shape)


def _gates(mode, x, w_small, p0, p1, seq_len, tm=512):
    n, k = x.shape
    tm = min(tm, seq_len)
    row = pl.BlockSpec((1, LANES), lambda i: (0, 0))
    return pl.pallas_call(
        functools.partial(_gates_body, mode, tm, seq_len // tm),
        grid=(n // tm,),
        in_specs=[pl.BlockSpec((tm, k), lambda i: (i, 0)),
                  pl.BlockSpec((k, LANES), lambda i: (0, 0)), row, row],
        out_specs=pl.BlockSpec((tm, LANES), lambda i: (i, 0)),
        out_shape=jax.ShapeDtypeStruct((n, LANES), F32),
        scratch_shapes=[pltpu.VMEM((SUBLANES, LANES), F32)],
        compiler_params=_cparams(("arbitrary",)),
        name="gates_" + mode,
    )(x, w_small, p0, p1)


def _pad_lanes(a):
    return jnp.pad(a, [(0, 0)] * (a.ndim - 1) + [(0, LANES - a.shape[-1])])


def _qknorm_body(x_ref, w_ref, o_ref):
    x = x_ref[...].astype(F32)
    lane = lax.broadcasted_iota(I32, x.shape, 1)
    lo = lane < C_HEAD_DIM
    ss = x * x
    s_lo = jnp.sum(jnp.where(lo, ss, 0.0), axis=-1, keepdims=True)
    s_hi = jnp.sum(jnp.where(lo, 0.0, ss), axis=-1, keepdims=True)
    inv = jnp.where(lo, lax.rsqrt(s_lo / C_HEAD_DIM + RMS_EPS), lax.rsqrt(s_hi / C_HEAD_DIM + RMS_EPS))
    o_ref[...] = (x * inv * w_ref[...]).astype(o_ref.dtype)


def _qknorm(proj, w2, tm=1024):
    n = proj.shape[0]
    tm = min(tm, n)
    nblk = 2 * C_HEADS * C_HEAD_DIM // LANES
    return pl.pallas_call(
        _qknorm_body,
        grid=(n // tm, nblk),
        in_specs=[pl.BlockSpec((tm, LANES), lambda i, c: (i, c)),
                  pl.BlockSpec((None, 1, LANES), lambda i, c: (c // (nblk // 2), 0, 0))],
        out_specs=pl.BlockSpec((tm, LANES), lambda i, c: (i, c)),
        out_shape=jax.ShapeDtypeStruct((n, nblk * LANES), BF16),
        compiler_params=_cparams(("parallel", "parallel")),
        name="fox_qknorm",
    )(proj, w2)


def _attn_body(mode, t, lam_init, ii_ref, jj_ref, q_ref, k_ref, v_ref, *rest):
    if mode == "diff":
        strip_ref, far_ref, lam_ref, subln_ref, o_ref, qs_ref, m_ref, acc_ref, l_ref = rest
    else:
        ck_ref, c0_ref, gate_ref, o_ref, qs_ref, m_ref, acc_ref = rest
    p = pl.program_id(2)
    i = ii_ref[p]
    j = jj_ref[p]
    rc = min(ATTN_ROWS, t)

    @pl.when(j == 0)
    def _():
        q = q_ref[...]
        lane = lax.broadcasted_iota(I32, q.shape, 1)
        zero = jnp.zeros_like(q)
        qs_ref[0:t, :] = jnp.where(lane < 64, q, zero)
        qs_ref[t:2 * t, :] = jnp.where(lane < 64, zero, q)
        m_ref[...] = jnp.full(m_ref.shape, NEG_INF, F32)
        acc_ref[...] = jnp.zeros_like(acc_ref)
        if mode == "diff":
            l_ref[...] = jnp.zeros_like(l_ref)

    def step(kind):
        k = k_ref[...]
        v = v_ref[...]
        if mode == "fox":
            lane = lax.broadcasted_iota(I32, v.shape, 1)
            one = jnp.ones_like(v)
            rhs = (jnp.where(lane < 64, v, one), jnp.where(lane < 64, one, v))
            bias = (c0_ref[:, 0:1] - ck_ref[...]) * LOG2E
        elif kind != "far":
            x = jnp.broadcast_to(strip_ref[...], (t, 2 * t))
            tile = pltpu.roll(x, t + 1, 1, stride=1, stride_axis=0)[:, :t]
        for c in range(2 * t // rc):
            r0 = c * rc
            half = r0 // t
            h0 = r0 - half * t
            rows = slice(r0, r0 + rc)
            s = _dot_nt(qs_ref[rows, :], k)
            if mode == "fox":
                s = s + bias[half:half + 1]
            elif kind == "far":
                s = s + far_ref[0:1, 0:1]
            else:
                s = s + tile[h0:h0 + rc]
            if kind == "diag":
                rr = lax.broadcasted_iota(I32, (rc, t), 0) + h0
                cc = lax.broadcasted_iota(I32, (rc, t), 1)
                s = jnp.where(rr >= cc, s, NEG_INF)
            m_prev = m_ref[rows, :]
            m_new = jnp.maximum(m_prev, jnp.max(s, axis=1, keepdims=True))
            alpha = jnp.exp2(m_prev - m_new)
            pr = jnp.exp2(s - pltpu.repeat(m_new, t // LANES, 1))
            if mode == "diff":
                l_ref[rows, :] = alpha * l_ref[rows, :] + jnp.sum(pr, axis=1, keepdims=True)
                pv = _dot(pr.astype(BF16), v)
            else:
                pv = _dot(pr.astype(BF16), rhs[half])
            acc_ref[rows, :] = alpha * acc_ref[rows, :] + pv
            m_ref[rows, :] = m_new

    if mode == "diff":
        d = i - j
        near = d * t - (t - 1) < T5_MAX_DISTANCE
        pl.when(d == 0)(functools.partial(step, "diag"))
        pl.when(jnp.logical_and(d > 0, near))(functools.partial(step, "near"))
        pl.when(jnp.logical_not(near))(functools.partial(step, "far"))
    else:
        pl.when(j == i)(functools.partial(step, "diag"))
        pl.when(j != i)(functools.partial(step, "off"))

    @pl.when(j == i)
    def _():
        acc = acc_ref[...]
        if mode == "diff":
            o = acc * (1.0 / l_ref[...])
            lp = lam_ref[...]
            lam = (jnp.exp(jnp.sum(lp[0:1] * lp[1:2], axis=-1, keepdims=True))
                   - jnp.exp(jnp.sum(lp[2:3] * lp[3:4], axis=-1, keepdims=True)) + lam_init)
            dlt = o[0:t] - lam * o[t:2 * t]
            ms = jnp.mean(dlt * dlt, axis=-1, keepdims=True)
            out = dlt * lax.rsqrt(ms + RMS_EPS) * subln_ref[...] * (1.0 - lam_init)
        else:
            lane = lax.broadcasted_iota(I32, (t, LANES), 1)
            lo = acc[0:t]
            hi = acc[t:2 * t]
            out = jnp.where(lane < 64, lo * (1.0 / lo[:, 64:65]), hi * (1.0 / hi[:, 0:1]))
            out = out * _sigmoid(gate_ref[...].astype(F32))
        o_ref[...] = out.astype(o_ref.dtype)


def _causal_pairs(nq):
    ii, jj = [], []
    for i in range(nq):
        for j in range(i + 1):
            ii.append(i)
            jj.append(j)
    return jnp.asarray(np.array(ii, np.int32)), jnp.asarray(np.array(jj, np.int32))


def _attn_scratch(mode, t):
    base = [pltpu.VMEM((2 * t, LANES), BF16), pltpu.VMEM((2 * t, LANES), F32), pltpu.VMEM((2 * t, LANES), F32)]
    return base + ([pltpu.VMEM((2 * t, LANES), F32)] if mode == "diff" else [])


def _diff_attention(proj, bsz, seq, strips, far, lam_params, subln_w, lam_init, t):
    nq = seq // t
    ii, jj = _causal_pairs(nq)
    nd = strips.shape[1]
    h_ = A_HEADS
    grid_spec = pltpu.PrefetchScalarGridSpec(
        num_scalar_prefetch=2,
        grid=(bsz, h_, int(ii.shape[0])),
        in_specs=[
            pl.BlockSpec((t, LANES), lambda b, h, p, ii, jj: (b * nq + ii[p], h)),
            pl.BlockSpec((t, LANES), lambda b, h, p, ii, jj: (b * nq + jj[p], h_ + h)),
            pl.BlockSpec((t, LANES), lambda b, h, p, ii, jj: (b * nq + jj[p], 2 * h_ + h)),
            pl.BlockSpec((None, None, 1, 2 * t),
                         lambda b, h, p, ii, jj: (h, jnp.minimum(ii[p] - jj[p], nd - 1), 0, 0)),
            pl.BlockSpec((None, 1, LANES), lambda b, h, p, ii, jj: (h, 0, 0)),
            pl.BlockSpec(lam_params.shape, lambda b, h, p, ii, jj: (0, 0)),
            pl.BlockSpec((1, LANES), lambda b, h, p, ii, jj: (0, 0)),
        ],
        out_specs=pl.BlockSpec((t, LANES), lambda b, h, p, ii, jj: (b * nq + ii[p], h)),
        scratch_shapes=_attn_scratch("diff", t),
    )
    return pl.pallas_call(
        functools.partial(_attn_body, "diff", t, lam_init),
        grid_spec=grid_spec,
        out_shape=jax.ShapeDtypeStruct((bsz * seq, h_ * LANES), BF16),
        compiler_params=_cparams(("parallel", "parallel", "arbitrary")),
        name="diff_attn",
    )(ii, jj, proj, proj, proj, strips, far, lam_params, subln_w)


def _fox_attention(qk, proj, cum_t, bsz, seq, t):
    nq = seq // t
    ii, jj = _causal_pairs(nq)
    hp = C_HEADS // 2
    grid_spec = pltpu.PrefetchScalarGridSpec(
        num_scalar_prefetch=2,
        grid=(bsz, hp, int(ii.shape[0])),
        in_specs=[
            pl.BlockSpec((t, LANES), lambda b, h, p, ii, jj: (b * nq + ii[p], h)),
            pl.BlockSpec((t, LANES), lambda b, h, p, ii, jj: (b * nq + jj[p], hp + h)),
            pl.BlockSpec((t, LANES), lambda b, h, p, ii, jj: (b * nq + jj[p], 2 * hp + h)),
            pl.BlockSpec((None, None, 2, t), lambda b, h, p, ii, jj: (b, h, 0, jj[p])),
            pl.BlockSpec((None, None, 2, t), lambda b, h, p, ii, jj: (b, h, 0, ii[p])),
            pl.BlockSpec((t, LANES), lambda b, h, p, ii, jj: (b * nq + ii[p], 3 * hp + h)),
        ],
        out_specs=pl.BlockSpec((t, LANES), lambda b, h, p, ii, jj: (b * nq + ii[p], h)),
        scratch_shapes=_attn_scratch("fox", t),
    )
    return pl.pallas_call(
        functools.partial(_attn_body, "fox", t, 0.0),
        grid_spec=grid_spec,
        out_shape=jax.ShapeDtypeStruct((bsz * seq, hp * LANES), BF16),
        compiler_params=_cparams(("parallel", "parallel", "arbitrary")),
        name="fox_attn",
    )(ii, jj, qk, qk, proj, cum_t, cum_t, proj)


def _t5_bias_by_distance(t5_bias, seq):
    n = jnp.arange(seq, dtype=I32)
    max_exact = T5_BUCKETS // 2
    nf = jnp.maximum(n, 1).astype(F32)
    large = max_exact + (jnp.log(nf / max_exact) / math.log(T5_MAX_DISTANCE / max_exact)
                         * (T5_BUCKETS - max_exact)).astype(I32)
    large = jnp.minimum(large, T5_BUCKETS - 1)
    bucket = jnp.where(n < max_exact, n, large)
    return t5_bias.astype(F32).T[:, bucket]


def _t5_strips(t5_bias, seq, t):
    vec = _t5_bias_by_distance(t5_bias, seq) * LOG2E
    nd = 1
    while nd * t - (t - 1) < T5_MAX_DISTANCE and nd < seq // t:
        nd += 1
    d = np.arange(nd)[:, None]
    c = np.arange(2 * t)[None, :]
    rel = d * t + t - 1 - c
    rel = np.clip(rel, 0, seq - 1)
    strips = vec[:, jnp.asarray(rel.astype(np.int32))]
    far = jnp.broadcast_to((t5_bias.astype(F32).T[:, T5_BUCKETS - 1] * LOG2E)[:, None, None],
                           (t5_bias.shape[1], 1, LANES))
    return strips[:, :, None, :], far


def _gdn_body(tb, xq_ref, xk_ref, xv_ref, z_ref, gcol_ref, grow_ref, cw_ref, nw_ref, o_ref, s_ref, carry_ref):
    h = pl.program_id(1)
    c_ = GDN_CHUNK

    @pl.when(pl.program_id(2) == 0)
    def _():
        s_ref[...] = jnp.zeros_like(s_ref)
        carry_ref[...] = jnp.zeros_like(carry_ref)

    cw = cw_ref[...]

    def conv_silu(idx, x_ref):
        x = x_ref[...].astype(F32)
        xe = jnp.concatenate([carry_ref[idx], x], axis=0)
        y = cw[4 * idx + 3:4 * idx + 4] * x
        for tap in range(CONV_WIDTH - 1):
            shift = CONV_WIDTH - 1 - tap
            y = y + cw[4 * idx + tap:4 * idx + tap + 1] * pltpu.roll(xe, shift, 0)[SUBLANES:]
        carry_ref[idx] = x[tb - SUBLANES:]
        return y * _sigmoid(y)

    q = conv_silu(0, xq_ref)
    k = conv_silu(1, xk_ref)
    v = conv_silu(2, xv_ref)
    q = q * lax.rsqrt(jnp.sum(q * q, axis=-1, keepdims=True) + RMS_EPS) * (B_HEAD_DIM ** -0.5)
    k = k * lax.rsqrt(jnp.sum(k * k, axis=-1, keepdims=True) + RMS_EPS)

    gates = gcol_ref[...]
    lane = lax.broadcasted_iota(I32, gates.shape, 1)
    beta = jnp.sum(jnp.where(lane == h, gates, 0.0), axis=1, keepdims=True)
    gc = jnp.sum(jnp.where(lane == B_HEADS + h, gates, 0.0), axis=1, keepdims=True)
    gc_row = grow_ref[1:2, :]
    egc = jnp.exp(gc)
    kb = k * beta
    rhs = jnp.concatenate([v * beta, kb * egc], axis=1)
    qd = q * egc

    ri = lax.broadcasted_iota(I32, (c_, c_), 0)
    ci = lax.broadcasted_iota(I32, (c_, c_), 1)
    state = s_ref[...]
    outs = []
    for c in range(tb // c_):
        sl = slice(c * c_, (c + 1) * c_)
        gcc = gc[sl]
        decay = jnp.exp(jnp.where(ri >= ci, gcc - gc_row[:, sl], NEG_INF))
        kc = k[sl].astype(BF16)
        kk = _dot_nt(kb[sl].astype(BF16), kc)
        qk = _dot_nt(q[sl].astype(BF16), kc) * decay
        x = -(kk * jnp.where(ri > ci, decay, 0.0))
        r = x
        pw = x
        for _ in range(int(math.log2(c_)) - 1):
            pwb = pw.astype(BF16)
            pw = _dot(pwb, pwb)
            r = r + pw + _dot(r.astype(BF16), pw.astype(BF16))
        sol = rhs[sl] + _dot(r.astype(BF16), rhs[sl].astype(BF16))
        u = sol[:, :B_HEAD_DIM]
        w = sol[:, B_HEAD_DIM:]
        g_last = gcc[c_ - 1:c_]
        kd = k[sl] * jnp.exp(g_last - gcc)
        sb = state.astype(BF16)
        v_new = u - _dot(w.astype(BF16), sb)
        vnb = v_new.astype(BF16)
        outs.append(_dot(qd[sl].astype(BF16), sb) + _dot(qk.astype(BF16), vnb))
        state = state * jnp.exp(g_last) + _dot_tn(kd.astype(BF16), vnb)
    s_ref[...] = state
    o = jnp.concatenate(outs, axis=0)
    o = o * lax.rsqrt(jnp.mean(o * o, axis=-1, keepdims=True) + RMS_EPS) * nw_ref[...]
    z = z_ref[...].astype(F32)
    o_ref[...] = (o * (z * _sigmoid(z))).astype(o_ref.dtype)


def _gdn(proj, gates, gates_row, conv_w, norm_w, bsz, seq, tb):
    nt = seq // tb
    h_ = B_HEADS
    col0 = 3 * A_HEADS
    blk = lambda off: pl.BlockSpec((tb, LANES), lambda b, h, s: (b * nt + s, col0 + off * h_ + h))
    return pl.pallas_call(
        functools.partial(_gdn_body, tb),
        grid=(bsz, h_, nt),
        in_specs=[blk(0), blk(1), blk(2), blk(3),
                  pl.BlockSpec((tb, LANES), lambda b, h, s: (b * nt + s, 0)),
                  pl.BlockSpec((None, 2, tb), lambda b, h, s: (b * h_ + h, 0, s)),
                  pl.BlockSpec((None, 3 * CONV_WIDTH, LANES), lambda b, h, s: (h, 0, 0)),
                  pl.BlockSpec((1, LANES), lambda b, h, s: (0, 0))],
        out_specs=pl.BlockSpec((tb, LANES), lambda b, h, s: (b * nt + s, h)),
        out_shape=jax.ShapeDtypeStruct((bsz * seq, h_ * LANES), BF16),
        scratch_shapes=[pltpu.VMEM((B_HEAD_DIM, B_HEAD_DIM), F32),
                        pltpu.VMEM((3, SUBLANES, LANES), F32)],
        compiler_params=_cparams(("parallel", "parallel", "arbitrary")),
        name="gdn",
    )(proj, proj, proj, proj, gates, gates_row, conv_w, norm_w)


def _outproj_ln_body(a_ref, b_ref, w_ref, x_ref, g_ref, bb_ref, o_ref, wb_ref):
    @pl.when(pl.program_id(0) == 0)
    def _():
        wb_ref[...] = w_ref[...].astype(BF16)

    half = a_ref.shape[1]
    hmix = _dot(a_ref[...], wb_ref[0:half, :]) + _dot(b_ref[...], wb_ref[half:, :])
    o_ref[...] = _layer_norm(DEEPNORM_ALPHA * x_ref[...] + hmix, g_ref[...], bb_ref[...])


def _outproj_ln(a, a_blk, b, b_blk, w, x, g, bb, tm=512):
    n, d = x.shape
    tm = min(tm, n)
    half = d // 2
    row = pl.BlockSpec((1, d), lambda i: (0, 0))
    return pl.pallas_call(
        _outproj_ln_body,
        grid=(n // tm,),
        in_specs=[pl.BlockSpec((tm, half), lambda i: (i, a_blk)),
                  pl.BlockSpec((tm, half), lambda i: (i, b_blk)),
                  pl.BlockSpec((d, d), lambda i: (0, 0)),
                  pl.BlockSpec((tm, d), lambda i: (i, 0)), row, row],
        out_specs=pl.BlockSpec((tm, d), lambda i: (i, 0)),
        out_shape=jax.ShapeDtypeStruct((n, d), F32),
        scratch_shapes=[pltpu.VMEM((d, d), BF16)],
        compiler_params=_cparams(("arbitrary",)),
        name="outproj_ln",
    )(a, b, w, x, g, bb)


def _router_body(tm, x_ref, w_ref, b_ref, idx_ref, gate_ref, rank_ref, cnt_ref, carry_ref):
    @pl.when(pl.program_id(0) == 0)
    def _():
        carry_ref[...] = jnp.zeros_like(carry_ref)

    logits = _dot(x_ref[...], w_ref[...], precision=HIGHEST) + b_ref[...]
    lg = jnp.transpose(logits)[0:N_EXPERTS, :]
    e_iota = lax.broadcasted_iota(I32, (N_EXPERTS, tm), 0).astype(F32)
    vals, idxs, hots = [], [], []
    for _ in range(TOP_K):
        m = jnp.max(lg, axis=0, keepdims=True)
        idx = jnp.min(jnp.where(lg == m, e_iota, float(N_EXPERTS)), axis=0, keepdims=True)
        hot = e_iota == idx
        lg = jnp.where(hot, NEG_INF, lg)
        vals.append(m)
        idxs.append(idx)
        hots.append(hot)
    es = [jnp.exp(v - vals[0]) for v in vals]
    den = es[0] + es[1] + es[2] + es[3]
    sel = jnp.zeros((N_EXPERTS, tm), F32)
    for hot in hots:
        sel = sel + jnp.where(hot, 1.0, 0.0)
    before = jnp.where(lax.broadcasted_iota(I32, (tm, tm), 0) < lax.broadcasted_iota(I32, (tm, tm), 1),
                       1.0, 0.0).astype(BF16)
    cum = _dot(sel.astype(BF16), before) + carry_ref[:, 0:1]
    ranks = [jnp.sum(jnp.where(hot, cum, 0.0), axis=0, keepdims=True) for hot in hots]
    total = carry_ref[...] + jnp.sum(sel, axis=1, keepdims=True)
    carry_ref[...] = total
    idx_ref[...] = jnp.concatenate(idxs, axis=0).astype(I32)
    gate_ref[...] = jnp.concatenate([e / den for e in es], axis=0)
    rank_ref[...] = jnp.concatenate(ranks, axis=0).astype(I32)
    cnt_ref[...] = total


def _router(x, w_pad, b_pad, tm=512):
    n, d = x.shape
    tm = min(tm, n)
    out4 = lambda dt: jax.ShapeDtypeStruct((TOP_K, n), dt)
    blk4 = pl.BlockSpec((TOP_K, tm), lambda i: (0, i))
    return pl.pallas_call(
        functools.partial(_router_body, tm),
        grid=(n // tm,),
        in_specs=[pl.BlockSpec((tm, d), lambda i: (i, 0)),
                  pl.BlockSpec((d, LANES), lambda i: (0, 0)),
                  pl.BlockSpec((1, LANES), lambda i: (0, 0))],
        out_specs=[blk4, blk4, blk4, pl.BlockSpec((N_EXPERTS, LANES), lambda i: (0, 0))],
        out_shape=[out4(I32), out4(F32), out4(I32), jax.ShapeDtypeStruct((N_EXPERTS, LANES), F32)],
        scratch_shapes=[pltpu.VMEM((N_EXPERTS, LANES), F32)],
        compiler_params=_cparams(("arbitrary",)),
        name="moe_router",
    )(x, w_pad, b_pad)


def _assignment_row(idx_ref, rank_ref, start_ref, k, r):
    return start_ref[idx_ref[k, r]] + rank_ref[k, r]


def _dispatch_body(td, idx_ref, rank_ref, start_ref, x_ref, xg_in, xg_out, sem):
    del xg_in

    def row_copy(k, r):
        dst = _assignment_row(idx_ref, rank_ref, start_ref, k, r)
        return pltpu.make_async_copy(x_ref.at[pl.ds(r, 1)], xg_out.at[pl.ds(dst, 1)], sem)

    def issue(r, carry):
        for k in range(TOP_K):
            row_copy(k, r).start()
        return carry

    def drain(r, carry):
        for k in range(TOP_K):
            row_copy(k, r).wait()
        return carry

    lax.fori_loop(0, td, issue, 0)
    lax.fori_loop(0, td, drain, 0)


def _smem_rows(tile):
    return pl.BlockSpec((TOP_K, tile), lambda i: (0, i), memory_space=pltpu.SMEM)


def _dispatch(x, idx, rank, pad_start, n_rows, td=512):
    n, d = x.shape
    td = min(td, n)
    xg0 = jnp.zeros((n_rows, d), x.dtype)
    return pl.pallas_call(
        functools.partial(_dispatch_body, td),
        grid=(n // td,),
        in_specs=[_smem_rows(td), _smem_rows(td),
                  pl.BlockSpec(memory_space=pltpu.SMEM),
                  pl.BlockSpec((td, d), lambda i: (i, 0)),
                  pl.BlockSpec(memory_space=pl.ANY)],
        out_specs=pl.BlockSpec(memory_space=pl.ANY),
        out_shape=jax.ShapeDtypeStruct((n_rows, d), x.dtype),
        scratch_shapes=[pltpu.SemaphoreType.DMA(())],
        input_output_aliases={4: 0},
        compiler_params=_cparams(("arbitrary",)),
        name="moe_dispatch",
    )(idx, rank, pad_start, x, xg0)


def _expert_body(be_ref, nu_ref, x_ref, wgu_ref, bgu_ref, wd_ref, bd_ref, o_ref, wgu_b, wd_b):
    i = pl.program_id(0)
    prev = be_ref[jnp.maximum(i - 1, 0)]
    fresh = jnp.logical_or(i == 0, be_ref[i] != prev)

    @pl.when(jnp.logical_and(i < nu_ref[0], fresh))
    def _():
        wgu_b[...] = wgu_ref[...].astype(BF16)
        wd_b[...] = wd_ref[...].astype(BF16)

    @pl.when(i < nu_ref[0])
    def _():
        hcat = _dot(x_ref[...].astype(BF16), wgu_b[...]) + bgu_ref[...]
        g = jnp.minimum(hcat[:, :D_FF], SWIGLU_LIMIT)
        u = jnp.clip(hcat[:, D_FF:], -SWIGLU_LIMIT, SWIGLU_LIMIT)
        act = g * _sigmoid(SWIGLU_ALPHA * g) * (u + 1.0)
        o_ref[...] = _dot(act.astype(BF16), wd_b[...]) + bd_ref[...]

    @pl.when(i >= nu_ref[0])
    def _():
        o_ref[...] = jnp.zeros_like(o_ref)


def _experts(xg, block_expert, n_used, w_gu, b_gu, w_down, b_down):
    n_rows, d = xg.shape
    nb = n_rows // MOE_ROWS
    grid_spec = pltpu.PrefetchScalarGridSpec(
        num_scalar_prefetch=2,
        grid=(nb,),
        in_specs=[
            pl.BlockSpec((MOE_ROWS, d), lambda i, be, nu: (i, 0)),
            pl.BlockSpec((None, d, 2 * D_FF), lambda i, be, nu: (be[i], 0, 0)),
            pl.BlockSpec((None, 1, 2 * D_FF), lambda i, be, nu: (be[i], 0, 0)),
            pl.BlockSpec((None, D_FF, d), lambda i, be, nu: (be[i], 0, 0)),
            pl.BlockSpec((None, 1, d), lambda i, be, nu: (be[i], 0, 0)),
        ],
        out_specs=pl.BlockSpec((MOE_ROWS, d), lambda i, be, nu: (i, 0)),
        scratch_shapes=[pltpu.VMEM((d, 2 * D_FF), BF16), pltpu.VMEM((D_FF, d), BF16)],
    )
    return pl.pallas_call(
        _expert_body,
        grid_spec=grid_spec,
        out_shape=jax.ShapeDtypeStruct((n_rows, d), F32),
        compiler_params=_cparams(("arbitrary",)),
        name="moe_experts",
    )(block_expert, n_used, xg, w_gu, b_gu[:, None, :], w_down, b_down[:, None, :])


def _combine_ln_body(tc, idx_ref, rank_ref, start_ref, yg_hbm, gate_ref, x_ref, g_ref, b_ref, o_ref, ybuf, sem):
    def row_copy(k, r):
        src = _assignment_row(idx_ref, rank_ref, start_ref, k, r)
        return pltpu.make_async_copy(yg_hbm.at[pl.ds(src, 1)], ybuf.at[k, pl.ds(r, 1)], sem)

    def issue(r, carry):
        for k in range(TOP_K):
            row_copy(k, r).start()
        return carry

    def drain(r, carry):
        for k in range(TOP_K):
            row_copy(k, r).wait()
        return carry

    lax.fori_loop(0, tc, issue, 0)
    lax.fori_loop(0, tc, drain, 0)
    gate = gate_ref[...]
    hmoe = gate[:, 0:1] * ybuf[0]
    for k in range(1, TOP_K):
        hmoe = hmoe + gate[:, k:k + 1] * ybuf[k]
    o_ref[...] = _layer_norm(DEEPNORM_ALPHA * x_ref[...] + hmoe, g_ref[...], b_ref[...])


def _combine_ln(yg, idx, rank, pad_start, gate_t, x, g, b, tc=256):
    n, d = x.shape
    tc = min(tc, n)
    row = pl.BlockSpec((1, d), lambda i: (0, 0))
    return pl.pallas_call(
        functools.partial(_combine_ln_body, tc),
        grid=(n // tc,),
        in_specs=[_smem_rows(tc), _smem_rows(tc),
                  pl.BlockSpec(memory_space=pltpu.SMEM),
                  pl.BlockSpec(memory_space=pl.ANY),
                  pl.BlockSpec((tc, TOP_K), lambda i: (i, 0)),
                  pl.BlockSpec((tc, d), lambda i: (i, 0)), row, row],
        out_specs=pl.BlockSpec((tc, d), lambda i: (i, 0)),
        out_shape=jax.ShapeDtypeStruct((n, d), F32),
        scratch_shapes=[pltpu.VMEM((TOP_K, tc, d), F32), pltpu.SemaphoreType.DMA(())],
        compiler_params=_cparams(("arbitrary",)),
        name="moe_combine_ln",
    )(idx, rank, pad_start, yg, gate_t, x, g, b)


def _moe_ln(x, router_w, router_b, w_gu, b_gu, w_down, b_down, ln_g, ln_b):
    n, d = x.shape
    idx, gate, rank, cnt = _router(x, _pad_lanes(router_w), _pad_lanes(router_b[None, :]))
    counts = cnt[:, 0].astype(I32)
    padded = (counts + MOE_ROWS - 1) // MOE_ROWS * MOE_ROWS
    pad_end = jnp.cumsum(padded)
    pad_start = pad_end - padded
    n_blocks = -(-(n * TOP_K) // MOE_ROWS) + N_EXPERTS
    block_row0 = jnp.arange(n_blocks, dtype=I32) * MOE_ROWS
    block_expert = jnp.minimum(jnp.sum((pad_end[None, :] <= block_row0[:, None]).astype(I32), axis=1),
                               N_EXPERTS - 1)
    n_used = (pad_end[-1:] // MOE_ROWS).astype(I32)
    xg = _dispatch(x, idx, rank, pad_start, n_blocks * MOE_ROWS)
    yg = _experts(xg, block_expert, n_used, w_gu, b_gu, w_down, b_down)
    return _combine_ln(yg, idx, rank, pad_start, gate.T, x, ln_g[None, :], ln_b[None, :])


def _even_mixer_ln(x, bsz, seq, w_in, w_out, lam_params, subln_w, conv_w, a_log, dt_bias, gdn_norm_w,
                   t5_bias, lam_init, ln_g, ln_b):
    main = 3 * A_HEADS * LANES + 4 * B_HEADS * B_HEAD_DIM
    n_q = A_HEADS * 2 * A_HEAD_DIM
    col_scale = jnp.concatenate([jnp.full((1, n_q), A_HEAD_DIM ** -0.5 * LOG2E, F32),
                                 jnp.ones((1, main - n_q), F32)], axis=1)
    proj = _proj(x, w_in[:, :main], col_scale)
    zeros4 = jnp.zeros((B_HEADS,), F32)
    p0 = _pad_lanes(jnp.concatenate([zeros4, a_log.astype(F32)])[None, :])
    p1 = _pad_lanes(jnp.concatenate([zeros4, dt_bias.astype(F32)])[None, :])
    gates = _gates("even", x, _pad_lanes(w_in[:, main:]), p0, p1, seq)
    t = min(ATTN_TILE, seq)
    strips, far = _t5_strips(t5_bias, seq, t)
    ao = _diff_attention(proj, bsz, seq, strips, far, lam_params.astype(F32), subln_w[None, :].astype(F32),
                         lam_init, t)
    g8 = gates[:, :2 * B_HEADS].reshape(bsz, seq, 2, B_HEADS)
    gates_row = g8.transpose(0, 3, 2, 1).reshape(bsz * B_HEADS, 2, seq)
    cw = conv_w.astype(F32).reshape(CONV_WIDTH, 3, B_HEADS, B_HEAD_DIM).transpose(2, 1, 0, 3)
    cw = cw.reshape(B_HEADS, 3 * CONV_WIDTH, B_HEAD_DIM)
    bo = _gdn(proj, gates, gates_row, cw, gdn_norm_w[None, :].astype(F32), bsz, seq, min(GDN_TILE, seq))
    return _outproj_ln(ao, 0, bo, 0, w_out, x, ln_g[None, :], ln_b[None, :])


def _odd_mixer_ln(x, bsz, seq, w_in, w_out, qk_norm_w, forget_b, ln_g, ln_b):
    main = 4 * C_HEADS * C_HEAD_DIM
    proj = _proj(x, w_in[:, :main], jnp.ones((1, main), F32))
    fb =_pad_lanes(forget_b.astype(F32)[None, :])
    cum = _gates("odd", x, _pad_lanes(w_in[:, main:]), fb, fb, seq)
    cum_t = cum[:, :C_HEADS].reshape(bsz, seq, C_HEADS // 2, 2).transpose(0, 2, 3, 1)
    scale = jnp.asarray([C_HEAD_DIM ** -0.5 * LOG2E, 1.0], F32)[:, None]
    w2 = jnp.tile(qk_norm_w.astype(F32) * scale, (1, 2))[:, None, :]
    qk = _qknorm(proj, w2)
    o = _fox_attention(qk, proj, cum_t, bsz, seq, min(ATTN_TILE, seq))
    return _outproj_ln(o, 0, o, 1, w_out, x, ln_g[None, :], ln_b[None, :])


def kernel(x, t5_bias, even_w_in, even_w_out, diff_lambda, diff_subln_w, gdn_conv_w, gdn_a_log, gdn_dt_bias, gdn_norm_w, odd_w_in, odd_w_out, fox_qk_norm_w, fox_forget_b, router_w, router_b, moe_w_gate_up, moe_b_gate_up, moe_w_down, moe_b_down, ln_mix_g, ln_mix_b, ln_ffn_g, ln_ffn_b):
    bsz, seq, d = x.shape
    xf = x.reshape(bsz * seq, d)
    for layer in range(DEPTH):
        i = layer // 2
        if layer % 2 == 0:
            lam_init = 0.8 - 0.6 * math.exp(-0.3 * layer)
            xf = _even_mixer_ln(xf, bsz, seq, even_w_in[i], even_w_out[i], diff_lambda[i], diff_subln_w[i],
                                gdn_conv_w[i], gdn_a_log[i], gdn_dt_bias[i], gdn_norm_w[i], t5_bias, lam_init,
                                ln_mix_g[layer], ln_mix_b[layer])
        else:
            xf = _odd_mixer_ln(xf, bsz, seq, odd_w_in[i], odd_w_out[i], fox_qk_norm_w[i], fox_forget_b[i],
                               ln_mix_g[layer], ln_mix_b[layer])
        xf = _moe_ln(xf, router_w[layer], router_b[layer], moe_w_gate_up[layer], moe_b_gate_up[layer],
                     moe_w_down[layer], moe_b_down[layer], ln_ffn_g[layer], ln_ffn_b[layer])
    return xf.reshape(bsz, seq, d)
```

```python
import functools
import math

import numpy as np
import jax
import jax.numpy as jnp
from jax import lax
from jax.experimental import pallas as pl
from jax.experimental.pallas import tpu as pltpu

F32 = jnp.float32
BF16 = jnp.bfloat16
I32 = jnp.int32
HIGHEST = lax.Precision.HIGHEST

D_MODEL = 1024
DEPTH = 4
A_HEADS = 4
A_HEAD_DIM = 64
B_HEADS = 4
B_HEAD_DIM = 128
CONV_WIDTH = 4
C_HEADS = 16
C_HEAD_DIM = 64
T5_BUCKETS = 32
T5_MAX_DISTANCE = 2048
N_EXPERTS = 32
TOP_K = 4
D_FF = D_MODEL
SWIGLU_LIMIT = 7.0
SWIGLU_ALPHA = 1.702
DEEPNORM_ALPHA = (2 * DEPTH) ** 0.25
LN_EPS = 1e-5
RMS_EPS = 1e-6

LANES = 128
SUBLANES = 8
VMEM_LIMIT = 56 * 1024 * 1024

ATTN_TILE = 1024
ATTN_ROWS = 1024
GDN_CHUNK = 128
GDN_TILE = 512
MOE_ROWS = 256
NEG_INF = float("-inf")
LOG2E = math.log2(math.e)


def _cparams(sem, vmem=VMEM_LIMIT):
    return pltpu.CompilerParams(dimension_semantics=sem, vmem_limit_bytes=vmem)


def _dot(a, b, **kw):
    return jnp.dot(a, b, preferred_element_type=F32, **kw)


def _dot_nt(a, b, **kw):
    return lax.dot_general(a, b, (((1,), (1,)), ((), ())), preferred_element_type=F32, **kw)


def _dot_tn(a, b, **kw):
    return lax.dot_general(a, b, (((0,), (0,)), ((), ())), preferred_element_type=F32, **kw)


def _sigmoid(x):
    return 1.0 / (1.0 + jnp.exp(-x))


def _softplus(x):
    return jnp.maximum(x, 0.0) + jnp.log(1.0 + jnp.exp(-jnp.abs(x)))


def _layer_norm(xf, g, b):
    mu = jnp.mean(xf, axis=-1, keepdims=True)
    xc = xf - mu
    var = jnp.mean(xc * xc, axis=-1, keepdims=True)
    return xc * lax.rsqrt(var + LN_EPS) * g + b


def _proj_body(x_ref, w_ref, cs_ref, o_ref, xb_ref):
    @pl.when(pl.program_id(1) == 0)
    def _():
        xb_ref[...] = x_ref[...].astype(BF16)

    o_ref[...] = (_dot(xb_ref[...], w_ref[...].astype(BF16)) * cs_ref[...]).astype(o_ref.dtype)


def _proj(x, w, col_scale, tm=1024, tn=512):
    n, k = x.shape
    m = w.shape[1]
    tm = min(tm, n)
    return pl.pallas_call(
        _proj_body,
        grid=(n // tm, m // tn),
        in_specs=[pl.BlockSpec((tm, k), lambda i, j: (i, 0)),
                  pl.BlockSpec((k, tn), lambda i, j: (0, j)),
                  pl.BlockSpec((1, tn), lambda i, j: (0, j))],
        out_specs=pl.BlockSpec((tm, tn), lambda i, j: (i, j)),
        out_shape=jax.ShapeDtypeStruct((n, m), BF16),
        scratch_shapes=[pltpu.VMEM((tm, k), BF16)],
        compiler_params=_cparams(("arbitrary", "arbitrary")),
        name="proj",
    )(x, w, col_scale)


def _gates_body(mode, tm, steps_per_seq, x_ref, w_ref, p0_ref, p1_ref, o_ref, carry_ref):
    z = _dot(x_ref[...], w_ref[...], precision=HIGHEST)
    ri = lax.broadcasted_iota(I32, (tm, tm), 0)
    ci = lax.broadcasted_iota(I32, (tm, tm), 1)
    if mode == "even":
        lane = lax.broadcasted_iota(I32, (tm, LANES), 1)
        beta = _sigmoid(z)
        g = -jnp.exp(p0_ref[...]) * _softplus(z + p1_ref[...])
        shift = int(math.log2(GDN_CHUNK))
        same_chunk = jnp.right_shift(ri, shift) == jnp.right_shift(ci, shift)
        tri = jnp.where(jnp.logical_and(same_chunk, ci <= ri), 1.0, 0.0)
        gc = _dot(tri, g, precision=HIGHEST)
        o_ref[...] = jnp.where(lane < B_HEADS, beta, gc)
    else:
        @pl.when(pl.program_id(0) % steps_per_seq == 0)
        def _():
            carry_ref[...] = jnp.zeros_like(carry_ref)

        logf = -_softplus(-(z + p0_ref[...]))
        tri = jnp.where(ci <= ri, 1.0, 0.0)
        cum = _dot(tri, logf, precision=HIGHEST) + carry_ref[0:1, :]
        o_ref[...] = cum
        carry_ref[...] = jnp.broadcast_to(cum[tm - 1:tm, :], carry_ref.shape)


def _gates(mode, x, w_small, p0, p1, seq_len, tm=512):
    n, k = x.shape
    tm = min(tm, seq_len)
    row = pl.BlockSpec((1, LANES), lambda i: (0, 0))
    return pl.pallas_call(
        functools.partial(_gates_body, mode, tm, seq_len // tm),
        grid=(n // tm,),
        in_specs=[pl.BlockSpec((tm, k), lambda i: (i, 0)),
                  pl.BlockSpec((k, LANES), lambda i: (0, 0)), row, row],
        out_specs=pl.BlockSpec((tm, LANES), lambda i: (i, 0)),
        out_shape=jax.ShapeDtypeStruct((n, LANES), F32),
        scratch_shapes=[pltpu.VMEM((SUBLANES, LANES), F32)],
        compiler_params=_cparams(("arbitrary",)),
        name="gates_" + mode,
    )(x, w_small, p0, p1)


def _pad_lanes(a):
    return jnp.pad(a, [(0, 0)] * (a.ndim - 1) + [(0, LANES - a.shape[-1])])


def _qknorm_body(x_ref, w_ref, o_ref):
    x = x_ref[...].astype(F32)
    lane = lax.broadcasted_iota(I32, x.shape, 1)
    lo = lane < C_HEAD_DIM
    ss = x * x
    s_lo = jnp.sum(jnp.where(lo, ss, 0.0), axis=-1, keepdims=True)
    s_hi = jnp.sum(jnp.where(lo, 0.0, ss), axis=-1, keepdims=True)
    inv = jnp.where(lo, lax.rsqrt(s_lo / C_HEAD_DIM + RMS_EPS), lax.rsqrt(s_hi / C_HEAD_DIM + RMS_EPS))
    o_ref[...] = (x * inv * w_ref[...]).astype(o_ref.dtype)


def _qknorm(proj, w2, tm=1024):
    n = proj.shape[0]
    tm = min(tm, n)
    nblk = 2 * C_HEADS * C_HEAD_DIM // LANES
    return pl.pallas_call(
        _qknorm_body,
        grid=(n // tm, nblk),
        in_specs=[pl.BlockSpec((tm, LANES), lambda i, c: (i, c)),
                  pl.BlockSpec((None, 1, LANES), lambda i, c: (c // (nblk // 2), 0, 0))],
        out_specs=pl.BlockSpec((tm, LANES), lambda i, c: (i, c)),
        out_shape=jax.ShapeDtypeStruct((n, nblk * LANES), BF16),
        compiler_params=_cparams(("parallel", "parallel")),
        name="fox_qknorm",
    )(proj, w2)


def _attn_body(mode, t, lam_init, ii_ref, jj_ref, q_ref, k_ref, v_ref, *rest):
    if mode == "diff":
        strip_ref, far_ref, lam_ref, subln_ref, o_ref, qs_ref, m_ref, acc_ref, l_ref = rest
    else:
        ck_ref, c0_ref, gate_ref, o_ref, qs_ref, m_ref, acc_ref = rest
    p = pl.program_id(2)
    i = ii_ref[p]
    j = jj_ref[p]
    rc = min(ATTN_ROWS, t)

    @pl.when(j == 0)
    def _():
        q = q_ref[...]
        lane = lax.broadcasted_iota(I32, q.shape, 1)
        zero = jnp.zeros_like(q)
        qs_ref[0:t, :] = jnp.where(lane < 64, q, zero)
        qs_ref[t:2 * t, :] = jnp.where(lane < 64, zero, q)
        m_ref[...] = jnp.full(m_ref.shape, NEG_INF, F32)
        acc_ref[...] = jnp.zeros_like(acc_ref)
        if mode == "diff":
            l_ref[...] = jnp.zeros_like(l_ref)

    def step(kind):
        k = k_ref[...]
        v = v_ref[...]
        if mode == "fox":
            lane = lax.broadcasted_iota(I32, v.shape, 1)
            one = jnp.ones_like(v)
            rhs = (jnp.where(lane < 64, v, one), jnp.where(lane < 64, one, v))
            bias = (c0_ref[:, 0:1] - ck_ref[...]) * LOG2E
        elif kind != "far":
            x = jnp.broadcast_to(strip_ref[...], (t, 2 * t))
            tile = pltpu.roll(x, t + 1, 1, stride=1, stride_axis=0)[:, :t]
        for c in range(2 * t // rc):
            r0 = c * rc
            half = r0 // t
            h0 = r0 - half * t
            rows = slice(r0, r0 + rc)
            s = _dot_nt(qs_ref[rows, :], k)
            if mode == "fox":
                s = s + bias[half:half + 1]
            elif kind == "far":
                s = s + far_ref[0:1, 0:1]
            else:
                s = s + tile[h0:h0 + rc]
            if kind == "diag":
                rr = lax.broadcasted_iota(I32, (rc, t), 0) + h0
                cc = lax.broadcasted_iota(I32, (rc, t), 1)
                s = jnp.where(rr >= cc, s, NEG_INF)
            m_prev = m_ref[rows, :]
            m_new = jnp.maximum(m_prev, jnp.max(s, axis=1, keepdims=True))
            alpha = jnp.exp2(m_prev - m_new)
            pr = jnp.exp2(s - pltpu.repeat(m_new, t // LANES, 1))
            if mode == "diff":
                l_ref[rows, :] = alpha * l_ref[rows, :] + jnp.sum(pr, axis=1, keepdims=True)
                pv = _dot(pr.astype(BF16), v)
            else:
                pv = _dot(pr.astype(BF16), rhs[half])
            acc_ref[rows, :] = alpha * acc_ref[rows, :] + pv
            m_ref[rows, :] = m_new

    if mode == "diff":
        d = i - j
        near = d * t - (t - 1) < T5_MAX_DISTANCE
        pl.when(d == 0)(functools.partial(step, "diag"))
        pl.when(jnp.logical_and(d > 0, near))(functools.partial(step, "near"))
        pl.when(jnp.logical_not(near))(functools.partial(step, "far"))
    else:
        pl.when(j == i)(functools.partial(step, "diag"))
        pl.when(j != i)(functools.partial(step, "off"))

    @pl.when(j == i)
    def _():
        acc = acc_ref[...]
        if mode == "diff":
            o = acc * (1.0 / l_ref[...])
            lp = lam_ref[...]
            lam = (jnp.exp(jnp.sum(lp[0:1] * lp[1:2], axis=-1, keepdims=True))
                   - jnp.exp(jnp.sum(lp[2:3] * lp[3:4], axis=-1, keepdims=True)) + lam_init)
            dlt = o[0:t] - lam * o[t:2 * t]
            ms = jnp.mean(dlt * dlt, axis=-1, keepdims=True)
            out = dlt * lax.rsqrt(ms + RMS_EPS) * subln_ref[...] * (1.0 - lam_init)
        else:
            lane = lax.broadcasted_iota(I32, (t, LANES), 1)
            lo = acc[0:t]
            hi = acc[t:2 * t]
            out = jnp.where(lane < 64, lo * (1.0 / lo[:, 64:65]), hi * (1.0 / hi[:, 0:1]))
            out = out * _sigmoid(gate_ref[...].astype(F32))
        o_ref[...] = out.astype(o_ref.dtype)


def _causal_pairs(nq):
    ii, jj = [], []
    for i in range(nq):
        for j in range(i + 1):
            ii.append(i)
            jj.append(j)
    return jnp.asarray(np.array(ii, np.int32)), jnp.asarray(np.array(jj, np.int32))


def _attn_scratch(mode, t):
    base = [pltpu.VMEM((2 * t, LANES), BF16), pltpu.VMEM((2 * t, LANES), F32), pltpu.VMEM((2 * t, LANES), F32)]
    return base + ([pltpu.VMEM((2 * t, LANES), F32)] if mode == "diff" else [])


def _diff_attention(proj, bsz, seq, strips, far, lam_params, subln_w, lam_init, t):
    nq = seq // t
    ii, jj = _causal_pairs(nq)
    nd = strips.shape[1]
    h_ = A_HEADS
    grid_spec = pltpu.PrefetchScalarGridSpec(
        num_scalar_prefetch=2,
        grid=(bsz, h_, int(ii.shape[0])),
        in_specs=[
            pl.BlockSpec((t, LANES), lambda b, h, p, ii, jj: (b * nq + ii[p], h)),
            pl.BlockSpec((t, LANES), lambda b, h, p, ii, jj: (b * nq + jj[p], h_ + h)),
            pl.BlockSpec((t, LANES), lambda b, h, p, ii, jj: (b * nq + jj[p], 2 * h_ + h)),
            pl.BlockSpec((None, None, 1, 2 * t),
                         lambda b, h, p, ii, jj: (h, jnp.minimum(ii[p] - jj[p], nd - 1), 0, 0)),
            pl.BlockSpec((None, 1, LANES), lambda b, h, p, ii, jj: (h, 0, 0)),
            pl.BlockSpec(lam_params.shape, lambda b, h, p, ii, jj: (0, 0)),
            pl.BlockSpec((1, LANES), lambda b, h, p, ii, jj: (0, 0)),
        ],
        out_specs=pl.BlockSpec((t, LANES), lambda b, h, p, ii, jj: (b * nq + ii[p], h)),
        scratch_shapes=_attn_scratch("diff", t),
    )
    return pl.pallas_call(
        functools.partial(_attn_body, "diff", t, lam_init),
        grid_spec=grid_spec,
        out_shape=jax.ShapeDtypeStruct((bsz * seq, h_ * LANES), BF16),
        compiler_params=_cparams(("parallel", "parallel", "arbitrary")),
        name="diff_attn",
    )(ii, jj, proj, proj, proj, strips, far, lam_params, subln_w)


def _fox_attention(qk, proj, cum_t, bsz, seq, t):
    nq = seq // t
    ii, jj = _causal_pairs(nq)
    hp = C_HEADS // 2
    grid_spec = pltpu.PrefetchScalarGridSpec(
        num_scalar_prefetch=2,
        grid=(bsz, hp, int(ii.shape[0])),
        in_specs=[
            pl.BlockSpec((t, LANES), lambda b, h, p, ii, jj: (b * nq + ii[p], h)),
            pl.BlockSpec((t, LANES), lambda b, h, p, ii, jj: (b * nq + jj[p], hp + h)),
            pl.BlockSpec((t, LANES), lambda b, h, p, ii, jj: (b * nq + jj[p], 2 * hp + h)),
            pl.BlockSpec((None, None, 2, t), lambda b, h, p, ii, jj: (b, h, 0, jj[p])),
            pl.BlockSpec((None, None, 2, t), lambda b, h, p, ii, jj: (b, h, 0, ii[p])),
            pl.BlockSpec((t, LANES), lambda b, h, p, ii, jj: (b * nq + ii[p], 3 * hp + h)),
        ],
        out_specs=pl.BlockSpec((t, LANES), lambda b, h, p, ii, jj: (b * nq + ii[p], h)),
        scratch_shapes=_attn_scratch("fox", t),
    )
    return pl.pallas_call(
        functools.partial(_attn_body, "fox", t, 0.0),
        grid_spec=grid_spec,
        out_shape=jax.ShapeDtypeStruct((bsz * seq, hp * LANES), BF16),
        compiler_params=_cparams(("parallel", "parallel", "arbitrary")),
        name="fox_attn",
    )(ii, jj, qk, qk, proj, cum_t, cum_t, proj)


def _t5_bias_by_distance(t5_bias, seq):
    n = jnp.arange(seq, dtype=I32)
    max_exact = T5_BUCKETS // 2
    nf = jnp.maximum(n, 1).astype(F32)
    large = max_exact + (jnp.log(nf / max_exact) / math.log(T5_MAX_DISTANCE / max_exact)
                         * (T5_BUCKETS - max_exact)).astype(I32)
    large = jnp.minimum(large, T5_BUCKETS - 1)
    bucket = jnp.where(n < max_exact, n, large)
    return t5_bias.astype(F32).T[:, bucket]


def _t5_strips(t5_bias, seq, t):
    vec = _t5_bias_by_distance(t5_bias, seq) * LOG2E
    nd = 1
    while nd * t - (t - 1) < T5_MAX_DISTANCE and nd < seq // t:
        nd += 1
    d = np.arange(nd)[:, None]
    c = np.arange(2 * t)[None, :]
    rel = d * t + t - 1 - c
    rel = np.clip(rel, 0, seq - 1)
    strips = vec[:, jnp.asarray(rel.astype(np.int32))]
    far = jnp.broadcast_to((t5_bias.astype(F32).T[:, T5_BUCKETS - 1] * LOG2E)[:, None, None],
                           (t5_bias.shape[1], 1, LANES))
    return strips[:, :, None, :], far


def _gdn_body(tb, xq_ref, xk_ref, xv_ref, z_ref, gcol_ref, grow_ref, cw_ref, nw_ref, o_ref, s_ref, carry_ref):
    h = pl.program_id(1)
    c_ = GDN_CHUNK

    @pl.when(pl.program_id(2) == 0)
    def _():
        s_ref[...] = jnp.zeros_like(s_ref)
        carry_ref[...] = jnp.zeros_like(carry_ref)

    cw = cw_ref[...]

    def conv_silu(idx, x_ref):
        x = x_ref[...].astype(F32)
        xe = jnp.concatenate([carry_ref[idx], x], axis=0)
        y = cw[4 * idx + 3:4 * idx + 4] * x
        for tap in range(CONV_WIDTH - 1):
            shift = CONV_WIDTH - 1 - tap
            y = y + cw[4 * idx + tap:4 * idx + tap + 1] * pltpu.roll(xe, shift, 0)[SUBLANES:]
        carry_ref[idx] = x[tb - SUBLANES:]
        return y * _sigmoid(y)

    q = conv_silu(0, xq_ref)
    k = conv_silu(1, xk_ref)
    v = conv_silu(2, xv_ref)
    q = q * lax.rsqrt(jnp.sum(q * q, axis=-1, keepdims=True) + RMS_EPS) * (B_HEAD_DIM ** -0.5)
    k = k * lax.rsqrt(jnp.sum(k * k, axis=-1, keepdims=True) + RMS_EPS)

    gates = gcol_ref[...]
    lane = lax.broadcasted_iota(I32, gates.shape, 1)
    beta = jnp.sum(jnp.where(lane == h, gates, 0.0), axis=1, keepdims=True)
    gc = jnp.sum(jnp.where(lane == B_HEADS + h, gates, 0.0), axis=1, keepdims=True)
    gc_row = grow_ref[1:2, :]
    egc = jnp.exp(gc)
    kb = k * beta
    rhs = jnp.concatenate([v * beta, kb * egc], axis=1)
    qd = q * egc

    ri = lax.broadcasted_iota(I32, (c_, c_), 0)
    ci = lax.broadcasted_iota(I32, (c_, c_), 1)
    state = s_ref[...]
    outs = []
    for c in range(tb // c_):
        sl = slice(c * c_, (c + 1) * c_)
        gcc = gc[sl]
        decay = jnp.exp(jnp.where(ri >= ci, gcc - gc_row[:, sl], NEG_INF))
        kc = k[sl].astype(BF16)
        kk = _dot_nt(kb[sl].astype(BF16), kc)
        qk = _dot_nt(q[sl].astype(BF16), kc) * decay
        x = -(kk * jnp.where(ri > ci, decay, 0.0))
        r = x
        pw = x
        for _ in range(int(math.log2(c_)) - 1):
            pwb = pw.astype(BF16)
            pw = _dot(pwb, pwb)
            r = r + pw + _dot(r.astype(BF16), pw.astype(BF16))
        sol = rhs[sl] + _dot(r.astype(BF16), rhs[sl].astype(BF16))
        u = sol[:, :B_HEAD_DIM]
        w = sol[:, B_HEAD_DIM:]
        g_last = gcc[c_ - 1:c_]
        kd = k[sl] * jnp.exp(g_last - gcc)
        sb = state.astype(BF16)
        v_new = u - _dot(w.astype(BF16), sb)
        vnb = v_new.astype(BF16)
        outs.append(_dot(qd[sl].astype(BF16), sb) + _dot(qk.astype(BF16), vnb))
        state = state * jnp.exp(g_last) + _dot_tn(kd.astype(BF16), vnb)
    s_ref[...] = state
    o = jnp.concatenate(outs, axis=0)
    o = o * lax.rsqrt(jnp.mean(o * o, axis=-1, keepdims=True) + RMS_EPS) * nw_ref[...]
    z = z_ref[...].astype(F32)
    o_ref[...] = (o * (z * _sigmoid(z))).astype(o_ref.dtype)


def _gdn(proj, gates, gates_row, conv_w, norm_w, bsz, seq, tb):
    nt = seq // tb
    h_ = B_HEADS
    col0 = 3 * A_HEADS
    blk = lambda off: pl.BlockSpec((tb, LANES), lambda b, h, s: (b * nt + s, col0 + off * h_ + h))
    return pl.pallas_call(
        functools.partial(_gdn_body, tb),
        grid=(bsz, h_, nt),
        in_specs=[blk(0), blk(1), blk(2), blk(3),
                  pl.BlockSpec((tb, LANES), lambda b, h, s: (b * nt + s, 0)),
                  pl.BlockSpec((None, 2, tb), lambda b, h, s: (b * h_ + h, 0, s)),
                  pl.BlockSpec((None, 3 * CONV_WIDTH, LANES), lambda b, h, s: (h, 0, 0)),
                  pl.BlockSpec((1, LANES), lambda b, h, s: (0, 0))],
        out_specs=pl.BlockSpec((tb, LANES), lambda b, h, s: (b * nt + s, h)),
        out_shape=jax.ShapeDtypeStruct((bsz * seq, h_ * LANES), BF16),
        scratch_shapes=[pltpu.VMEM((B_HEAD_DIM, B_HEAD_DIM), F32),
                        pltpu.VMEM((3, SUBLANES, LANES), F32)],
        compiler_params=_cparams(("parallel", "parallel", "arbitrary")),
        name="gdn",
    )(proj, proj, proj, proj, gates, gates_row, conv_w, norm_w)


ROW_TILES = D_MODEL // LANES


def _store_tile_rows(ref, y):
    t = y.shape[0]
    for c in range(ROW_TILES):
        ref[pl.ds(c, t, stride=ROW_TILES), :] = y[:, c * LANES:(c + 1) * LANES]


def _load_tile_rows(ref, t, lead=()):
    return jnp.concatenate([ref[lead + (pl.ds(c, t, stride=ROW_TILES), slice(None))] for c in range(ROW_TILES)],
                           axis=1)


def _outproj_ln_body(a_ref, b_ref, w_ref, x_ref, g_ref, bb_ref, o_ref, o3_ref, wb_ref):
    @pl.when(pl.program_id(0) == 0)
    def _():
        wb_ref[...] = w_ref[...].astype(BF16)

    half = a_ref.shape[1]
    hmix = _dot(a_ref[...], wb_ref[0:half, :]) + _dot(b_ref[...], wb_ref[half:, :])
    out = _layer_norm(DEEPNORM_ALPHA * x_ref[...] + hmix, g_ref[...], bb_ref[...])
    o_ref[...] = out
    _store_tile_rows(o3_ref, out)


def _outproj_ln(a, a_blk, b, b_blk, w, x, g, bb, tm=512):
    n, d = x.shape
    tm = min(tm, n)
    half = d // 2
    row = pl.BlockSpec((1, d), lambda i: (0, 0))
    return pl.pallas_call(
        _outproj_ln_body,
        grid=(n // tm,),
        in_specs=[pl.BlockSpec((tm, half), lambda i: (i, a_blk)),
                  pl.BlockSpec((tm, half), lambda i: (i, b_blk)),
                  pl.BlockSpec((d, d), lambda i: (0, 0)),
                  pl.BlockSpec((tm, d), lambda i: (i, 0)), row, row],
        out_specs=[pl.BlockSpec((tm, d), lambda i: (i, 0)),
                   pl.BlockSpec((tm * ROW_TILES, LANES), lambda i: (i, 0))],
        out_shape=[jax.ShapeDtypeStruct((n, d), F32), jax.ShapeDtypeStruct((n * ROW_TILES, LANES), F32)],
        scratch_shapes=[pltpu.VMEM((d, d), BF16)],
        compiler_params=_cparams(("arbitrary",)),
        name="outproj_ln",
    )(a, b, w, x, g, bb)


def _router_body(tm, x_ref, w_ref, b_ref, idx_ref, gate_ref, rank_ref, cnt_ref, carry_ref):
    @pl.when(pl.program_id(0) == 0)
    def _():
        carry_ref[...] = jnp.zeros_like(carry_ref)

    logits = _dot(x_ref[...], w_ref[...], precision=HIGHEST) + b_ref[...]
    lg = jnp.transpose(logits)[0:N_EXPERTS, :]
    e_iota = lax.broadcasted_iota(I32, (N_EXPERTS, tm), 0).astype(F32)
    vals, idxs, hots = [], [], []
    for _ in range(TOP_K):
        m = jnp.max(lg, axis=0, keepdims=True)
        idx = jnp.min(jnp.where(lg == m, e_iota, float(N_EXPERTS)), axis=0, keepdims=True)
        hot = e_iota == idx
        lg = jnp.where(hot, NEG_INF, lg)
        vals.append(m)
        idxs.append(idx)
        hots.append(hot)
    es = [jnp.exp(v - vals[0]) for v in vals]
    den = es[0] + es[1] + es[2] + es[3]
    sel = jnp.zeros((N_EXPERTS, tm), F32)
    for hot in hots:
        sel = sel + jnp.where(hot, 1.0, 0.0)
    before = jnp.where(lax.broadcasted_iota(I32, (tm, tm), 0) < lax.broadcasted_iota(I32, (tm, tm), 1),
                       1.0, 0.0).astype(BF16)
    cum = _dot(sel.astype(BF16), before) + carry_ref[:, 0:1]
    ranks = [jnp.sum(jnp.where(hot, cum, 0.0), axis=0, keepdims=True) for hot in hots]
    total = carry_ref[...] + jnp.sum(sel, axis=1, keepdims=True)
    carry_ref[...] = total
    idx_ref[...] = jnp.concatenate(idxs, axis=0).astype(I32)
    gate_ref[...] = jnp.concatenate([e / den for e in es], axis=0)
    rank_ref[...] = jnp.concatenate(ranks, axis=0).astype(I32)
    cnt_ref[...] = total


def _router(x, w_pad, b_pad, tm=512):
    n, d = x.shape
    tm = min(tm, n)
    out4 = lambda dt: jax.ShapeDtypeStruct((TOP_K, n), dt)
    blk4 = pl.BlockSpec((TOP_K, tm), lambda i: (0, i))
    return pl.pallas_call(
        functools.partial(_router_body, tm),
        grid=(n // tm,),
        in_specs=[pl.BlockSpec((tm, d), lambda i: (i, 0)),
                  pl.BlockSpec((d, LANES), lambda i: (0, 0)),
                  pl.BlockSpec((1, LANES), lambda i: (0, 0))],
        out_specs=[blk4, blk4, blk4, pl.BlockSpec((N_EXPERTS, LANES), lambda i: (0, 0))],
        out_shape=[out4(I32), out4(F32), out4(I32), jax.ShapeDtypeStruct((N_EXPERTS, LANES), F32)],
        scratch_shapes=[pltpu.VMEM((N_EXPERTS, LANES), F32)],
        compiler_params=_cparams(("arbitrary",)),
        name="moe_router",
    )(x, w_pad, b_pad)


def _row_slab(row):
    return pl.ds(pl.multiple_of(row * ROW_TILES, ROW_TILES), ROW_TILES)


def _row_dma_loops(n_rows, make_copy):
    def issue(r, carry):
        for k in range(TOP_K):
            make_copy(k, r).start()
        return carry

    def drain(r, carry):
        for k in range(TOP_K):
            make_copy(k, r).wait()
        return carry

    lax.fori_loop(0, n_rows, issue, 0, unroll=4)
    lax.fori_loop(0, n_rows, drain, 0, unroll=8)


def _dispatch_body(td, dest_ref, x3_ref, xg_in, xg_out, sem):
    del xg_in
    _row_dma_loops(td, lambda k, r: pltpu.make_async_copy(
        x3_ref.at[_row_slab(r)], xg_out.at[_row_slab(dest_ref[k, r])], sem))


def _smem_rows(tile):
    return pl.BlockSpec((TOP_K, tile), lambda i: (0, i), memory_space=pltpu.SMEM)


def _dispatch(x3, dest, n_rows, td=512):
    n = x3.shape[0] // ROW_TILES
    td = min(td, n)
    xg0 = jnp.zeros((n_rows * ROW_TILES, LANES), x3.dtype)
    return pl.pallas_call(
        functools.partial(_dispatch_body, td),
        grid=(n // td,),
        in_specs=[_smem_rows(td),
                  pl.BlockSpec((td * ROW_TILES, LANES), lambda i: (i, 0)),
                  pl.BlockSpec(memory_space=pl.ANY)],
        out_specs=pl.BlockSpec(memory_space=pl.ANY),
        out_shape=jax.ShapeDtypeStruct(xg0.shape, x3.dtype),
        scratch_shapes=[pltpu.SemaphoreType.DMA(())],
        input_output_aliases={2: 0},
        compiler_params=_cparams(("arbitrary",)),
        name="moe_dispatch",
    )(dest, x3, xg0)


def _expert_body(be_ref, nu_ref, x_ref, wgu_ref, bgu_ref, wd_ref, bd_ref, o_ref, wgu_b, wd_b):
    i = pl.program_id(0)
    prev = be_ref[jnp.maximum(i - 1, 0)]
    fresh = jnp.logical_or(i == 0, be_ref[i] != prev)

    @pl.when(jnp.logical_and(i < nu_ref[0], fresh))
    def _():
        wgu_b[...] = wgu_ref[...].astype(BF16)
        wd_b[...] = wd_ref[...].astype(BF16)

    @pl.when(i < nu_ref[0])
    def _():
        x = _load_tile_rows(x_ref, MOE_ROWS).astype(BF16)
        hcat = _dot(x, wgu_b[...]) + bgu_ref[...]
        g = jnp.minimum(hcat[:, :D_FF], SWIGLU_LIMIT)
        u = jnp.clip(hcat[:, D_FF:], -SWIGLU_LIMIT, SWIGLU_LIMIT)
        act = g * _sigmoid(SWIGLU_ALPHA * g) * (u + 1.0)
        _store_tile_rows(o_ref, _dot(act.astype(BF16), wd_b[...]) + bd_ref[...])

    @pl.when(i >= nu_ref[0])
    def _():
        o_ref[...] = jnp.zeros_like(o_ref)


def _experts(xg, layer, block_expert, n_used, w_gu, b_gu, w_down, b_down):
    d = D_MODEL
    nb = xg.shape[0] // (MOE_ROWS * ROW_TILES)
    blk = pl.BlockSpec((MOE_ROWS * ROW_TILES, LANES), lambda i, be, nu: (i, 0))
    grid_spec = pltpu.PrefetchScalarGridSpec(
        num_scalar_prefetch=2,
        grid=(nb,),
        in_specs=[
            blk,
            pl.BlockSpec((None, None, d, 2 * D_FF), lambda i, be, nu: (layer, be[i], 0, 0)),
            pl.BlockSpec((None, None, 1, 2 * D_FF), lambda i, be, nu: (layer, be[i], 0, 0)),
            pl.BlockSpec((None, None, D_FF, d), lambda i, be, nu: (layer, be[i], 0, 0)),
            pl.BlockSpec((None, None, 1, d), lambda i, be, nu: (layer, be[i], 0, 0)),
        ],
        out_specs=blk,
        scratch_shapes=[pltpu.VMEM((d, 2 * D_FF), BF16), pltpu.VMEM((D_FF, d), BF16)],
    )
    return pl.pallas_call(
        _expert_body,
        grid_spec=grid_spec,
        out_shape=jax.ShapeDtypeStruct(xg.shape, F32),
        compiler_params=_cparams(("arbitrary",)),
        name="moe_experts",
    )(block_expert, n_used, xg, w_gu, b_gu[:, :, None, :], w_down, b_down[:, :, None, :])


def _combine_ln_body(tc, dest_ref, yg_hbm, gate_ref, x_ref, g_ref, b_ref, o_ref, ybuf, sem):
    _row_dma_loops(tc, lambda k, r: pltpu.make_async_copy(
        yg_hbm.at[_row_slab(dest_ref[k, r])], ybuf.at[k, _row_slab(r)], sem))
    gate = gate_ref[...]
    hmoe = gate[:, 0:1] * _load_tile_rows(ybuf, tc, (0,))
    for k in range(1, TOP_K):
        hmoe = hmoe + gate[:, k:k + 1] * _load_tile_rows(ybuf, tc, (k,))
    o_ref[...] = _layer_norm(DEEPNORM_ALPHA * x_ref[...] + hmoe, g_ref[...], b_ref[...])


def _combine_ln(yg, dest, gate_t, x, g, b, tc=256):
    n, d = x.shape
    tc = min(tc, n)
    row = pl.BlockSpec((1, d), lambda i: (0, 0))
    return pl.pallas_call(
        functools.partial(_combine_ln_body, tc),
        grid=(n // tc,),
        in_specs=[_smem_rows(tc),
                  pl.BlockSpec(memory_space=pl.ANY),
                  pl.BlockSpec((tc, TOP_K), lambda i: (i, 0)),
                  pl.BlockSpec((tc, d), lambda i: (i, 0)), row, row],
        out_specs=pl.BlockSpec((tc, d), lambda i: (i, 0)),
        out_shape=jax.ShapeDtypeStruct((n, d), F32),
        scratch_shapes=[pltpu.VMEM((TOP_K, tc * ROW_TILES, LANES), F32), pltpu.SemaphoreType.DMA(())],
        compiler_params=_cparams(("arbitrary",)),
        name="moe_combine_ln",
    )(dest, yg, gate_t, x, g, b)


def _moe_ln(x, x3, layer, router_w, router_b, w_gu, b_gu, w_down, b_down, ln_g, ln_b):
    n, d = x.shape
    idx, gate, rank, cnt = _router(x, _pad_lanes(router_w), _pad_lanes(router_b[None, :]))
    counts = cnt[:, 0].astype(I32)
    padded = (counts + MOE_ROWS - 1) // MOE_ROWS * MOE_ROWS
    pad_end = jnp.cumsum(padded)
    pad_start = pad_end - padded
    hot = idx[:, :, None] == jnp.arange(N_EXPERTS, dtype=I32)[None, None, :]
    dest = jnp.sum(jnp.where(hot, pad_start[None, None, :], 0), axis=-1) + rank
    n_blocks = -(-(n * TOP_K) // MOE_ROWS) + N_EXPERTS
    block_row0 = jnp.arange(n_blocks, dtype=I32) * MOE_ROWS
    block_expert = jnp.minimum(jnp.sum((pad_end[None, :] <= block_row0[:, None]).astype(I32), axis=1),
                               N_EXPERTS - 1)
    n_used = (pad_end[-1:] // MOE_ROWS).astype(I32)
    xg = _dispatch(x3, dest, n_blocks * MOE_ROWS)
    yg = _experts(xg, layer, block_expert, n_used, w_gu, b_gu, w_down, b_down)
    return _combine_ln(yg, dest, gate.T, x, ln_g[None, :], ln_b[None, :])


def _even_mixer_ln(x, bsz, seq, w_in, w_out, lam_params, subln_w, conv_w, a_log, dt_bias, gdn_norm_w,
                   t5_bias, lam_init, ln_g, ln_b):
    main = 3 * A_HEADS * LANES + 4 * B_HEADS * B_HEAD_DIM
    n_q = A_HEADS * 2 * A_HEAD_DIM
    col_scale = jnp.concatenate([jnp.full((1, n_q), A_HEAD_DIM ** -0.5 * LOG2E, F32),
                                 jnp.ones((1, main - n_q), F32)], axis=1)
    proj = _proj(x, w_in[:, :main], col_scale)
    zeros4 = jnp.zeros((B_HEADS,), F32)
    p0 = _pad_lanes(jnp.concatenate([zeros4, a_log.astype(F32)])[None, :])
    p1 = _pad_lanes(jnp.concatenate([zeros4, dt_bias.astype(F32)])[None, :])
    gates = _gates("even", x, _pad_lanes(w_in[:, main:]), p0, p1, seq)
    t = min(ATTN_TILE, seq)
    strips, far = _t5_strips(t5_bias, seq, t)
    ao = _diff_attention(proj, bsz, seq, strips, far, lam_params.astype(F32), subln_w[None, :].astype(F32),
                         lam_init, t)
    g8 = gates[:, :2 * B_HEADS].reshape(bsz, seq, 2, B_HEADS)
    gates_row = g8.transpose(0, 3, 2, 1).reshape(bsz * B_HEADS, 2, seq)
    cw = conv_w.astype(F32).reshape(CONV_WIDTH, 3, B_HEADS, B_HEAD_DIM).transpose(2, 1, 0, 3)
    cw = cw.reshape(B_HEADS, 3 * CONV_WIDTH, B_HEAD_DIM)
    bo = _gdn(proj, gates, gates_row, cw, gdn_norm_w[None, :].astype(F32), bsz, seq, min(GDN_TILE, seq))
    return _outproj_ln(ao, 0, bo, 0, w_out, x, ln_g[None, :], ln_b[None, :])


def _odd_mixer_ln(x, bsz, seq, w_in, w_out, qk_norm_w, forget_b, ln_g, ln_b):
    main = 4 * C_HEADS * C_HEAD_DIM
    proj = _proj(x, w_in[:, :main], jnp.ones((1, main), F32))
    fb =_pad_lanes(forget_b.astype(F32)[None, :])
    cum = _gates("odd", x, _pad_lanes(w_in[:, main:]), fb, fb, seq)
    cum_t = cum[:, :C_HEADS].reshape(bsz, seq, C_HEADS // 2, 2).transpose(0, 2, 3, 1)
    scale = jnp.asarray([C_HEAD_DIM ** -0.5 * LOG2E, 1.0], F32)[:, None]
    w2 = jnp.tile(qk_norm_w.astype(F32) * scale, (1, 2))[:, None, :]
    qk = _qknorm(proj, w2)
    o = _fox_attention(qk, proj, cum_t, bsz, seq, min(ATTN_TILE, seq))
    return _outproj_ln(o, 0, o, 1, w_out, x, ln_g[None, :], ln_b[None, :])


def kernel(x, t5_bias, even_w_in, even_w_out, diff_lambda, diff_subln_w, gdn_conv_w, gdn_a_log, gdn_dt_bias, gdn_norm_w, odd_w_in, odd_w_out, fox_qk_norm_w, fox_forget_b, router_w, router_b, moe_w_gate_up, moe_b_gate_up, moe_w_down, moe_b_down, ln_mix_g, ln_mix_b, ln_ffn_g, ln_ffn_b):
    bsz, seq, d = x.shape
    xf = x.reshape(bsz * seq, d)
    for layer in range(DEPTH):
        i = layer // 2
        if layer % 2 == 0:
            lam_init = 0.8 - 0.6 * math.exp(-0.3 * layer)
            xf, x3 = _even_mixer_ln(xf, bsz, seq, even_w_in[i], even_w_out[i], diff_lambda[i], diff_subln_w[i],
                                    gdn_conv_w[i], gdn_a_log[i], gdn_dt_bias[i], gdn_norm_w[i], t5_bias, lam_init,
                                    ln_mix_g[layer], ln_mix_b[layer])
        else:
            xf, x3 = _odd_mixer_ln(xf, bsz, seq, odd_w_in[i], odd_w_out[i], fox_qk_norm_w[i], fox_forget_b[i],
                                   ln_mix_g[layer], ln_mix_b[layer])
        xf = _moe_ln(xf, x3, layer, router_w[layer], router_b[layer], moe_w_gate_up, moe_b_gate_up,
                     moe_w_down, moe_b_down, ln_ffn_g[layer], ln_ffn_b[layer])
    return xf.reshape(bsz, seq, d)
```

```python
import functools
import math

import numpy as np
import jax
import jax.numpy as jnp
from jax import lax
from jax.experimental import pallas as pl
from jax.experimental.pallas import tpu as pltpu

F32 = jnp.float32
BF16 = jnp.bfloat16
I32 = jnp.int32
HIGHEST = lax.Precision.HIGHEST

D_MODEL = 1024
DEPTH = 4
A_HEADS = 4
A_HEAD_DIM = 64
B_HEADS = 4
B_HEAD_DIM = 128
CONV_WIDTH = 4
C_HEADS = 16
C_HEAD_DIM = 64
T5_BUCKETS = 32
T5_MAX_DISTANCE = 2048
N_EXPERTS = 32
TOP_K = 4
D_FF = D_MODEL
SWIGLU_LIMIT = 7.0
SWIGLU_ALPHA = 1.702
DEEPNORM_ALPHA = (2 * DEPTH) ** 0.25
LN_EPS = 1e-5
RMS_EPS = 1e-6

LANES = 128
SUBLANES = 8
VMEM_LIMIT = 56 * 1024 * 1024

ATTN_TILE = 1024
ATTN_ROWS = 1024
GDN_CHUNK = 128
GDN_TILE = 512
MOE_ROWS = 256
FOX_SKIP_MARGIN = 170.0
NEG_INF = float("-inf")
LOG2E = math.log2(math.e)


def _cparams(sem, vmem=VMEM_LIMIT):
    return pltpu.CompilerParams(dimension_semantics=sem, vmem_limit_bytes=vmem)


def _dot(a, b, **kw):
    return jnp.dot(a, b, preferred_element_type=F32, **kw)


def _dot_nt(a, b, **kw):
    return lax.dot_general(a, b, (((1,), (1,)), ((), ())), preferred_element_type=F32, **kw)


def _dot_tn(a, b, **kw):
    return lax.dot_general(a, b, (((0,), (0,)), ((), ())), preferred_element_type=F32, **kw)


def _sigmoid(x):
    return 1.0 / (1.0 + jnp.exp(-x))


def _softplus(x):
    return jnp.maximum(x, 0.0) + jnp.log(1.0 + jnp.exp(-jnp.abs(x)))


def _layer_norm(xf, g, b):
    mu = jnp.mean(xf, axis=-1, keepdims=True)
    xc = xf - mu
    var = jnp.mean(xc * xc, axis=-1, keepdims=True)
    return xc * lax.rsqrt(var + LN_EPS) * g + b


def _proj_body(x_ref, w_ref, cs_ref, o_ref, xb_ref):
    @pl.when(pl.program_id(1) == 0)
    def _():
        xb_ref[...] = x_ref[...].astype(BF16)

    o_ref[...] = (_dot(xb_ref[...], w_ref[...].astype(BF16)) * cs_ref[...]).astype(o_ref.dtype)


def _proj(x, w, col_scale, tm=1024, tn=512):
    n, k = x.shape
    m = w.shape[1]
    tm = min(tm, n)
    return pl.pallas_call(
        _proj_body,
        grid=(n // tm, m // tn),
        in_specs=[pl.BlockSpec((tm, k), lambda i, j: (i, 0)),
                  pl.BlockSpec((k, tn), lambda i, j: (0, j)),
                  pl.BlockSpec((1, tn), lambda i, j: (0, j))],
        out_specs=pl.BlockSpec((tm, tn), lambda i, j: (i, j)),
        out_shape=jax.ShapeDtypeStruct((n, m), BF16),
        scratch_shapes=[pltpu.VMEM((tm, k), BF16)],
        compiler_params=_cparams(("arbitrary", "arbitrary")),
        name="proj",
    )(x, w, col_scale)


def _gates_body(mode, tm, steps_per_seq, x_ref, w_ref, p0_ref, p1_ref, o_ref, carry_ref):
    z = _dot(x_ref[...], w_ref[...], precision=HIGHEST)
    ri = lax.broadcasted_iota(I32, (tm, tm), 0)
    ci = lax.broadcasted_iota(I32, (tm, tm), 1)
    if mode == "even":
        lane = lax.broadcasted_iota(I32, (tm, LANES), 1)
        beta = _sigmoid(z)
        g = -jnp.exp(p0_ref[...]) * _softplus(z + p1_ref[...])
        shift = int(math.log2(GDN_CHUNK))
        same_chunk = jnp.right_shift(ri, shift) == jnp.right_shift(ci, shift)
        tri = jnp.where(jnp.logical_and(same_chunk, ci <= ri), 1.0, 0.0)
        gc = _dot(tri, g, precision=HIGHEST)
        o_ref[...] = jnp.where(lane < B_HEADS, beta, gc)
    else:
        @pl.when(pl.program_id(0) % steps_per_seq == 0)
        def _():
            carry_ref[...] = jnp.zeros_like(carry_ref)

        logf = -_softplus(-(z + p0_ref[...]))
        tri = jnp.where(ci <= ri, 1.0, 0.0)
        cum = _dot(tri, logf, precision=HIGHEST) + carry_ref[0:1, :]
        o_ref[...] = cum
        carry_ref[...] = jnp.broadcast_to(cum[tm - 1:tm, :], carry_ref.shape)


def _gates(mode, x, w_small, p0, p1, seq_len, tm=512):
    n, k = x.shape
    tm = min(tm, seq_len)
    row = pl.BlockSpec((1, LANES), lambda i: (0, 0))
    return pl.pallas_call(
        functools.partial(_gates_body, mode, tm, seq_len // tm),
        grid=(n // tm,),
        in_specs=[pl.BlockSpec((tm, k), lambda i: (i, 0)),
                  pl.BlockSpec((k, LANES), lambda i: (0, 0)), row, row],
        out_specs=pl.BlockSpec((tm, LANES), lambda i: (i, 0)),
        out_shape=jax.ShapeDtypeStruct((n, LANES), F32),
        scratch_shapes=[pltpu.VMEM((SUBLANES, LANES), F32)],
        compiler_params=_cparams(("arbitrary",)),
        name="gates_" + mode,
    )(x, w_small, p0, p1)


def _pad_lanes(a):
    return jnp.pad(a, [(0, 0)] * (a.ndim - 1) + [(0, LANES - a.shape[-1])])


def _qknorm_body(x_ref, w_ref, o_ref):
    x = x_ref[...].astype(F32)
    lane = lax.broadcasted_iota(I32, x.shape, 1)
    lo = lane < C_HEAD_DIM
    ss = x * x
    s_lo = jnp.sum(jnp.where(lo, ss, 0.0), axis=-1, keepdims=True)
    s_hi = jnp.sum(jnp.where(lo, 0.0, ss), axis=-1, keepdims=True)
    inv = jnp.where(lo, lax.rsqrt(s_lo / C_HEAD_DIM + RMS_EPS), lax.rsqrt(s_hi / C_HEAD_DIM + RMS_EPS))
    o_ref[...] = (x * inv * w_ref[...]).astype(o_ref.dtype)


def _qknorm(proj, w2, tm=1024):
    n = proj.shape[0]
    tm = min(tm, n)
    nblk = 2 * C_HEADS * C_HEAD_DIM // LANES
    return pl.pallas_call(
        _qknorm_body,
        grid=(n // tm, nblk),
        in_specs=[pl.BlockSpec((tm, LANES), lambda i, c: (i, c)),
                  pl.BlockSpec((None, 1, LANES), lambda i, c: (c // (nblk // 2), 0, 0))],
        out_specs=pl.BlockSpec((tm, LANES), lambda i, c: (i, c)),
        out_shape=jax.ShapeDtypeStruct((n, nblk * LANES), BF16),
        compiler_params=_cparams(("parallel", "parallel")),
        name="fox_qknorm",
    )(proj, w2)


def _attn_body(mode, t, lam_init, ii_ref, jj_ref, *rest):
    if mode == "diff":
        q_ref, k_ref, v_ref, strip_ref, far_ref, lam_ref, subln_ref, o_ref, qs_ref, m_ref, acc_ref, l_ref = rest
    else:
        jmin_ref, q_ref, k_ref, v_ref, ck_ref, c0_ref, gate_ref, o_ref, qs_ref, m_ref, acc_ref = rest
    p = pl.program_id(2)
    i = ii_ref[p]
    j = jj_ref[p]
    rc = min(ATTN_ROWS, t)

    @pl.when(j == 0)
    def _():
        q = q_ref[...]
        lane = lax.broadcasted_iota(I32, q.shape, 1)
        zero = jnp.zeros_like(q)
        qs_ref[0:t, :] = jnp.where(lane < 64, q, zero)
        qs_ref[t:2 * t, :] = jnp.where(lane < 64, zero, q)
        m_ref[...] = jnp.full(m_ref.shape, NEG_INF, F32)
        acc_ref[...] = jnp.zeros_like(acc_ref)
        if mode == "diff":
            l_ref[...] = jnp.zeros_like(l_ref)

    def step(kind):
        k = k_ref[...]
        v = v_ref[...]
        if mode == "fox":
            lane = lax.broadcasted_iota(I32, v.shape, 1)
            one = jnp.ones_like(v)
            rhs = (jnp.where(lane < 64, v, one), jnp.where(lane < 64, one, v))
            bias = (c0_ref[:, 0:1] - ck_ref[...]) * LOG2E
        elif kind != "far":
            x = jnp.broadcast_to(strip_ref[...], (t, 2 * t))
            tile = pltpu.roll(x, t + 1, 1, stride=1, stride_axis=0)[:, :t]
        for c in range(2 * t // rc):
            r0 = c * rc
            half = r0 // t
            h0 = r0 - half * t
            rows = slice(r0, r0 + rc)
            s = _dot_nt(qs_ref[rows, :], k)
            if mode == "fox":
                s = s + bias[half:half + 1]
            elif kind == "far":
                s = s + far_ref[0:1, 0:1]
            else:
                s = s + tile[h0:h0 + rc]
            if kind == "diag":
                rr = lax.broadcasted_iota(I32, (rc, t), 0) + h0
                cc = lax.broadcasted_iota(I32, (rc, t), 1)
                s = jnp.where(rr >= cc, s, NEG_INF)
            m_prev = m_ref[rows, :]
            m_new = jnp.maximum(m_prev, jnp.max(s, axis=1, keepdims=True))
            alpha = jnp.exp2(m_prev - m_new)
            pr = jnp.exp2(s - pltpu.repeat(m_new, t // LANES, 1))
            if mode == "diff":
                l_ref[rows, :] = alpha * l_ref[rows, :] + jnp.sum(pr, axis=1, keepdims=True)
                pv = _dot(pr.astype(BF16), v)
            else:
                pv = _dot(pr.astype(BF16), rhs[half])
            acc_ref[rows, :] = alpha * acc_ref[rows, :] + pv
            m_ref[rows, :] = m_new

    if mode == "diff":
        d = i - j
        near = d * t - (t - 1) < T5_MAX_DISTANCE
        pl.when(d == 0)(functools.partial(step, "diag"))
        pl.when(jnp.logical_and(d > 0, near))(functools.partial(step, "near"))
        pl.when(jnp.logical_not(near))(functools.partial(step, "far"))
    else:
        nq = jmin_ref.shape[0] // (pl.num_programs(0) * pl.num_programs(1))
        live = j >= jmin_ref[(pl.program_id(0) * pl.num_programs(1) + pl.program_id(1)) * nq + i]
        pl.when(j == i)(functools.partial(step, "diag"))
        pl.when(jnp.logical_and(j != i, live))(functools.partial(step, "off"))

    @pl.when(j == i)
    def _():
        acc = acc_ref[...]
        if mode == "diff":
            o = acc * (1.0 / l_ref[...])
            lp = lam_ref[...]
            lam = (jnp.exp(jnp.sum(lp[0:1] * lp[1:2], axis=-1, keepdims=True))
                   - jnp.exp(jnp.sum(lp[2:3] * lp[3:4], axis=-1, keepdims=True)) + lam_init)
            dlt = o[0:t] - lam * o[t:2 * t]
            ms = jnp.mean(dlt * dlt, axis=-1, keepdims=True)
            out = dlt * lax.rsqrt(ms + RMS_EPS) * subln_ref[...] * (1.0 - lam_init)
        else:
            lane = lax.broadcasted_iota(I32, (t, LANES), 1)
            lo = acc[0:t]
            hi = acc[t:2 * t]
            out = jnp.where(lane < 64, lo * (1.0 / lo[:, 64:65]), hi * (1.0 / hi[:, 0:1]))
            out = out * _sigmoid(gate_ref[...].astype(F32))
        o_ref[...] = out.astype(o_ref.dtype)


def _causal_pairs(nq):
    ii, jj = [], []
    for i in range(nq):
        for j in range(i + 1):
            ii.append(i)
            jj.append(j)
    return jnp.asarray(np.array(ii, np.int32)), jnp.asarray(np.array(jj, np.int32))


def _attn_scratch(mode, t):
    base = [pltpu.VMEM((2 * t, LANES), BF16), pltpu.VMEM((2 * t, LANES), F32), pltpu.VMEM((2 * t, LANES), F32)]
    return base + ([pltpu.VMEM((2 * t, LANES), F32)] if mode == "diff" else [])


def _diff_attention(proj, bsz, seq, strips, far, lam_params, subln_w, lam_init, t):
    nq = seq // t
    ii, jj = _causal_pairs(nq)
    nd = strips.shape[1]
    h_ = A_HEADS
    grid_spec = pltpu.PrefetchScalarGridSpec(
        num_scalar_prefetch=2,
        grid=(bsz, h_, int(ii.shape[0])),
        in_specs=[
            pl.BlockSpec((t, LANES), lambda b, h, p, ii, jj: (b * nq + ii[p], h)),
            pl.BlockSpec((t, LANES), lambda b, h, p, ii, jj: (b * nq + jj[p], h_ + h)),
            pl.BlockSpec((t, LANES), lambda b, h, p, ii, jj: (b * nq + jj[p], 2 * h_ + h)),
            pl.BlockSpec((None, None, 1, 2 * t),
                         lambda b, h, p, ii, jj: (h, jnp.minimum(ii[p] - jj[p], nd - 1), 0, 0)),
            pl.BlockSpec((None, 1, LANES), lambda b, h, p, ii, jj: (h, 0, 0)),
            pl.BlockSpec(lam_params.shape, lambda b, h, p, ii, jj: (0, 0)),
            pl.BlockSpec((1, LANES), lambda b, h, p, ii, jj: (0, 0)),
        ],
        out_specs=pl.BlockSpec((t, LANES), lambda b, h, p, ii, jj: (b * nq + ii[p], h)),
        scratch_shapes=_attn_scratch("diff", t),
    )
    return pl.pallas_call(
        functools.partial(_attn_body, "diff", t, lam_init),
        grid_spec=grid_spec,
        out_shape=jax.ShapeDtypeStruct((bsz * seq, h_ * LANES), BF16),
        compiler_params=_cparams(("parallel", "parallel", "arbitrary")),
        name="diff_attn",
    )(ii, jj, proj, proj, proj, strips, far, lam_params, subln_w)


def _fox_first_live_block(cum_t, logit_bound, t):
    bsz, hp, _, seq = cum_t.shape
    nq = seq // t
    c_start = cum_t[..., 0::t]
    c_end = cum_t[..., t - 1::t]
    gap = jnp.max(c_start[..., :, None] - c_end[..., None, :], axis=2) * LOG2E
    dead = gap < -(FOX_SKIP_MARGIN + 2.0 * logit_bound)
    dead = jnp.logical_and(dead, jnp.arange(nq)[None, :] < jnp.arange(nq)[:, None])
    return jnp.sum(jnp.cumprod(dead.astype(I32), axis=-1), axis=-1).reshape(-1)


def _fox_attention(qk, proj, cum_t, logit_bound, bsz, seq, t):
    nq = seq // t
    ii, jj = _causal_pairs(nq)
    hp = C_HEADS // 2
    jmin = _fox_first_live_block(cum_t, logit_bound, t)

    def kblk(b, h, p, ii, jj, jmin):
        return jnp.maximum(jj[p], jmin[(b * hp + h) * nq + ii[p]])

    grid_spec = pltpu.PrefetchScalarGridSpec(
        num_scalar_prefetch=3,
        grid=(bsz, hp, int(ii.shape[0])),
        in_specs=[
            pl.BlockSpec((t, LANES), lambda b, h, p, ii, jj, jm: (b * nq + ii[p], h)),
            pl.BlockSpec((t, LANES), lambda b, h, p, ii, jj, jm: (b * nq + kblk(b, h, p, ii, jj, jm), hp + h)),
            pl.BlockSpec((t, LANES), lambda b, h, p, ii, jj, jm: (b * nq + kblk(b, h, p, ii, jj, jm), 2 * hp + h)),
            pl.BlockSpec((None, None, 2, t), lambda b, h, p, ii, jj, jm: (b, h, 0, kblk(b, h, p, ii, jj, jm))),
            pl.BlockSpec((None, None, 2, t), lambda b, h, p, ii, jj, jm: (b, h, 0, ii[p])),
            pl.BlockSpec((t, LANES), lambda b, h, p, ii, jj, jm: (b * nq + ii[p], 3 * hp + h)),
        ],
        out_specs=pl.BlockSpec((t, LANES), lambda b, h, p, ii, jj, jm: (b * nq + ii[p], h)),
        scratch_shapes=_attn_scratch("fox", t),
    )
    return pl.pallas_call(
        functools.partial(_attn_body, "fox", t, 0.0),
        grid_spec=grid_spec,
        out_shape=jax.ShapeDtypeStruct((bsz * seq, hp * LANES), BF16),
        compiler_params=_cparams(("parallel", "parallel", "arbitrary")),
        name="fox_attn",
    )(ii, jj, jmin, qk, qk, proj, cum_t, cum_t, proj)


def _t5_bias_by_distance(t5_bias, seq):
    n = jnp.arange(seq, dtype=I32)
    max_exact = T5_BUCKETS // 2
    nf = jnp.maximum(n, 1).astype(F32)
    large = max_exact + (jnp.log(nf / max_exact) / math.log(T5_MAX_DISTANCE / max_exact)
                         * (T5_BUCKETS - max_exact)).astype(I32)
    large = jnp.minimum(large, T5_BUCKETS - 1)
    bucket = jnp.where(n < max_exact, n, large)
    return t5_bias.astype(F32).T[:, bucket]


def _t5_strips(t5_bias, seq, t):
    vec = _t5_bias_by_distance(t5_bias, seq) * LOG2E
    nd = 1
    while nd * t - (t - 1) < T5_MAX_DISTANCE and nd < seq // t:
        nd += 1
    d = np.arange(nd)[:, None]
    c = np.arange(2 * t)[None, :]
    rel = d * t + t - 1 - c
    rel = np.clip(rel, 0, seq - 1)
    strips = vec[:, jnp.asarray(rel.astype(np.int32))]
    far = jnp.broadcast_to((t5_bias.astype(F32).T[:, T5_BUCKETS - 1] * LOG2E)[:, None, None],
                           (t5_bias.shape[1], 1, LANES))
    return strips[:, :, None, :], far


def _gdn_body(tb, xq_ref, xk_ref, xv_ref, z_ref, gcol_ref, grow_ref, cw_ref, nw_ref, o_ref, s_ref, carry_ref):
    h = pl.program_id(1)
    c_ = GDN_CHUNK

    @pl.when(pl.program_id(2) == 0)
    def _():
        s_ref[...] = jnp.zeros_like(s_ref)
        carry_ref[...] = jnp.zeros_like(carry_ref)

    cw = cw_ref[...]

    def conv_silu(idx, x_ref):
        x = x_ref[...].astype(F32)
        xe = jnp.concatenate([carry_ref[idx], x], axis=0)
        y = cw[4 * idx + 3:4 * idx + 4] * x
        for tap in range(CONV_WIDTH - 1):
            shift = CONV_WIDTH - 1 - tap
            y = y + cw[4 * idx + tap:4 * idx + tap + 1] * pltpu.roll(xe, shift, 0)[SUBLANES:]
        carry_ref[idx] = x[tb - SUBLANES:]
        return y * _sigmoid(y)

    q = conv_silu(0, xq_ref)
    k = conv_silu(1, xk_ref)
    v = conv_silu(2, xv_ref)
    q = q * lax.rsqrt(jnp.sum(q * q, axis=-1, keepdims=True) + RMS_EPS) * (B_HEAD_DIM ** -0.5)
    k = k * lax.rsqrt(jnp.sum(k * k, axis=-1, keepdims=True) + RMS_EPS)

    gates = gcol_ref[...]
    lane = lax.broadcasted_iota(I32, gates.shape, 1)
    beta = jnp.sum(jnp.where(lane == h, gates, 0.0), axis=1, keepdims=True)
    gc = jnp.sum(jnp.where(lane == B_HEADS + h, gates, 0.0), axis=1, keepdims=True)
    gc_row = grow_ref[1:2, :]
    egc = jnp.exp(gc)
    kb = k * beta
    rhs = jnp.concatenate([v * beta, kb * egc], axis=1)
    qd = q * egc

    ri = lax.broadcasted_iota(I32, (c_, c_), 0)
    ci = lax.broadcasted_iota(I32, (c_, c_), 1)
    state = s_ref[...]
    outs = []
    for c in range(tb // c_):
        sl = slice(c * c_, (c + 1) * c_)
        gcc = gc[sl]
        decay = jnp.exp(jnp.where(ri >= ci, gcc - gc_row[:, sl], NEG_INF))
        kc = k[sl].astype(BF16)
        kk = _dot_nt(kb[sl].astype(BF16), kc)
        qk = _dot_nt(q[sl].astype(BF16), kc) * decay
        x = -(kk * jnp.where(ri > ci, decay, 0.0))
        r = x
        pw = x
        for _ in range(int(math.log2(c_)) - 1):
            pwb = pw.astype(BF16)
            pw = _dot(pwb, pwb)
            r = r + pw + _dot(r.astype(BF16), pw.astype(BF16))
        sol = rhs[sl] + _dot(r.astype(BF16), rhs[sl].astype(BF16))
        u = sol[:, :B_HEAD_DIM]
        w = sol[:, B_HEAD_DIM:]
        g_last = gcc[c_ - 1:c_]
        kd = k[sl] * jnp.exp(g_last - gcc)
        sb = state.astype(BF16)
        v_new = u - _dot(w.astype(BF16), sb)
        vnb = v_new.astype(BF16)
        outs.append(_dot(qd[sl].astype(BF16), sb) + _dot(qk.astype(BF16), vnb))
        state = state * jnp.exp(g_last) + _dot_tn(kd.astype(BF16), vnb)
    s_ref[...] = state
    o = jnp.concatenate(outs, axis=0)
    o = o * lax.rsqrt(jnp.mean(o * o, axis=-1, keepdims=True) + RMS_EPS) * nw_ref[...]
    z = z_ref[...].astype(F32)
    o_ref[...] = (o * (z * _sigmoid(z))).astype(o_ref.dtype)


def _gdn(proj, gates, gates_row, conv_w, norm_w, bsz, seq, tb):
    nt = seq // tb
    h_ = B_HEADS
    col0 = 3 * A_HEADS
    blk = lambda off: pl.BlockSpec((tb, LANES), lambda b, h, s: (b * nt + s, col0 + off * h_ + h))
    return pl.pallas_call(
        functools.partial(_gdn_body, tb),
        grid=(bsz, h_, nt),
        in_specs=[blk(0), blk(1), blk(2), blk(3),
                  pl.BlockSpec((tb, LANES), lambda b, h, s: (b * nt + s, 0)),
                  pl.BlockSpec((None, 2, tb), lambda b, h, s: (b * h_ + h, 0, s)),
                  pl.BlockSpec((None, 3 * CONV_WIDTH, LANES), lambda b, h, s: (h, 0, 0)),
                  pl.BlockSpec((1, LANES), lambda b, h, s: (0, 0))],
        out_specs=pl.BlockSpec((tb, LANES), lambda b, h, s: (b * nt + s, h)),
        out_shape=jax.ShapeDtypeStruct((bsz * seq, h_ * LANES), BF16),
        scratch_shapes=[pltpu.VMEM((B_HEAD_DIM, B_HEAD_DIM), F32),
                        pltpu.VMEM((3, SUBLANES, LANES), F32)],
        compiler_params=_cparams(("parallel", "parallel", "arbitrary")),
        name="gdn",
    )(proj, proj, proj, proj, gates, gates_row, conv_w, norm_w)


ROW_TILES = D_MODEL // LANES


def _store_tile_rows(ref, y):
    t = y.shape[0]
    for c in range(ROW_TILES):
        ref[pl.ds(c, t, stride=ROW_TILES), :] = y[:, c * LANES:(c + 1) * LANES]


def _load_tile_rows(ref, t, lead=()):
    return jnp.concatenate([ref[lead + (pl.ds(c, t, stride=ROW_TILES), slice(None))] for c in range(ROW_TILES)],
                           axis=1)


def _outproj_ln_body(a_ref, b_ref, w_ref, x_ref, g_ref, bb_ref, o_ref, o3_ref, wb_ref):
    @pl.when(pl.program_id(0) == 0)
    def _():
        wb_ref[...] = w_ref[...].astype(BF16)

    half = a_ref.shape[1]
    hmix = _dot(a_ref[...], wb_ref[0:half, :]) + _dot(b_ref[...], wb_ref[half:, :])
    out = _layer_norm(DEEPNORM_ALPHA * x_ref[...] + hmix, g_ref[...], bb_ref[...])
    o_ref[...] = out
    _store_tile_rows(o3_ref, out)


def _outproj_ln(a, a_blk, b, b_blk, w, x, g, bb, tm=512):
    n, d = x.shape
    tm = min(tm, n)
    half = d // 2
    row = pl.BlockSpec((1, d), lambda i: (0, 0))
    return pl.pallas_call(
        _outproj_ln_body,
        grid=(n // tm,),
        in_specs=[pl.BlockSpec((tm, half), lambda i: (i, a_blk)),
                  pl.BlockSpec((tm, half), lambda i: (i, b_blk)),
                  pl.BlockSpec((d, d), lambda i: (0, 0)),
                  pl.BlockSpec((tm, d), lambda i: (i, 0)), row, row],
        out_specs=[pl.BlockSpec((tm, d), lambda i: (i, 0)),
                   pl.BlockSpec((tm * ROW_TILES, LANES), lambda i: (i, 0))],
        out_shape=[jax.ShapeDtypeStruct((n, d), F32), jax.ShapeDtypeStruct((n * ROW_TILES, LANES), F32)],
        scratch_shapes=[pltpu.VMEM((d, d), BF16)],
        compiler_params=_cparams(("arbitrary",)),
        name="outproj_ln",
    )(a, b, w, x, g, bb)


def _router_body(tm, x_ref, w_ref, b_ref, idx_ref, gate_ref, rank_ref, cnt_ref, carry_ref):
    @pl.when(pl.program_id(0) == 0)
    def _():
        carry_ref[...] = jnp.zeros_like(carry_ref)

    logits = _dot(x_ref[...], w_ref[...], precision=HIGHEST) + b_ref[...]
    lg = jnp.transpose(logits)[0:N_EXPERTS, :]
    e_iota = lax.broadcasted_iota(I32, (N_EXPERTS, tm), 0).astype(F32)
    vals, idxs, hots = [], [], []
    for _ in range(TOP_K):
        m = jnp.max(lg, axis=0, keepdims=True)
        idx = jnp.min(jnp.where(lg == m, e_iota, float(N_EXPERTS)), axis=0, keepdims=True)
        hot = e_iota == idx
        lg = jnp.where(hot, NEG_INF, lg)
        vals.append(m)
        idxs.append(idx)
        hots.append(hot)
    es = [jnp.exp(v - vals[0]) for v in vals]
    den = es[0] + es[1] + es[2] + es[3]
    sel = jnp.zeros((N_EXPERTS, tm), F32)
    for hot in hots:
        sel = sel + jnp.where(hot, 1.0, 0.0)
    before = jnp.where(lax.broadcasted_iota(I32, (tm, tm), 0) < lax.broadcasted_iota(I32, (tm, tm), 1),
                       1.0, 0.0).astype(BF16)
    cum = _dot(sel.astype(BF16), before) + carry_ref[:, 0:1]
    ranks = [jnp.sum(jnp.where(hot, cum, 0.0), axis=0, keepdims=True) for hot in hots]
    total = carry_ref[...] + jnp.sum(sel, axis=1, keepdims=True)
    carry_ref[...] = total
    idx_ref[...] = jnp.concatenate(idxs, axis=0).astype(I32)
    gate_ref[...] = jnp.concatenate([e / den for e in es], axis=0)
    rank_ref[...] = jnp.concatenate(ranks, axis=0).astype(I32)
    cnt_ref[...] = total


def _router(x, w_pad, b_pad, tm=512):
    n, d = x.shape
    tm = min(tm, n)
    out4 = lambda dt: jax.ShapeDtypeStruct((TOP_K, n), dt)
    blk4 = pl.BlockSpec((TOP_K, tm), lambda i: (0, i))
    return pl.pallas_call(
        functools.partial(_router_body, tm),
        grid=(n // tm,),
        in_specs=[pl.BlockSpec((tm, d), lambda i: (i, 0)),
                  pl.BlockSpec((d, LANES), lambda i: (0, 0)),
                  pl.BlockSpec((1, LANES), lambda i: (0, 0))],
        out_specs=[blk4, blk4, blk4, pl.BlockSpec((N_EXPERTS, LANES), lambda i: (0, 0))],
        out_shape=[out4(I32), out4(F32), out4(I32), jax.ShapeDtypeStruct((N_EXPERTS, LANES), F32)],
        scratch_shapes=[pltpu.VMEM((N_EXPERTS, LANES), F32)],
        compiler_params=_cparams(("arbitrary",)),
        name="moe_router",
    )(x, w_pad, b_pad)


def _row_slab(row):
    return pl.ds(pl.multiple_of(row * ROW_TILES, ROW_TILES), ROW_TILES)


def _row_dma_loops(n_rows, make_copy):
    def issue(r, carry):
        for k in range(TOP_K):
            make_copy(k, r).start()
        return carry

    def drain(r, carry):
        for k in range(TOP_K):
            make_copy(k, r).wait()
        return carry

    lax.fori_loop(0, n_rows, issue, 0, unroll=4)
    lax.fori_loop(0, n_rows, drain, 0, unroll=8)


def _dispatch_body(td, dest_ref, x3_ref, xg_in, xg_out, sem):
    del xg_in
    _row_dma_loops(td, lambda k, r: pltpu.make_async_copy(
        x3_ref.at[_row_slab(r)], xg_out.at[_row_slab(dest_ref[k, r])], sem))


def _smem_rows(tile):
    return pl.BlockSpec((TOP_K, tile), lambda i: (0, i), memory_space=pltpu.SMEM)


def _dispatch(x3, dest, n_rows, td=512):
    n = x3.shape[0] // ROW_TILES
    td = min(td, n)
    xg0 = jnp.zeros((n_rows * ROW_TILES, LANES), x3.dtype)
    return pl.pallas_call(
        functools.partial(_dispatch_body, td),
        grid=(n // td,),
        in_specs=[_smem_rows(td),
                  pl.BlockSpec((td * ROW_TILES, LANES), lambda i: (i, 0)),
                  pl.BlockSpec(memory_space=pl.ANY)],
        out_specs=pl.BlockSpec(memory_space=pl.ANY),
        out_shape=jax.ShapeDtypeStruct(xg0.shape, x3.dtype),
        scratch_shapes=[pltpu.SemaphoreType.DMA(())],
        input_output_aliases={2: 0},
        compiler_params=_cparams(("arbitrary",)),
        name="moe_dispatch",
    )(dest, x3, xg0)


def _expert_body(be_ref, nu_ref, x_ref, wgu_ref, bgu_ref, wd_ref, bd_ref, o_ref, wgu_b, wd_b):
    i = pl.program_id(0)
    prev = be_ref[jnp.maximum(i - 1, 0)]
    fresh = jnp.logical_or(i == 0, be_ref[i] != prev)

    @pl.when(jnp.logical_and(i < nu_ref[0], fresh))
    def _():
        wgu_b[...] = wgu_ref[...].astype(BF16)
        wd_b[...] = wd_ref[...].astype(BF16)

    @pl.when(i < nu_ref[0])
    def _():
        x = _load_tile_rows(x_ref, MOE_ROWS).astype(BF16)
        hcat = _dot(x, wgu_b[...]) + bgu_ref[...]
        g = jnp.minimum(hcat[:, :D_FF], SWIGLU_LIMIT)
        u = jnp.clip(hcat[:, D_FF:], -SWIGLU_LIMIT, SWIGLU_LIMIT)
        act = g * _sigmoid(SWIGLU_ALPHA * g) * (u + 1.0)
        _store_tile_rows(o_ref, _dot(act.astype(BF16), wd_b[...]) + bd_ref[...])

    @pl.when(i >= nu_ref[0])
    def _():
        o_ref[...] = jnp.zeros_like(o_ref)


def _experts(xg, layer, block_expert, n_used, w_gu, b_gu, w_down, b_down):
    d = D_MODEL
    nb = xg.shape[0] // (MOE_ROWS * ROW_TILES)
    blk = pl.BlockSpec((MOE_ROWS * ROW_TILES, LANES), lambda i, be, nu: (i, 0))
    grid_spec = pltpu.PrefetchScalarGridSpec(
        num_scalar_prefetch=2,
        grid=(nb,),
        in_specs=[
            blk,
            pl.BlockSpec((None, None, d, 2 * D_FF), lambda i, be, nu: (layer, be[i], 0, 0)),
            pl.BlockSpec((None, None, 1, 2 * D_FF), lambda i, be, nu: (layer, be[i], 0, 0)),
            pl.BlockSpec((None, None, D_FF, d), lambda i, be, nu: (layer, be[i], 0, 0)),
            pl.BlockSpec((None, None, 1, d), lambda i, be, nu: (layer, be[i], 0, 0)),
        ],
        out_specs=blk,
        scratch_shapes=[pltpu.VMEM((d, 2 * D_FF), BF16), pltpu.VMEM((D_FF, d), BF16)],
    )
    return pl.pallas_call(
        _expert_body,
        grid_spec=grid_spec,
        out_shape=jax.ShapeDtypeStruct(xg.shape, F32),
        compiler_params=_cparams(("arbitrary",)),
        name="moe_experts",
    )(block_expert, n_used, xg, w_gu, b_gu[:, :, None, :], w_down, b_down[:, :, None, :])


def _combine_ln_body(tc, dest_ref, yg_hbm, gate_ref, x_ref, g_ref, b_ref, o_ref, ybuf, sem):
    _row_dma_loops(tc, lambda k, r: pltpu.make_async_copy(
        yg_hbm.at[_row_slab(dest_ref[k, r])], ybuf.at[k, _row_slab(r)], sem))
    gate = gate_ref[...]
    hmoe = gate[:, 0:1] * _load_tile_rows(ybuf, tc, (0,))
    for k in range(1, TOP_K):
        hmoe = hmoe + gate[:, k:k + 1] * _load_tile_rows(ybuf, tc, (k,))
    o_ref[...] = _layer_norm(DEEPNORM_ALPHA * x_ref[...] + hmoe, g_ref[...], b_ref[...])


def _combine_ln(yg, dest, gate_t, x, g, b, tc=256):
    n, d = x.shape
    tc = min(tc, n)
    row = pl.BlockSpec((1, d), lambda i: (0, 0))
    return pl.pallas_call(
        functools.partial(_combine_ln_body, tc),
        grid=(n // tc,),
        in_specs=[_smem_rows(tc),
                  pl.BlockSpec(memory_space=pl.ANY),
                  pl.BlockSpec((tc, TOP_K), lambda i: (i, 0)),
                  pl.BlockSpec((tc, d), lambda i: (i, 0)), row, row],
        out_specs=pl.BlockSpec((tc, d), lambda i: (i, 0)),
        out_shape=jax.ShapeDtypeStruct((n, d), F32),
        scratch_shapes=[pltpu.VMEM((TOP_K, tc * ROW_TILES, LANES), F32), pltpu.SemaphoreType.DMA(())],
        compiler_params=_cparams(("arbitrary",)),
        name="moe_combine_ln",
    )(dest, yg, gate_t, x, g, b)


def _moe_ln(x, x3, layer, router_w, router_b, w_gu, b_gu, w_down, b_down, ln_g, ln_b):
    n, d = x.shape
    idx, gate, rank, cnt = _router(x, _pad_lanes(router_w), _pad_lanes(router_b[None, :]))
    counts = cnt[:, 0].astype(I32)
    padded = (counts + MOE_ROWS - 1) // MOE_ROWS * MOE_ROWS
    pad_end = jnp.cumsum(padded)
    pad_start = pad_end - padded
    hot = idx[:, :, None] == jnp.arange(N_EXPERTS, dtype=I32)[None, None, :]
    dest = jnp.sum(jnp.where(hot, pad_start[None, None, :], 0), axis=-1) + rank
    n_blocks = -(-(n * TOP_K) // MOE_ROWS) + N_EXPERTS
    block_row0 = jnp.arange(n_blocks, dtype=I32) * MOE_ROWS
    block_expert = jnp.minimum(jnp.sum((pad_end[None, :] <= block_row0[:, None]).astype(I32), axis=1),
                               N_EXPERTS - 1)
    n_used = (pad_end[-1:] // MOE_ROWS).astype(I32)
    xg = _dispatch(x3, dest, n_blocks * MOE_ROWS)
    yg = _experts(xg, layer, block_expert, n_used, w_gu, b_gu, w_down, b_down)
    return _combine_ln(yg, dest, gate.T, x, ln_g[None, :], ln_b[None, :])


def _even_mixer_ln(x, bsz, seq, w_in, w_out, lam_params, subln_w, conv_w, a_log, dt_bias, gdn_norm_w,
                   t5_bias, lam_init, ln_g, ln_b):
    main = 3 * A_HEADS * LANES + 4 * B_HEADS * B_HEAD_DIM
    n_q = A_HEADS * 2 * A_HEAD_DIM
    col_scale = jnp.concatenate([jnp.full((1, n_q), A_HEAD_DIM ** -0.5 * LOG2E, F32),
                                 jnp.ones((1, main - n_q), F32)], axis=1)
    proj = _proj(x, w_in[:, :main], col_scale)
    zeros4 = jnp.zeros((B_HEADS,), F32)
    p0 = _pad_lanes(jnp.concatenate([zeros4, a_log.astype(F32)])[None, :])
    p1 = _pad_lanes(jnp.concatenate([zeros4, dt_bias.astype(F32)])[None, :])
    gates = _gates("even", x, _pad_lanes(w_in[:, main:]), p0, p1, seq)
    t = min(ATTN_TILE, seq)
    strips, far = _t5_strips(t5_bias, seq, t)
    ao = _diff_attention(proj, bsz, seq, strips, far, lam_params.astype(F32), subln_w[None, :].astype(F32),
                         lam_init, t)
    g8 = gates[:, :2 * B_HEADS].reshape(bsz, seq, 2, B_HEADS)
    gates_row = g8.transpose(0, 3, 2, 1).reshape(bsz * B_HEADS, 2, seq)
    cw = conv_w.astype(F32).reshape(CONV_WIDTH, 3, B_HEADS, B_HEAD_DIM).transpose(2, 1, 0, 3)
    cw = cw.reshape(B_HEADS, 3 * CONV_WIDTH, B_HEAD_DIM)
    bo = _gdn(proj, gates, gates_row, cw, gdn_norm_w[None, :].astype(F32), bsz, seq, min(GDN_TILE, seq))
    return _outproj_ln(ao, 0, bo, 0, w_out, x, ln_g[None, :], ln_b[None, :])


def _odd_mixer_ln(x, bsz, seq, w_in, w_out, qk_norm_w, forget_b, ln_g, ln_b):
    main = 4 * C_HEADS * C_HEAD_DIM
    proj = _proj(x, w_in[:, :main], jnp.ones((1, main), F32))
    fb =_pad_lanes(forget_b.astype(F32)[None, :])
    cum = _gates("odd", x, _pad_lanes(w_in[:, main:]), fb, fb, seq)
    cum_t = cum[:, :C_HEADS].reshape(bsz, seq, C_HEADS // 2, 2).transpose(0, 2, 3, 1)
    scale = jnp.asarray([C_HEAD_DIM ** -0.5 * LOG2E, 1.0], F32)[:, None]
    wqk = qk_norm_w.astype(F32) * scale
    w2 = jnp.tile(wqk, (1, 2))[:, None, :]
    qk = _qknorm(proj, w2)
    logit_bound = 1.02 * C_HEAD_DIM * jnp.max(jnp.abs(wqk[0])) * jnp.max(jnp.abs(wqk[1]))
    o = _fox_attention(qk, proj, cum_t, logit_bound, bsz, seq, min(ATTN_TILE, seq))
    return _outproj_ln(o, 0, o, 1, w_out, x, ln_g[None, :], ln_b[None, :])


def kernel(x, t5_bias, even_w_in, even_w_out, diff_lambda, diff_subln_w, gdn_conv_w, gdn_a_log, gdn_dt_bias, gdn_norm_w, odd_w_in, odd_w_out, fox_qk_norm_w, fox_forget_b, router_w, router_b, moe_w_gate_up, moe_b_gate_up, moe_w_down, moe_b_down, ln_mix_g, ln_mix_b, ln_ffn_g, ln_ffn_b):
    bsz, seq, d = x.shape
    xf = x.reshape(bsz * seq, d)
    for layer in range(DEPTH):
        i = layer // 2
        if layer % 2 == 0:
            lam_init = 0.8 - 0.6 * math.exp(-0.3 * layer)
            xf, x3 = _even_mixer_ln(xf, bsz, seq, even_w_in[i], even_w_out[i], diff_lambda[i], diff_subln_w[i],
                                    gdn_conv_w[i], gdn_a_log[i], gdn_dt_bias[i], gdn_norm_w[i], t5_bias, lam_init,
                                    ln_mix_g[layer], ln_mix_b[layer])
        else:
            xf, x3 = _odd_mixer_ln(xf, bsz, seq, odd_w_in[i], odd_w_out[i], fox_qk_norm_w[i], fox_forget_b[i],
                                   ln_mix_g[layer], ln_mix_b[layer])
        xf = _moe_ln(xf, x3, layer, router_w[layer], router_b[layer], moe_w_gate_up, moe_b_gate_up,
                     moe_w_down, moe_b_down, ln_ffn_g[layer], ln_ffn_b[layer])
    return xf.reshape(bsz, seq, d)
```

```python
import functools
import math

import numpy as np
import jax
import jax.numpy as jnp
from jax import lax
from jax.experimental import pallas as pl
from jax.experimental.pallas import tpu as pltpu

F32 = jnp.float32
BF16 = jnp.bfloat16
I32 = jnp.int32
HIGHEST = lax.Precision.HIGHEST

D_MODEL = 1024
DEPTH = 4
A_HEADS = 4
A_HEAD_DIM = 64
B_HEADS = 4
B_HEAD_DIM = 128
CONV_WIDTH = 4
C_HEADS = 16
C_HEAD_DIM = 64
T5_BUCKETS = 32
T5_MAX_DISTANCE = 2048
N_EXPERTS = 32
TOP_K = 4
D_FF = D_MODEL
SWIGLU_LIMIT = 7.0
SWIGLU_ALPHA = 1.702
DEEPNORM_ALPHA = (2 * DEPTH) ** 0.25
LN_EPS = 1e-5
RMS_EPS = 1e-6

LANES = 128
SUBLANES = 8
VMEM_LIMIT = 56 * 1024 * 1024

ATTN_TILE = 1024
ATTN_ROWS = 1024
GDN_CHUNK = 128
GDN_TILE = 512
GDN_HEADS_PER_STEP = 4
MOE_ROWS = 256
FOX_SKIP_MARGIN = 170.0
NEG_INF = float("-inf")
LOG2E = math.log2(math.e)


def _cparams(sem, vmem=VMEM_LIMIT):
    return pltpu.CompilerParams(dimension_semantics=sem, vmem_limit_bytes=vmem)


def _dot(a, b, **kw):
    return jnp.dot(a, b, preferred_element_type=F32, **kw)


def _dot_nt(a, b, **kw):
    return lax.dot_general(a, b, (((1,), (1,)), ((), ())), preferred_element_type=F32, **kw)


def _dot_tn(a, b, **kw):
    return lax.dot_general(a, b, (((0,), (0,)), ((), ())), preferred_element_type=F32, **kw)


def _sigmoid(x):
    return 1.0 / (1.0 + jnp.exp(-x))


def _softplus(x):
    return jnp.maximum(x, 0.0) + jnp.log(1.0 + jnp.exp(-jnp.abs(x)))


def _layer_norm(xf, g, b):
    mu = jnp.mean(xf, axis=-1, keepdims=True)
    xc = xf - mu
    var = jnp.mean(xc * xc, axis=-1, keepdims=True)
    return xc * lax.rsqrt(var + LN_EPS) * g + b


def _proj_body(x_ref, w_ref, cs_ref, o_ref, xb_ref):
    @pl.when(pl.program_id(1) == 0)
    def _():
        xb_ref[...] = x_ref[...].astype(BF16)

    o_ref[...] = (_dot(xb_ref[...], w_ref[...].astype(BF16)) * cs_ref[...]).astype(o_ref.dtype)


def _proj(x, w, col_scale, tm=1024, tn=512):
    n, k = x.shape
    m = w.shape[1]
    tm = min(tm, n)
    return pl.pallas_call(
        _proj_body,
        grid=(n // tm, m // tn),
        in_specs=[pl.BlockSpec((tm, k), lambda i, j: (i, 0)),
                  pl.BlockSpec((k, tn), lambda i, j: (0, j)),
                  pl.BlockSpec((1, tn), lambda i, j: (0, j))],
        out_specs=pl.BlockSpec((tm, tn), lambda i, j: (i, j)),
        out_shape=jax.ShapeDtypeStruct((n, m), BF16),
        scratch_shapes=[pltpu.VMEM((tm, k), BF16)],
        compiler_params=_cparams(("arbitrary", "arbitrary")),
        name="proj",
    )(x, w, col_scale)


def _gates_body(mode, tm, steps_per_seq, x_ref, w_ref, p0_ref, p1_ref, o_ref, carry_ref):
    z = _dot(x_ref[...], w_ref[...], precision=HIGHEST)
    ri = lax.broadcasted_iota(I32, (tm, tm), 0)
    ci = lax.broadcasted_iota(I32, (tm, tm), 1)
    if mode == "even":
        lane = lax.broadcasted_iota(I32, (tm, LANES), 1)
        beta = _sigmoid(z)
        g = -jnp.exp(p0_ref[...]) * _softplus(z + p1_ref[...])
        shift = int(math.log2(GDN_CHUNK))
        same_chunk = jnp.right_shift(ri, shift) == jnp.right_shift(ci, shift)
        tri = jnp.where(jnp.logical_and(same_chunk, ci <= ri), 1.0, 0.0)
        gc = _dot(tri, g, precision=HIGHEST)
        o_ref[...] = jnp.where(lane < B_HEADS, beta, gc)
    else:
        @pl.when(pl.program_id(0) % steps_per_seq == 0)
        def _():
            carry_ref[...] = jnp.zeros_like(carry_ref)

        logf = -_softplus(-(z + p0_ref[...]))
        tri = jnp.where(ci <= ri, 1.0, 0.0)
        cum = _dot(tri, logf, precision=HIGHEST) + carry_ref[0:1, :]
        o_ref[...] = cum
        carry_ref[...] = jnp.broadcast_to(cum[tm - 1:tm, :], carry_ref.shape)


def _gates(mode, x, w_small, p0, p1, seq_len, tm=512):
    n, k = x.shape
    tm = min(tm, seq_len)
    row = pl.BlockSpec((1, LANES), lambda i: (0, 0))
    return pl.pallas_call(
        functools.partial(_gates_body, mode, tm, seq_len // tm),
        grid=(n // tm,),
        in_specs=[pl.BlockSpec((tm, k), lambda i: (i, 0)),
                  pl.BlockSpec((k, LANES), lambda i: (0, 0)), row, row],
        out_specs=pl.BlockSpec((tm, LANES), lambda i: (i, 0)),
        out_shape=jax.ShapeDtypeStruct((n, LANES), F32),
        scratch_shapes=[pltpu.VMEM((SUBLANES, LANES), F32)],
        compiler_params=_cparams(("arbitrary",)),
        name="gates_" + mode,
    )(x, w_small, p0, p1)


def _pad_lanes(a):
    return jnp.pad(a, [(0, 0)] * (a.ndim - 1) + [(0, LANES - a.shape[-1])])


def _qknorm_body(x_ref, w_ref, o_ref):
    x = x_ref[...].astype(F32)
    lane = lax.broadcasted_iota(I32, x.shape, 1)
    lo = lane < C_HEAD_DIM
    ss = x * x
    s_lo = jnp.sum(jnp.where(lo, ss, 0.0), axis=-1, keepdims=True)
    s_hi = jnp.sum(jnp.where(lo, 0.0, ss), axis=-1, keepdims=True)
    inv = jnp.where(lo, lax.rsqrt(s_lo / C_HEAD_DIM + RMS_EPS), lax.rsqrt(s_hi / C_HEAD_DIM + RMS_EPS))
    o_ref[...] = (x * inv * w_ref[...]).astype(o_ref.dtype)


def _qknorm(proj, w2, tm=1024):
    n = proj.shape[0]
    tm = min(tm, n)
    nblk = 2 * C_HEADS * C_HEAD_DIM // LANES
    return pl.pallas_call(
        _qknorm_body,
        grid=(n // tm, nblk),
        in_specs=[pl.BlockSpec((tm, LANES), lambda i, c: (i, c)),
                  pl.BlockSpec((None, 1, LANES), lambda i, c: (c // (nblk // 2), 0, 0))],
        out_specs=pl.BlockSpec((tm, LANES), lambda i, c: (i, c)),
        out_shape=jax.ShapeDtypeStruct((n, nblk * LANES), BF16),
        compiler_params=_cparams(("parallel", "parallel")),
        name="fox_qknorm",
    )(proj, w2)


def _attn_body(mode, t, lam_init, ii_ref, jj_ref, *rest):
    if mode == "diff":
        q_ref, k_ref, v_ref, strip_ref, far_ref, lam_ref, subln_ref, o_ref, qs_ref, m_ref, acc_ref, l_ref = rest
    else:
        jmin_ref, q_ref, k_ref, v_ref, ck_ref, c0_ref, gate_ref, o_ref, qs_ref, m_ref, acc_ref = rest
    p = pl.program_id(2)
    i = ii_ref[p]
    j = jj_ref[p]
    rc = min(ATTN_ROWS, t)

    @pl.when(j == 0)
    def _():
        q = q_ref[...]
        lane = lax.broadcasted_iota(I32, q.shape, 1)
        zero = jnp.zeros_like(q)
        qs_ref[0:t, :] = jnp.where(lane < 64, q, zero)
        qs_ref[t:2 * t, :] = jnp.where(lane < 64, zero, q)
        m_ref[...] = jnp.full(m_ref.shape, NEG_INF, F32)
        acc_ref[...] = jnp.zeros_like(acc_ref)
        if mode == "diff":
            l_ref[...] = jnp.zeros_like(l_ref)

    def step(kind):
        k = k_ref[...]
        v = v_ref[...]
        if mode == "fox":
            lane = lax.broadcasted_iota(I32, v.shape, 1)
            one = jnp.ones_like(v)
            rhs = (jnp.where(lane < 64, v, one), jnp.where(lane < 64, one, v))
            bias = (c0_ref[:, 0:1] - ck_ref[...]) * LOG2E
        elif kind != "far":
            x = jnp.broadcast_to(strip_ref[...], (t, 2 * t))
            tile = pltpu.roll(x, t + 1, 1, stride=1, stride_axis=0)[:, :t]
        for c in range(2 * t // rc):
            r0 = c * rc
            half = r0 // t
            h0 = r0 - half * t
            rows = slice(r0, r0 + rc)
            s = _dot_nt(qs_ref[rows, :], k)
            if mode == "fox":
                s = s + bias[half:half + 1]
            elif kind == "far":
                s = s + far_ref[0:1, 0:1]
            else:
                s = s + tile[h0:h0 + rc]
            if kind == "diag":
                rr = lax.broadcasted_iota(I32, (rc, t), 0) + h0
                cc = lax.broadcasted_iota(I32, (rc, t), 1)
                s = jnp.where(rr >= cc, s, NEG_INF)
            m_prev = m_ref[rows, :]
            m_new = jnp.maximum(m_prev, jnp.max(s, axis=1, keepdims=True))
            alpha = jnp.exp2(m_prev - m_new)
            pr = jnp.exp2(s - pltpu.repeat(m_new, t // LANES, 1))
            if mode == "diff":
                l_ref[rows, :] = alpha * l_ref[rows, :] + jnp.sum(pr, axis=1, keepdims=True)
                pv = _dot(pr.astype(BF16), v)
            else:
                pv = _dot(pr.astype(BF16), rhs[half])
            acc_ref[rows, :] = alpha * acc_ref[rows, :] + pv
            m_ref[rows, :] = m_new

    if mode == "diff":
        d = i - j
        near = d * t - (t - 1) < T5_MAX_DISTANCE
        pl.when(d == 0)(functools.partial(step, "diag"))
        pl.when(jnp.logical_and(d > 0, near))(functools.partial(step, "near"))
        pl.when(jnp.logical_not(near))(functools.partial(step, "far"))
    else:
        nq = jmin_ref.shape[0] // (pl.num_programs(0) * pl.num_programs(1))
        live = j >= jmin_ref[(pl.program_id(0) * pl.num_programs(1) + pl.program_id(1)) * nq + i]
        pl.when(j == i)(functools.partial(step, "diag"))
        pl.when(jnp.logical_and(j != i, live))(functools.partial(step, "off"))

    @pl.when(j == i)
    def _():
        acc = acc_ref[...]
        if mode == "diff":
            o = acc * (1.0 / l_ref[...])
            lp = lam_ref[...]
            lam = (jnp.exp(jnp.sum(lp[0:1] * lp[1:2], axis=-1, keepdims=True))
                   - jnp.exp(jnp.sum(lp[2:3] * lp[3:4], axis=-1, keepdims=True)) + lam_init)
            dlt = o[0:t] - lam * o[t:2 * t]
            ms = jnp.mean(dlt * dlt, axis=-1, keepdims=True)
            out = dlt * lax.rsqrt(ms + RMS_EPS) * subln_ref[...] * (1.0 - lam_init)
        else:
            lane = lax.broadcasted_iota(I32, (t, LANES), 1)
            lo = acc[0:t]
            hi = acc[t:2 * t]
            out = jnp.where(lane < 64, lo * (1.0 / lo[:, 64:65]), hi * (1.0 / hi[:, 0:1]))
            out = out * _sigmoid(gate_ref[...].astype(F32))
        o_ref[...] = out.astype(o_ref.dtype)


def _causal_pairs(nq):
    ii, jj = [], []
    for i in range(nq):
        for j in range(i + 1):
            ii.append(i)
            jj.append(j)
    return jnp.asarray(np.array(ii, np.int32)), jnp.asarray(np.array(jj, np.int32))


def _attn_scratch(mode, t):
    base = [pltpu.VMEM((2 * t, LANES), BF16), pltpu.VMEM((2 * t, LANES), F32), pltpu.VMEM((2 * t, LANES), F32)]
    return base + ([pltpu.VMEM((2 * t, LANES), F32)] if mode == "diff" else [])


def _diff_attention(proj, bsz, seq, strips, far, lam_params, subln_w, lam_init, t):
    nq = seq // t
    ii, jj = _causal_pairs(nq)
    nd = strips.shape[1]
    h_ = A_HEADS
    grid_spec = pltpu.PrefetchScalarGridSpec(
        num_scalar_prefetch=2,
        grid=(bsz, h_, int(ii.shape[0])),
        in_specs=[
            pl.BlockSpec((t, LANES), lambda b, h, p, ii, jj: (b * nq + ii[p], h)),
            pl.BlockSpec((t, LANES), lambda b, h, p, ii, jj: (b * nq + jj[p], h_ + h)),
            pl.BlockSpec((t, LANES), lambda b, h, p, ii, jj: (b * nq + jj[p], 2 * h_ + h)),
            pl.BlockSpec((None, None, 1, 2 * t),
                         lambda b, h, p, ii, jj: (h, jnp.minimum(ii[p] - jj[p], nd - 1), 0, 0)),
            pl.BlockSpec((None, 1, LANES), lambda b, h, p, ii, jj: (h, 0, 0)),
            pl.BlockSpec(lam_params.shape, lambda b, h, p, ii, jj: (0, 0)),
            pl.BlockSpec((1, LANES), lambda b, h, p, ii, jj: (0, 0)),
        ],
        out_specs=pl.BlockSpec((t, LANES), lambda b, h, p, ii, jj: (b * nq + ii[p], h)),
        scratch_shapes=_attn_scratch("diff", t),
    )
    return pl.pallas_call(
        functools.partial(_attn_body, "diff", t, lam_init),
        grid_spec=grid_spec,
        out_shape=jax.ShapeDtypeStruct((bsz * seq, h_ * LANES), BF16),
        compiler_params=_cparams(("parallel", "parallel", "arbitrary")),
        name="diff_attn",
    )(ii, jj, proj, proj, proj, strips, far, lam_params, subln_w)


def _fox_first_live_block(cum_t, logit_bound, t):
    bsz, hp, _, seq = cum_t.shape
    nq = seq // t
    c_start = cum_t[..., 0::t]
    c_end = cum_t[..., t - 1::t]
    gap = jnp.max(c_start[..., :, None] - c_end[..., None, :], axis=2) * LOG2E
    dead = gap < -(FOX_SKIP_MARGIN + 2.0 * logit_bound)
    dead = jnp.logical_and(dead, jnp.arange(nq)[None, :] < jnp.arange(nq)[:, None])
    return jnp.sum(jnp.cumprod(dead.astype(I32), axis=-1), axis=-1).reshape(-1)


def _fox_attention(qk, proj, cum_t, logit_bound, bsz, seq, t):
    nq = seq // t
    ii, jj = _causal_pairs(nq)
    hp = C_HEADS // 2
    jmin = _fox_first_live_block(cum_t, logit_bound, t)

    def kblk(b, h, p, ii, jj, jmin):
        return jnp.maximum(jj[p], jmin[(b * hp + h) * nq + ii[p]])

    grid_spec = pltpu.PrefetchScalarGridSpec(
        num_scalar_prefetch=3,
        grid=(bsz, hp, int(ii.shape[0])),
        in_specs=[
            pl.BlockSpec((t, LANES), lambda b, h, p, ii, jj, jm: (b * nq + ii[p], h)),
            pl.BlockSpec((t, LANES), lambda b, h, p, ii, jj, jm: (b * nq + kblk(b, h, p, ii, jj, jm), hp + h)),
            pl.BlockSpec((t, LANES), lambda b, h, p, ii, jj, jm: (b * nq + kblk(b, h, p, ii, jj, jm), 2 * hp + h)),
            pl.BlockSpec((None, None, 2, t), lambda b, h, p, ii, jj, jm: (b, h, 0, kblk(b, h, p, ii, jj, jm))),
            pl.BlockSpec((None, None, 2, t), lambda b, h, p, ii, jj, jm: (b, h, 0, ii[p])),
            pl.BlockSpec((t, LANES), lambda b, h, p, ii, jj, jm: (b * nq + ii[p], 3 * hp + h)),
        ],
        out_specs=pl.BlockSpec((t, LANES), lambda b, h, p, ii, jj, jm: (b * nq + ii[p], h)),
        scratch_shapes=_attn_scratch("fox", t),
    )
    return pl.pallas_call(
        functools.partial(_attn_body, "fox", t, 0.0),
        grid_spec=grid_spec,
        out_shape=jax.ShapeDtypeStruct((bsz * seq, hp * LANES), BF16),
        compiler_params=_cparams(("parallel", "parallel", "arbitrary")),
        name="fox_attn",
    )(ii, jj, jmin, qk, qk, proj, cum_t, cum_t, proj)


def _t5_bias_by_distance(t5_bias, seq):
    n = jnp.arange(seq, dtype=I32)
    max_exact = T5_BUCKETS // 2
    nf = jnp.maximum(n, 1).astype(F32)
    large = max_exact + (jnp.log(nf / max_exact) / math.log(T5_MAX_DISTANCE / max_exact)
                         * (T5_BUCKETS - max_exact)).astype(I32)
    large = jnp.minimum(large, T5_BUCKETS - 1)
    bucket = jnp.where(n < max_exact, n, large)
    return t5_bias.astype(F32).T[:, bucket]


def _t5_strips(t5_bias, seq, t):
    vec = _t5_bias_by_distance(t5_bias, seq) * LOG2E
    nd = 1
    while nd * t - (t - 1) < T5_MAX_DISTANCE and nd < seq // t:
        nd += 1
    d = np.arange(nd)[:, None]
    c = np.arange(2 * t)[None, :]
    rel = d * t + t - 1 - c
    rel = np.clip(rel, 0, seq - 1)
    strips = vec[:, jnp.asarray(rel.astype(np.int32))]
    far = jnp.broadcast_to((t5_bias.astype(F32).T[:, T5_BUCKETS - 1] * LOG2E)[:, None, None],
                           (t5_bias.shape[1], 1, LANES))
    return strips[:, :, None, :], far


def _gdn_body(tb, xq_ref, xk_ref, xv_ref, z_ref, gcol_ref, grow_ref, cw_ref, nw_ref, o_ref, s_ref, carry_ref):
    @pl.when(pl.program_id(2) == 0)
    def _():
        s_ref[...] = jnp.zeros_like(s_ref)
        carry_ref[...] = jnp.zeros_like(carry_ref)

    n_heads = GDN_HEADS_PER_STEP
    c_ = GDN_CHUNK
    nc = tb // c_
    gates = gcol_ref[...]
    lane = lax.broadcasted_iota(I32, gates.shape, 1)

    def conv_silu(g, idx, x_ref):
        x = x_ref[:, g * LANES:(g + 1) * LANES].astype(F32)
        xe = jnp.concatenate([carry_ref[g, idx], x], axis=0)
        taps = cw_ref[g, 4 * idx:4 * idx + 4, :]
        y = taps[3:4] * x
        for tap in range(CONV_WIDTH - 1):
            y = y + taps[tap:tap + 1] * pltpu.roll(xe, CONV_WIDTH - 1 - tap, 0)[SUBLANES:]
        carry_ref[g, idx] = x[tb - SUBLANES:]
        return y * _sigmoid(y)

    q_l, k_l, v_l, beta_l, gc_l, gcr_l = [], [], [], [], [], []
    for g in range(n_heads):
        h = pl.program_id(1) * n_heads + g
        q = conv_silu(g, 0, xq_ref)
        k = conv_silu(g, 1, xk_ref)
        v_l.append(conv_silu(g, 2, xv_ref))
        q_l.append(q * lax.rsqrt(jnp.sum(q * q, axis=-1, keepdims=True) + RMS_EPS) * (B_HEAD_DIM ** -0.5))
        k_l.append(k * lax.rsqrt(jnp.sum(k * k, axis=-1, keepdims=True) + RMS_EPS))
        beta_l.append(jnp.sum(jnp.where(lane == h, gates, 0.0), axis=1, keepdims=True))
        gc_l.append(jnp.sum(jnp.where(lane == B_HEADS + h, gates, 0.0), axis=1, keepdims=True))
        gc_row = grow_ref[g, 1:2, :]
        gcr_l.extend(gc_row[None, :, c * c_:(c + 1) * c_] for c in range(nc))

    def chunked(xs):
        return jnp.concatenate([x.reshape(nc, c_, x.shape[-1]) for x in xs], axis=0)

    def bdot(a, b):
        return lax.dot_general(a.astype(BF16), b.astype(BF16), (((2,), (1,)), ((0,), (0,))),
                               preferred_element_type=F32)

    def bdot_nt(a, b):
        return lax.dot_general(a.astype(BF16), b.astype(BF16), (((2,), (2,)), ((0,), (0,))),
                               preferred_element_type=F32)

    q3, k3, v3, beta3, gc3 = chunked(q_l), chunked(k_l), chunked(v_l), chunked(beta_l), chunked(gc_l)
    gcr3 = jnp.concatenate(gcr_l, axis=0)
    ri = lax.broadcasted_iota(I32, (1, c_, c_), 1)
    ci = lax.broadcasted_iota(I32, (1, c_, c_), 2)
    decay = jnp.exp(jnp.where(ri >= ci, gc3 - gcr3, NEG_INF))
    egc = jnp.exp(gc3)
    kb3 = k3 * beta3
    rhs3 = jnp.concatenate([v3 * beta3, kb3 * egc], axis=2)
    g_last = gc3[:, c_ - 1:c_, :]
    qd3 = q3 * egc
    kd3 = k3 * jnp.exp(g_last - gc3)
    gl3 = jnp.exp(g_last)
    kk = bdot_nt(kb3, k3)
    qk3 = bdot_nt(q3, k3) * decay
    x = -(kk * jnp.where(ri > ci, decay, 0.0))
    r = x
    pw = x
    for _ in range(int(math.log2(c_)) - 1):
        pw = bdot(pw, pw)
        r = r + pw + bdot(r, pw)
    sol = rhs3 + bdot(r, rhs3)

    states = [s_ref[g] for g in range(n_heads)]
    outs = [[] for _ in range(n_heads)]
    for c in range(nc):
        for g in range(n_heads):
            n = g * nc + c
            sb = states[g].astype(BF16)
            v_new = sol[n, :, :B_HEAD_DIM] - _dot(sol[n, :, B_HEAD_DIM:].astype(BF16), sb)
            vnb = v_new.astype(BF16)
            outs[g].append(_dot(qd3[n].astype(BF16), sb) + _dot(qk3[n].astype(BF16), vnb))
            states[g] = states[g] * gl3[n] + _dot_tn(kd3[n].astype(BF16), vnb)
    for g in range(n_heads):
        cols = slice(g * LANES, (g + 1) * LANES)
        s_ref[g] = states[g]
        o = jnp.concatenate(outs[g], axis=0)
        o = o * lax.rsqrt(jnp.mean(o * o, axis=-1, keepdims=True) + RMS_EPS) * nw_ref[...]
        z = z_ref[:, cols].astype(F32)
        o_ref[:, cols] = (o * (z * _sigmoid(z))).astype(o_ref.dtype)


def _gdn(proj, gates, gates_row, conv_w, norm_w, bsz, seq, tb):
    nt = seq // tb
    h_ = B_HEADS
    gh = GDN_HEADS_PER_STEP
    col0 = 3 * A_HEADS
    assert h_ % gh == 0 and col0 % gh == 0
    blk = lambda off: pl.BlockSpec((tb, gh * LANES), lambda b, h, s: (b * nt + s, (col0 + off * h_) // gh + h))
    return pl.pallas_call(
        functools.partial(_gdn_body, tb),
        grid=(bsz, h_ // gh, nt),
        in_specs=[blk(0), blk(1), blk(2), blk(3),
                  pl.BlockSpec((tb, LANES), lambda b, h, s: (b * nt + s, 0)),
                  pl.BlockSpec((gh, 2, tb), lambda b, h, s: (b * (h_ // gh) + h, 0, s)),
                  pl.BlockSpec((gh, 3 * CONV_WIDTH, LANES), lambda b, h, s: (h, 0, 0)),
                  pl.BlockSpec((1, LANES), lambda b, h, s: (0, 0))],
        out_specs=pl.BlockSpec((tb, gh * LANES), lambda b, h, s: (b * nt + s, h)),
        out_shape=jax.ShapeDtypeStruct((bsz * seq, h_ * LANES), BF16),
        scratch_shapes=[pltpu.VMEM((gh, B_HEAD_DIM, B_HEAD_DIM), F32),
                        pltpu.VMEM((gh, 3, SUBLANES, LANES), F32)],
        compiler_params=_cparams(("parallel", "parallel", "arbitrary")),
        name="gdn",
    )(proj, proj, proj, proj, gates, gates_row, conv_w, norm_w)


ROW_TILES = D_MODEL // LANES


def _store_tile_rows(ref, y):
    t = y.shape[0]
    for c in range(ROW_TILES):
        ref[pl.ds(c, t, stride=ROW_TILES), :] = y[:, c * LANES:(c + 1) * LANES]


def _load_tile_rows(ref, t, lead=()):
    return jnp.concatenate([ref[lead + (pl.ds(c, t, stride=ROW_TILES), slice(None))] for c in range(ROW_TILES)],
                           axis=1)


def _outproj_ln_body(a_ref, b_ref, w_ref, x_ref, g_ref, bb_ref, o_ref, o3_ref, wb_ref):
    @pl.when(pl.program_id(0) == 0)
    def _():
        wb_ref[...] = w_ref[...].astype(BF16)

    half = a_ref.shape[1]
    hmix = _dot(a_ref[...], wb_ref[0:half, :]) + _dot(b_ref[...], wb_ref[half:, :])
    out = _layer_norm(DEEPNORM_ALPHA * x_ref[...] + hmix, g_ref[...], bb_ref[...])
    o_ref[...] = out
    _store_tile_rows(o3_ref, out)


def _outproj_ln(a, a_blk, b, b_blk, w, x, g, bb, tm=512):
    n, d = x.shape
    tm = min(tm, n)
    half = d // 2
    row = pl.BlockSpec((1, d), lambda i: (0, 0))
    return pl.pallas_call(
        _outproj_ln_body,
        grid=(n // tm,),
        in_specs=[pl.BlockSpec((tm, half), lambda i: (i, a_blk)),
                  pl.BlockSpec((tm, half), lambda i: (i, b_blk)),
                  pl.BlockSpec((d, d), lambda i: (0, 0)),
                  pl.BlockSpec((tm, d), lambda i: (i, 0)), row, row],
        out_specs=[pl.BlockSpec((tm, d), lambda i: (i, 0)),
                   pl.BlockSpec((tm * ROW_TILES, LANES), lambda i: (i, 0))],
        out_shape=[jax.ShapeDtypeStruct((n, d), F32), jax.ShapeDtypeStruct((n * ROW_TILES, LANES), F32)],
        scratch_shapes=[pltpu.VMEM((d, d), BF16)],
        compiler_params=_cparams(("arbitrary",)),
        name="outproj_ln",
    )(a, b, w, x, g, bb)


def _router_body(tm, x_ref, w_ref, b_ref, idx_ref, gate_ref, rank_ref, cnt_ref, carry_ref):
    @pl.when(pl.program_id(0) == 0)
    def _():
        carry_ref[...] = jnp.zeros_like(carry_ref)

    logits = _dot(x_ref[...], w_ref[...], precision=HIGHEST) + b_ref[...]
    lg = jnp.transpose(logits)[0:N_EXPERTS, :]
    e_iota = lax.broadcasted_iota(I32, (N_EXPERTS, tm), 0).astype(F32)
    vals, idxs, hots = [], [], []
    for _ in range(TOP_K):
        m = jnp.max(lg, axis=0, keepdims=True)
        idx = jnp.min(jnp.where(lg == m, e_iota, float(N_EXPERTS)), axis=0, keepdims=True)
        hot = e_iota == idx
        lg = jnp.where(hot, NEG_INF, lg)
        vals.append(m)
        idxs.append(idx)
        hots.append(hot)
    es = [jnp.exp(v - vals[0]) for v in vals]
    den = es[0] + es[1] + es[2] + es[3]
    sel = jnp.zeros((N_EXPERTS, tm), F32)
    for hot in hots:
        sel = sel + jnp.where(hot, 1.0, 0.0)
    before = jnp.where(lax.broadcasted_iota(I32, (tm, tm), 0) < lax.broadcasted_iota(I32, (tm, tm), 1),
                       1.0, 0.0).astype(BF16)
    cum = _dot(sel.astype(BF16), before) + carry_ref[:, 0:1]
    ranks = [jnp.sum(jnp.where(hot, cum, 0.0), axis=0, keepdims=True) for hot in hots]
    total = carry_ref[...] + jnp.sum(sel, axis=1, keepdims=True)
    carry_ref[...] = total
    idx_ref[...] = jnp.concatenate(idxs, axis=0).astype(I32)
    gate_ref[...] = jnp.concatenate([e / den for e in es], axis=0)
    rank_ref[...] = jnp.concatenate(ranks, axis=0).astype(I32)
    cnt_ref[...] = total


def _router(x, w_pad, b_pad, tm=512):
    n, d = x.shape
    tm = min(tm, n)
    out4 = lambda dt: jax.ShapeDtypeStruct((TOP_K, n), dt)
    blk4 = pl.BlockSpec((TOP_K, tm), lambda i: (0, i))
    return pl.pallas_call(
        functools.partial(_router_body, tm),
        grid=(n // tm,),
        in_specs=[pl.BlockSpec((tm, d), lambda i: (i, 0)),
                  pl.BlockSpec((d, LANES), lambda i: (0, 0)),
                  pl.BlockSpec((1, LANES), lambda i: (0, 0))],
        out_specs=[blk4, blk4, blk4, pl.BlockSpec((N_EXPERTS, LANES), lambda i: (0, 0))],
        out_shape=[out4(I32), out4(F32), out4(I32), jax.ShapeDtypeStruct((N_EXPERTS, LANES), F32)],
        scratch_shapes=[pltpu.VMEM((N_EXPERTS, LANES), F32)],
        compiler_params=_cparams(("arbitrary",)),
        name="moe_router",
    )(x, w_pad, b_pad)


def _row_slab(row):
    return pl.ds(pl.multiple_of(row * ROW_TILES, ROW_TILES), ROW_TILES)


def _start_row_dmas(n_rows, make_copy):
    def issue(r, carry):
        for k in range(TOP_K):
            make_copy(k, r).start()
        return carry

    lax.fori_loop(0, n_rows, issue, 0, unroll=4)


def _wait_row_dmas(n_rows, make_copy):
    def drain(r, carry):
        for k in range(TOP_K):
            make_copy(k, r).wait()
        return carry

    lax.fori_loop(0, n_rows, drain, 0, unroll=8)


def _row_dma_loops(n_rows, make_copy):
    _start_row_dmas(n_rows, make_copy)
    _wait_row_dmas(n_rows, make_copy)


def _dispatch_body(td, dest_ref, fill_ref, x3_ref, xg_out, zero_ref, sem, zsem):
    @pl.when(pl.program_id(0) == 0)
    def _():
        zero_ref[...] = jnp.zeros_like(zero_ref)

        def zero_copy(r):
            return pltpu.make_async_copy(zero_ref, xg_out.at[_row_slab(r)], zsem)

        def each_expert(fn):
            def expert(e, carry):
                lax.fori_loop(fill_ref[0, e], fill_ref[1, e], lambda r, c: (fn(r), c)[1], 0)
                return carry
            lax.fori_loop(0, N_EXPERTS, expert, 0)

        each_expert(lambda r: zero_copy(r).start())
        each_expert(lambda r: zero_copy(r).wait())

    _row_dma_loops(td, lambda k, r: pltpu.make_async_copy(
        x3_ref.at[_row_slab(r)], xg_out.at[_row_slab(dest_ref[k, r])], sem))


def _smem_rows(tile):
    return pl.BlockSpec((TOP_K, tile), lambda i: (0, i), memory_space=pltpu.SMEM)


def _dispatch(x3, dest, fill, n_rows, td=512):
    n = x3.shape[0] // ROW_TILES
    td = min(td, n)
    return pl.pallas_call(
        functools.partial(_dispatch_body, td),
        grid=(n // td,),
        in_specs=[_smem_rows(td),
                  pl.BlockSpec(memory_space=pltpu.SMEM),
                  pl.BlockSpec((td * ROW_TILES, LANES), lambda i: (i, 0))],
        out_specs=pl.BlockSpec(memory_space=pl.ANY),
        out_shape=jax.ShapeDtypeStruct((n_rows * ROW_TILES, LANES), x3.dtype),
        scratch_shapes=[pltpu.VMEM((ROW_TILES, LANES), x3.dtype), pltpu.SemaphoreType.DMA(()),
                        pltpu.SemaphoreType.DMA(())],
        compiler_params=_cparams(("arbitrary",)),
        name="moe_dispatch",
    )(dest, fill, x3)


def _expert_body(be_ref, nu_ref, x_ref, wgu_ref, bgu_ref, wd_ref, bd_ref, o_ref, wgu_b, wd_b):
    i = pl.program_id(0)
    prev = be_ref[jnp.maximum(i - 1, 0)]
    fresh = jnp.logical_or(i == 0, be_ref[i] != prev)

    @pl.when(jnp.logical_and(i < nu_ref[0], fresh))
    def _():
        wgu_b[...] = wgu_ref[...].astype(BF16)
        wd_b[...] = wd_ref[...].astype(BF16)

    @pl.when(i < nu_ref[0])
    def _():
        x = _load_tile_rows(x_ref, MOE_ROWS).astype(BF16)
        hcat = _dot(x, wgu_b[...]) + bgu_ref[...]
        g = jnp.minimum(hcat[:, :D_FF], SWIGLU_LIMIT)
        u = jnp.clip(hcat[:, D_FF:], -SWIGLU_LIMIT, SWIGLU_LIMIT)
        act = g * _sigmoid(SWIGLU_ALPHA * g) * (u + 1.0)
        _store_tile_rows(o_ref, _dot(act.astype(BF16), wd_b[...]) + bd_ref[...])

    @pl.when(i >= nu_ref[0])
    def _():
        o_ref[...] = jnp.zeros_like(o_ref)


def _experts(xg, layer, block_expert, n_used, w_gu, b_gu, w_down, b_down):
    d = D_MODEL
    nb = xg.shape[0] // (MOE_ROWS * ROW_TILES)
    blk = pl.BlockSpec((MOE_ROWS * ROW_TILES, LANES), lambda i, be, nu: (i, 0))
    grid_spec = pltpu.PrefetchScalarGridSpec(
        num_scalar_prefetch=2,
        grid=(nb,),
        in_specs=[
            pl.BlockSpec((MOE_ROWS * ROW_TILES, LANES), lambda i, be, nu: (jnp.minimum(i, nu[0] - 1), 0)),
            pl.BlockSpec((None, None, d, 2 * D_FF), lambda i, be, nu: (layer, be[i], 0, 0)),
            pl.BlockSpec((None, None, 1, 2 * D_FF), lambda i, be, nu: (layer, be[i], 0, 0)),
            pl.BlockSpec((None, None, D_FF, d), lambda i, be, nu: (layer, be[i], 0, 0)),
            pl.BlockSpec((None, None, 1, d), lambda i, be, nu: (layer, be[i], 0, 0)),
        ],
        out_specs=blk,
        scratch_shapes=[pltpu.VMEM((d, 2 * D_FF), BF16), pltpu.VMEM((D_FF, d), BF16)],
    )
    return pl.pallas_call(
        _expert_body,
        grid_spec=grid_spec,
        out_shape=jax.ShapeDtypeStruct(xg.shape, F32),
        compiler_params=_cparams(("arbitrary",)),
        name="moe_experts",
    )(block_expert, n_used, xg, w_gu, b_gu[:, :, None, :], w_down, b_down[:, :, None, :])


def _combine_ln_body(tc, dest_ref, dest_next_ref, yg_hbm, gate_ref, x_ref, g_ref, b_ref, o_ref, ybuf, sems):
    i = pl.program_id(0)
    slot = i % 2

    def gather(rows_ref, s):
        return lambda k, r: pltpu.make_async_copy(
            yg_hbm.at[_row_slab(rows_ref[k, r])], ybuf.at[s, k, _row_slab(r)], sems.at[s])

    @pl.when(i == 0)
    def _():
        _start_row_dmas(tc, gather(dest_ref, slot))

    @pl.when(i + 1 < pl.num_programs(0))
    def _():
        _start_row_dmas(tc, gather(dest_next_ref, 1 - slot))

    _wait_row_dmas(tc, gather(dest_ref, slot))
    gate = gate_ref[...]
    hmoe = gate[:, 0:1] * _load_tile_rows(ybuf, tc, (slot, 0))
    for k in range(1, TOP_K):
        hmoe = hmoe + gate[:, k:k + 1] * _load_tile_rows(ybuf, tc, (slot, k))
    o_ref[...] = _layer_norm(DEEPNORM_ALPHA * x_ref[...] + hmoe, g_ref[...], b_ref[...])


def _combine_ln(yg, dest, gate_t, x, g, b, tc=256):
    n, d = x.shape
    tc = min(tc, n)
    steps = n // tc
    row = pl.BlockSpec((1, d), lambda i: (0, 0))
    return pl.pallas_call(
        functools.partial(_combine_ln_body, tc),
        grid=(steps,),
        in_specs=[_smem_rows(tc),
                  pl.BlockSpec((TOP_K, tc), lambda i: (0, jnp.minimum(i + 1, steps - 1)), memory_space=pltpu.SMEM),
                  pl.BlockSpec(memory_space=pl.ANY),
                  pl.BlockSpec((tc, TOP_K), lambda i: (i, 0)),
                  pl.BlockSpec((tc, d), lambda i: (i, 0)), row, row],
        out_specs=pl.BlockSpec((tc, d), lambda i: (i, 0)),
        out_shape=jax.ShapeDtypeStruct((n, d), F32),
        scratch_shapes=[pltpu.VMEM((2, TOP_K, tc * ROW_TILES, LANES), F32), pltpu.SemaphoreType.DMA((2,))],
        compiler_params=_cparams(("arbitrary",)),
        name="moe_combine_ln",
    )(dest, dest, yg, gate_t, x, g, b)


def _moe_ln(x, x3, layer, router_w, router_b, w_gu, b_gu, w_down, b_down, ln_g, ln_b):
    n, d = x.shape
    idx, gate, rank, cnt = _router(x, _pad_lanes(router_w), _pad_lanes(router_b[None, :]))
    counts = cnt[:, 0].astype(I32)
    padded = (counts + MOE_ROWS - 1) // MOE_ROWS * MOE_ROWS
    pad_end = jnp.cumsum(padded)
    pad_start = pad_end - padded
    hot = idx[:, :, None] == jnp.arange(N_EXPERTS, dtype=I32)[None, None, :]
    dest = jnp.sum(jnp.where(hot, pad_start[None, None, :], 0), axis=-1) + rank
    n_blocks = -(-(n * TOP_K) // MOE_ROWS) + N_EXPERTS
    block_row0 = jnp.arange(n_blocks, dtype=I32) * MOE_ROWS
    block_expert = jnp.minimum(jnp.sum((pad_end[None, :] <= block_row0[:, None]).astype(I32), axis=1),
                               N_EXPERTS - 1)
    n_used = (pad_end[-1:] // MOE_ROWS).astype(I32)
    fill = jnp.stack([pad_start + counts, pad_end])
    xg = _dispatch(x3, dest, fill, n_blocks * MOE_ROWS)
    yg = _experts(xg, layer, block_expert, n_used, w_gu, b_gu, w_down, b_down)
    return _combine_ln(yg, dest, gate.T, x, ln_g[None, :], ln_b[None, :])


def _even_mixer_ln(x, bsz, seq, w_in, w_out, lam_params, subln_w, conv_w, a_log, dt_bias, gdn_norm_w,
                   t5_bias, lam_init, ln_g, ln_b):
    main = 3 * A_HEADS * LANES + 4 * B_HEADS * B_HEAD_DIM
    n_q = A_HEADS * 2 * A_HEAD_DIM
    col_scale = jnp.concatenate([jnp.full((1, n_q), A_HEAD_DIM ** -0.5 * LOG2E, F32),
                                 jnp.ones((1, main - n_q), F32)], axis=1)
    proj = _proj(x, w_in[:, :main], col_scale)
    zeros4 = jnp.zeros((B_HEADS,), F32)
    p0 = _pad_lanes(jnp.concatenate([zeros4, a_log.astype(F32)])[None, :])
    p1 = _pad_lanes(jnp.concatenate([zeros4, dt_bias.astype(F32)])[None, :])
    gates = _gates("even", x, _pad_lanes(w_in[:, main:]), p0, p1, seq)
    t = min(ATTN_TILE, seq)
    strips, far = _t5_strips(t5_bias, seq, t)
    ao = _diff_attention(proj, bsz, seq, strips, far, lam_params.astype(F32), subln_w[None, :].astype(F32),
                         lam_init, t)
    g8 = gates[:, :2 * B_HEADS].reshape(bsz, seq, 2, B_HEADS)
    gates_row = g8.transpose(0, 3, 2, 1).reshape(bsz * B_HEADS, 2, seq)
    cw = conv_w.astype(F32).reshape(CONV_WIDTH, 3, B_HEADS, B_HEAD_DIM).transpose(2, 1, 0, 3)
    cw = cw.reshape(B_HEADS, 3 * CONV_WIDTH, B_HEAD_DIM)
    bo = _gdn(proj, gates, gates_row, cw, gdn_norm_w[None, :].astype(F32), bsz, seq, min(GDN_TILE, seq))
    return _outproj_ln(ao, 0, bo, 0, w_out, x, ln_g[None, :], ln_b[None, :])


def _odd_mixer_ln(x, bsz, seq, w_in, w_out, qk_norm_w, forget_b, ln_g, ln_b):
    main = 4 * C_HEADS * C_HEAD_DIM
    proj = _proj(x, w_in[:, :main], jnp.ones((1, main), F32))
    fb =_pad_lanes(forget_b.astype(F32)[None, :])
    cum = _gates("odd", x, _pad_lanes(w_in[:, main:]), fb, fb, seq)
    cum_t = cum[:, :C_HEADS].reshape(bsz, seq, C_HEADS // 2, 2).transpose(0, 2, 3, 1)
    scale = jnp.asarray([C_HEAD_DIM ** -0.5 * LOG2E, 1.0], F32)[:, None]
    wqk = qk_norm_w.astype(F32) * scale
    w2 = jnp.tile(wqk, (1, 2))[:, None, :]
    qk = _qknorm(proj, w2)
    logit_bound = 1.02 * C_HEAD_DIM * jnp.max(jnp.abs(wqk[0])) * jnp.max(jnp.abs(wqk[1]))
    o = _fox_attention(qk, proj, cum_t, logit_bound, bsz, seq, min(ATTN_TILE, seq))
    return _outproj_ln(o, 0, o, 1, w_out, x, ln_g[None, :], ln_b[None, :])


def kernel(x, t5_bias, even_w_in, even_w_out, diff_lambda, diff_subln_w, gdn_conv_w, gdn_a_log, gdn_dt_bias, gdn_norm_w, odd_w_in, odd_w_out, fox_qk_norm_w, fox_forget_b, router_w, router_b, moe_w_gate_up, moe_b_gate_up, moe_w_down, moe_b_down, ln_mix_g, ln_mix_b, ln_ffn_g, ln_ffn_b):
    bsz, seq, d = x.shape
    xf = x.reshape(bsz * seq, d)
    for layer in range(DEPTH):
        i = layer // 2
        if layer % 2 == 0:
            lam_init = 0.8 - 0.6 * math.exp(-0.3 * layer)
            xf, x3 = _even_mixer_ln(xf, bsz, seq, even_w_in[i], even_w_out[i], diff_lambda[i], diff_subln_w[i],
                                    gdn_conv_w[i], gdn_a_log[i], gdn_dt_bias[i], gdn_norm_w[i], t5_bias, lam_init,
                                    ln_mix_g[layer], ln_mix_b[layer])
        else:
            xf, x3 = _odd_mixer_ln(xf, bsz, seq, odd_w_in[i], odd_w_out[i], fox_qk_norm_w[i], fox_forget_b[i],
                                   ln_mix_g[layer], ln_mix_b[layer])
        xf = _moe_ln(xf, x3, layer, router_w[layer], router_b[layer], moe_w_gate_up, moe_b_gate_up,
                     moe_w_down, moe_b_down, ln_ffn_g[layer], ln_ffn_b[layer])
    return xf.reshape(bsz, seq, d)
```

```python
import functools
import math

import numpy as np
import jax
import jax.numpy as jnp
from jax import lax
from jax.experimental import pallas as pl
from jax.experimental.pallas import tpu as pltpu

F32 = jnp.float32
BF16 = jnp.bfloat16
I32 = jnp.int32
HIGHEST = lax.Precision.HIGHEST

D_MODEL = 1024
DEPTH = 4
A_HEADS = 4
A_HEAD_DIM = 64
B_HEADS = 4
B_HEAD_DIM = 128
CONV_WIDTH = 4
C_HEADS = 16
C_HEAD_DIM = 64
T5_BUCKETS = 32
T5_MAX_DISTANCE = 2048
N_EXPERTS = 32
TOP_K = 4
D_FF = D_MODEL
SWIGLU_LIMIT = 7.0
SWIGLU_ALPHA = 1.702
DEEPNORM_ALPHA = (2 * DEPTH) ** 0.25
LN_EPS = 1e-5
RMS_EPS = 1e-6

LANES = 128
SUBLANES = 8
VMEM_LIMIT = 56 * 1024 * 1024

ATTN_TILE = 1024
ATTN_ROWS = 1024
GDN_CHUNK = 128
GDN_TILE = 512
GDN_HEADS_PER_STEP = 4
MOE_ROWS = 256
FOX_SKIP_MARGIN = 170.0
NEG_INF = float("-inf")
LOG2E = math.log2(math.e)


def _cparams(sem, vmem=VMEM_LIMIT):
    return pltpu.CompilerParams(dimension_semantics=sem, vmem_limit_bytes=vmem)


def _dot(a, b, **kw):
    return jnp.dot(a, b, preferred_element_type=F32, **kw)


def _dot_nt(a, b, **kw):
    return lax.dot_general(a, b, (((1,), (1,)), ((), ())), preferred_element_type=F32, **kw)


def _dot_tn(a, b, **kw):
    return lax.dot_general(a, b, (((0,), (0,)), ((), ())), preferred_element_type=F32, **kw)


def _sigmoid(x):
    return 1.0 / (1.0 + jnp.exp(-x))


def _softplus(x):
    return jnp.maximum(x, 0.0) + jnp.log(1.0 + jnp.exp(-jnp.abs(x)))


def _layer_norm(xf, g, b):
    mu = jnp.mean(xf, axis=-1, keepdims=True)
    xc = xf - mu
    var = jnp.mean(xc * xc, axis=-1, keepdims=True)
    return xc * lax.rsqrt(var + LN_EPS) * g + b


PROJ_CHUNK = 512


def _proj_body(norm_chunks, x_ref, w_ref, cs_ref, gsum_ref, o_ref):
    xb = x_ref[...].astype(BF16)
    for c in range(w_ref.shape[1] // PROJ_CHUNK):
        cols = slice(c * PROJ_CHUNK, (c + 1) * PROJ_CHUNK)
        acc = _dot(xb, w_ref[:, cols])
        if c in norm_chunks:
            ss = _dot((acc * acc).astype(BF16), gsum_ref[...])
            acc = acc * lax.rsqrt(ss * (1.0 / C_HEAD_DIM) + RMS_EPS)
        o_ref[:, cols] = (acc * cs_ref[:, cols]).astype(o_ref.dtype)


def _proj(x, w, col_scale, norm_chunks=(), tm=512):
    n, k = x.shape
    m = w.shape[1]
    tm = min(tm, n)
    head = jnp.arange(PROJ_CHUNK, dtype=I32) // C_HEAD_DIM
    gsum = (head[:, None] == head[None, :]).astype(BF16)
    return pl.pallas_call(
        functools.partial(_proj_body, tuple(norm_chunks)),
        grid=(n // tm,),
        in_specs=[pl.BlockSpec((tm, k), lambda i: (i, 0)),
                  pl.BlockSpec((k, m), lambda i: (0, 0)),
                  pl.BlockSpec((1, m), lambda i: (0, 0)),
                  pl.BlockSpec((PROJ_CHUNK, PROJ_CHUNK), lambda i: (0, 0))],
        out_specs=pl.BlockSpec((tm, m), lambda i: (i, 0)),
        out_shape=jax.ShapeDtypeStruct((n, m), BF16),
        compiler_params=_cparams(("parallel",)),
        name="proj",
    )(x, w.astype(BF16), col_scale, gsum)


def _gates_body(mode, tm, steps_per_seq, x_ref, w_ref, p0_ref, p1_ref, o_ref, carry_ref):
    z = _dot(x_ref[...], w_ref[...], precision=HIGHEST)
    ri = lax.broadcasted_iota(I32, (tm, tm), 0)
    ci = lax.broadcasted_iota(I32, (tm, tm), 1)
    if mode == "even":
        lane = lax.broadcasted_iota(I32, (tm, LANES), 1)
        beta = _sigmoid(z)
        g = -jnp.exp(p0_ref[...]) * _softplus(z + p1_ref[...])
        shift = int(math.log2(GDN_CHUNK))
        same_chunk = jnp.right_shift(ri, shift) == jnp.right_shift(ci, shift)
        tri = jnp.where(jnp.logical_and(same_chunk, ci <= ri), 1.0, 0.0)
        gc = _dot(tri, g, precision=HIGHEST)
        o_ref[...] = jnp.where(lane < B_HEADS, beta, gc)
    else:
        @pl.when(pl.program_id(0) % steps_per_seq == 0)
        def _():
            carry_ref[...] = jnp.zeros_like(carry_ref)

        logf = -_softplus(-(z + p0_ref[...]))
        tri = jnp.where(ci <= ri, 1.0, 0.0)
        cum = _dot(tri, logf, precision=HIGHEST) + carry_ref[0:1, :]
        o_ref[...] = cum
        carry_ref[...] = jnp.broadcast_to(cum[tm - 1:tm, :], carry_ref.shape)


def _gates(mode, x, w_small, p0, p1, seq_len, tm=512):
    n, k = x.shape
    tm = min(tm, seq_len)
    row = pl.BlockSpec((1, LANES), lambda i: (0, 0))
    return pl.pallas_call(
        functools.partial(_gates_body, mode, tm, seq_len // tm),
        grid=(n // tm,),
        in_specs=[pl.BlockSpec((tm, k), lambda i: (i, 0)),
                  pl.BlockSpec((k, LANES), lambda i: (0, 0)), row, row],
        out_specs=pl.BlockSpec((tm, LANES), lambda i: (i, 0)),
        out_shape=jax.ShapeDtypeStruct((n, LANES), F32),
        scratch_shapes=[pltpu.VMEM((SUBLANES, LANES), F32)],
        compiler_params=_cparams(("arbitrary",)),
        name="gates_" + mode,
    )(x, w_small, p0, p1)


def _pad_lanes(a):
    return jnp.pad(a, [(0, 0)] * (a.ndim - 1) + [(0, LANES - a.shape[-1])])


def _attn_body(mode, t, lam_init, *rest):
    if mode == "diff":
        q_ref, k_ref, v_ref, strip_ref, far_ref, lam_ref, subln_ref, o_ref, qs_ref, m_ref, acc_ref, l_ref = rest
    else:
        jmin_ref, q_ref, k_ref, v_ref, ck_ref, gate_ref, o_ref, qs_ref, m_ref, acc_ref = rest
    i = pl.program_id(2)
    rc = min(ATTN_ROWS, t)

    q = q_ref[...]
    lane = lax.broadcasted_iota(I32, q.shape, 1)
    zero = jnp.zeros_like(q)
    qs_ref[0:t, :] = jnp.where(lane < 64, q, zero)
    qs_ref[t:2 * t, :] = jnp.where(lane < 64, zero, q)
    m_ref[...] = jnp.full(m_ref.shape, NEG_INF, F32)
    acc_ref[...] = jnp.zeros_like(acc_ref)
    if mode == "diff":
        l_ref[...] = jnp.zeros_like(l_ref)
    else:
        c0 = ck_ref[:, pl.ds(pl.multiple_of(i * t, t), LANES)][:, 0:1]

    def step(kind, j, d=0):
        keys = pl.ds(pl.multiple_of(j * t, t), t)
        k = k_ref[keys, :]
        v = v_ref[keys, :]
        if mode == "fox":
            lane = lax.broadcasted_iota(I32, v.shape, 1)
            one = jnp.ones_like(v)
            rhs = (jnp.where(lane < 64, v, one), jnp.where(lane < 64, one, v))
            bias = (c0 - ck_ref[:, keys]) * LOG2E
        elif kind != "far":
            x = jnp.broadcast_to(strip_ref[d], (t, 2 * t))
            tile = pltpu.roll(x, t + 1, 1, stride=1, stride_axis=0)[:, :t]
        for c in range(2 * t // rc):
            r0 = c * rc
            half = r0 // t
            h0 = r0 - half * t
            rows = slice(r0, r0 + rc)
            s = _dot_nt(qs_ref[rows, :], k)
            if mode == "fox":
                s = s + bias[half:half + 1]
            elif kind == "far":
                s = s + far_ref[0:1, 0:1]
            else:
                s = s + tile[h0:h0 + rc]
            if kind == "diag":
                rr = lax.broadcasted_iota(I32, (rc, t), 0) + h0
                cc = lax.broadcasted_iota(I32, (rc, t), 1)
                s = jnp.where(rr >= cc, s, NEG_INF)
            m_prev = m_ref[rows, :]
            m_new = jnp.maximum(m_prev, jnp.max(s, axis=1, keepdims=True))
            alpha = jnp.exp2(m_prev - m_new)
            pr = jnp.exp2(s - pltpu.repeat(m_new, t // LANES, 1))
            if mode == "diff":
                l_ref[rows, :] = alpha * l_ref[rows, :] + jnp.sum(pr, axis=1, keepdims=True)
                pv = _dot(pr.astype(BF16), v)
            else:
                pv = _dot(pr.astype(BF16), rhs[half])
            acc_ref[rows, :] = alpha * acc_ref[rows, :] + pv
            m_ref[rows, :] = m_new

    def each_block(kind, lo, hi):
        lax.fori_loop(lo, hi, lambda j, carry: (step(kind, j), carry)[1], 0)

    if mode == "diff":
        n_near = strip_ref.shape[0] - 1
        each_block("far", 0, jnp.maximum(i - n_near, 0))
        for d in range(n_near, 0, -1):
            pl.when(i >= d)(functools.partial(step, "near", i - d, d))
    else:
        row = (pl.program_id(0) * pl.num_programs(1) + pl.program_id(1)) * pl.num_programs(2)
        each_block("off", jmin_ref[row + i], i)
    step("diag", i)

    acc = acc_ref[...]
    if mode == "diff":
        o = acc * (1.0 / l_ref[...])
        lp = lam_ref[...]
        lam = (jnp.exp(jnp.sum(lp[0:1] * lp[1:2], axis=-1, keepdims=True))
               - jnp.exp(jnp.sum(lp[2:3] * lp[3:4], axis=-1, keepdims=True)) + lam_init)
        dlt = o[0:t] - lam * o[t:2 * t]
        ms = jnp.mean(dlt * dlt, axis=-1, keepdims=True)
        out = dlt * lax.rsqrt(ms + RMS_EPS) * subln_ref[...] * (1.0 - lam_init)
    else:
        lane = lax.broadcasted_iota(I32, (t, LANES), 1)
        lo = acc[0:t]
        hi = acc[t:2 * t]
        out = jnp.where(lane < 64, lo * (1.0 / lo[:, 64:65]), hi * (1.0 / hi[:, 0:1]))
        out = out * _sigmoid(gate_ref[...].astype(F32))
    o_ref[...] = out.astype(o_ref.dtype)


def _attn_scratch(mode, t):
    base = [pltpu.VMEM((2 * t, LANES), BF16), pltpu.VMEM((2 * t, LANES), F32), pltpu.VMEM((2 * t, LANES), F32)]
    return base + ([pltpu.VMEM((2 * t, LANES), F32)] if mode == "diff" else [])


def _diff_attention(proj, bsz, seq, strips, far, lam_params, subln_w, lam_init, t):
    nq = seq // t
    nd = strips.shape[1]
    h_ = A_HEADS
    return pl.pallas_call(
        functools.partial(_attn_body, "diff", t, lam_init),
        grid=(bsz, h_, nq),
        in_specs=[
            pl.BlockSpec((t, LANES), lambda b, h, i: (b * nq + i, h)),
            pl.BlockSpec((seq, LANES), lambda b, h, i: (b, h_ + h)),
            pl.BlockSpec((seq, LANES), lambda b, h, i: (b, 2 * h_ + h)),
            pl.BlockSpec((None, nd, 1, 2 * t), lambda b, h, i: (h, 0, 0, 0)),
            pl.BlockSpec((None, 1, LANES), lambda b, h, i: (h, 0, 0)),
            pl.BlockSpec(lam_params.shape, lambda b, h, i: (0, 0)),
            pl.BlockSpec((1, LANES), lambda b, h, i: (0, 0)),
        ],
        out_specs=pl.BlockSpec((t, LANES), lambda b, h, i: (b * nq + i, h)),
        out_shape=jax.ShapeDtypeStruct((bsz * seq, h_ * LANES), BF16),
        scratch_shapes=_attn_scratch("diff", t),
        compiler_params=_cparams(("parallel", "parallel", "arbitrary")),
        name="diff_attn",
    )(proj, proj, proj, strips, far, lam_params, subln_w)


def _fox_first_live_block(cum_t, logit_bound, t):
    bsz, hp, _, seq = cum_t.shape
    nq = seq // t
    c_start = cum_t[..., 0::t]
    c_end = cum_t[..., t - 1::t]
    gap = jnp.max(c_start[..., :, None] - c_end[..., None, :], axis=2) * LOG2E
    dead = gap < -(FOX_SKIP_MARGIN + 2.0 * logit_bound)
    dead = jnp.logical_and(dead, jnp.arange(nq)[None, :] < jnp.arange(nq)[:, None])
    return jnp.sum(jnp.cumprod(dead.astype(I32), axis=-1), axis=-1).reshape(-1)


def _fox_attention(qk, proj, cum_t, logit_bound, bsz, seq, t):
    nq = seq // t
    hp = C_HEADS // 2
    jmin = _fox_first_live_block(cum_t, logit_bound, t)
    grid_spec = pltpu.PrefetchScalarGridSpec(
        num_scalar_prefetch=1,
        grid=(bsz, hp, nq),
        in_specs=[
            pl.BlockSpec((t, LANES), lambda b, h, i, jm: (b * nq + i, h)),
            pl.BlockSpec((seq, LANES), lambda b, h, i, jm: (b, hp + h)),
            pl.BlockSpec((seq, LANES), lambda b, h, i, jm: (b, 2 * hp + h)),
            pl.BlockSpec((None, None, 2, seq), lambda b, h, i, jm: (b, h, 0, 0)),
            pl.BlockSpec((t, LANES), lambda b, h, i, jm: (b * nq + i, 3 * hp + h)),
        ],
        out_specs=pl.BlockSpec((t, LANES), lambda b, h, i, jm: (b * nq + i, h)),
        scratch_shapes=_attn_scratch("fox", t),
    )
    return pl.pallas_call(
        functools.partial(_attn_body, "fox", t, 0.0),
        grid_spec=grid_spec,
        out_shape=jax.ShapeDtypeStruct((bsz * seq, hp * LANES), BF16),
        compiler_params=_cparams(("parallel", "parallel", "arbitrary")),
        name="fox_attn",
    )(jmin, qk, qk, proj, cum_t, proj)


def _t5_bias_by_distance(t5_bias, seq):
    n = jnp.arange(seq, dtype=I32)
    max_exact = T5_BUCKETS // 2
    nf = jnp.maximum(n, 1).astype(F32)
    large = max_exact + (jnp.log(nf / max_exact) / math.log(T5_MAX_DISTANCE / max_exact)
                         * (T5_BUCKETS - max_exact)).astype(I32)
    large = jnp.minimum(large, T5_BUCKETS - 1)
    bucket = jnp.where(n < max_exact, n, large)
    return t5_bias.astype(F32).T[:, bucket]


def _t5_strips(t5_bias, seq, t):
    vec = _t5_bias_by_distance(t5_bias, seq) * LOG2E
    nd = 1
    while nd * t - (t - 1) < T5_MAX_DISTANCE and nd < seq // t:
        nd += 1
    d = np.arange(nd)[:, None]
    c = np.arange(2 * t)[None, :]
    rel = d * t + t - 1 - c
    rel = np.clip(rel, 0, seq - 1)
    strips = vec[:, jnp.asarray(rel.astype(np.int32))]
    far = jnp.broadcast_to((t5_bias.astype(F32).T[:, T5_BUCKETS - 1] * LOG2E)[:, None, None],
                           (t5_bias.shape[1], 1, LANES))
    return strips[:, :, None, :], far


def _gdn_body(tb, xq_ref, xk_ref, xv_ref, z_ref, gcol_ref, grow_ref, cw_ref, nw_ref, o_ref, s_ref, carry_ref):
    @pl.when(pl.program_id(2) == 0)
    def _():
        s_ref[...] = jnp.zeros_like(s_ref)
        carry_ref[...] = jnp.zeros_like(carry_ref)

    n_heads = GDN_HEADS_PER_STEP
    c_ = GDN_CHUNK
    nc = tb // c_
    gates = gcol_ref[...]
    lane = lax.broadcasted_iota(I32, gates.shape, 1)

    def conv_silu(g, idx, x_ref):
        x = x_ref[:, g * LANES:(g + 1) * LANES].astype(F32)
        xe = jnp.concatenate([carry_ref[g, idx], x], axis=0)
        taps = cw_ref[g, 4 * idx:4 * idx + 4, :]
        y = taps[3:4] * x
        for tap in range(CONV_WIDTH - 1):
            y = y + taps[tap:tap + 1] * pltpu.roll(xe, CONV_WIDTH - 1 - tap, 0)[SUBLANES:]
        carry_ref[g, idx] = x[tb - SUBLANES:]
        return y * _sigmoid(y)

    q_l, k_l, v_l, beta_l, gc_l, gcr_l = [], [], [], [], [], []
    for g in range(n_heads):
        h = pl.program_id(1) * n_heads + g
        q = conv_silu(g, 0, xq_ref)
        k = conv_silu(g, 1, xk_ref)
        v_l.append(conv_silu(g, 2, xv_ref))
        q_l.append(q * lax.rsqrt(jnp.sum(q * q, axis=-1, keepdims=True) + RMS_EPS) * (B_HEAD_DIM ** -0.5))
        k_l.append(k * lax.rsqrt(jnp.sum(k * k, axis=-1, keepdims=True) + RMS_EPS))
        beta_l.append(jnp.sum(jnp.where(lane == h, gates, 0.0), axis=1, keepdims=True))
        gc_l.append(jnp.sum(jnp.where(lane == B_HEADS + h, gates, 0.0), axis=1, keepdims=True))
        gc_row = grow_ref[g, 1:2, :]
        gcr_l.extend(gc_row[None, :, c * c_:(c + 1) * c_] for c in range(nc))

    def chunked(xs):
        return jnp.concatenate([x.reshape(nc, c_, x.shape[-1]) for x in xs], axis=0)

    def bdot(a, b):
        return lax.dot_general(a.astype(BF16), b.astype(BF16), (((2,), (1,)), ((0,), (0,))),
                               preferred_element_type=F32)

    def bdot_nt(a, b):
        return lax.dot_general(a.astype(BF16), b.astype(BF16), (((2,), (2,)), ((0,), (0,))),
                               preferred_element_type=F32)

    q3, k3, v3, beta3, gc3 = chunked(q_l), chunked(k_l), chunked(v_l), chunked(beta_l), chunked(gc_l)
    gcr3 = jnp.concatenate(gcr_l, axis=0)
    ri = lax.broadcasted_iota(I32, (1, c_, c_), 1)
    ci = lax.broadcasted_iota(I32, (1, c_, c_), 2)
    decay = jnp.exp(jnp.where(ri >= ci, gc3 - gcr3, NEG_INF))
    egc = jnp.exp(gc3)
    kb3 = k3 * beta3
    rhs3 = jnp.concatenate([v3 * beta3, kb3 * egc], axis=2)
    g_last = gc3[:, c_ - 1:c_, :]
    qd3 = q3 * egc
    kd3 = k3 * jnp.exp(g_last - gc3)
    gl3 = jnp.exp(g_last)
    kk = bdot_nt(kb3, k3)
    qk3 = bdot_nt(q3, k3) * decay
    x = -(kk * jnp.where(ri > ci, decay, 0.0))
    r = x
    pw = x
    for _ in range(int(math.log2(c_)) - 1):
        pw = bdot(pw, pw)
        r = r + pw + bdot(r, pw)
    sol = rhs3 + bdot(r, rhs3)

    states = [s_ref[g] for g in range(n_heads)]
    outs = [[] for _ in range(n_heads)]
    for c in range(nc):
        for g in range(n_heads):
            n = g * nc + c
            sb = states[g].astype(BF16)
            v_new = sol[n, :, :B_HEAD_DIM] - _dot(sol[n, :, B_HEAD_DIM:].astype(BF16), sb)
            vnb = v_new.astype(BF16)
            outs[g].append(_dot(qd3[n].astype(BF16), sb) + _dot(qk3[n].astype(BF16), vnb))
            states[g] = states[g] * gl3[n] + _dot_tn(kd3[n].astype(BF16), vnb)
    for g in range(n_heads):
        cols = slice(g * LANES, (g + 1) * LANES)
        s_ref[g] = states[g]
        o = jnp.concatenate(outs[g], axis=0)
        o = o * lax.rsqrt(jnp.mean(o * o, axis=-1, keepdims=True) + RMS_EPS) * nw_ref[...]
        z = z_ref[:, cols].astype(F32)
        o_ref[:, cols] = (o * (z * _sigmoid(z))).astype(o_ref.dtype)


def _gdn(proj, gates, gates_row, conv_w, norm_w, bsz, seq, tb):
    nt = seq // tb
    h_ = B_HEADS
    gh = GDN_HEADS_PER_STEP
    col0 = 3 * A_HEADS
    assert h_ % gh == 0 and col0 % gh == 0
    blk = lambda off: pl.BlockSpec((tb, gh * LANES), lambda b, h, s: (b * nt + s, (col0 + off * h_) // gh + h))
    return pl.pallas_call(
        functools.partial(_gdn_body, tb),
        grid=(bsz, h_ // gh, nt),
        in_specs=[blk(0), blk(1), blk(2), blk(3),
                  pl.BlockSpec((tb, LANES), lambda b, h, s: (b * nt + s, 0)),
                  pl.BlockSpec((gh, 2, tb), lambda b, h, s: (b * (h_ // gh) + h, 0, s)),
                  pl.BlockSpec((gh, 3 * CONV_WIDTH, LANES), lambda b, h, s: (h, 0, 0)),
                  pl.BlockSpec((1, LANES), lambda b, h, s: (0, 0))],
        out_specs=pl.BlockSpec((tb, gh * LANES), lambda b, h, s: (b * nt + s, h)),
        out_shape=jax.ShapeDtypeStruct((bsz * seq, h_ * LANES), BF16),
        scratch_shapes=[pltpu.VMEM((gh, B_HEAD_DIM, B_HEAD_DIM), F32),
                        pltpu.VMEM((gh, 3, SUBLANES, LANES), F32)],
        compiler_params=_cparams(("parallel", "parallel", "arbitrary")),
        name="gdn",
    )(proj, proj, proj, proj, gates, gates_row, conv_w, norm_w)


ROW_TILES = D_MODEL // LANES


def _store_tile_rows(ref, y):
    t = y.shape[0]
    for c in range(ROW_TILES):
        ref[pl.ds(c, t, stride=ROW_TILES), :] = y[:, c * LANES:(c + 1) * LANES]


def _load_tile_rows(ref, t, lead=()):
    return jnp.concatenate([ref[lead + (pl.ds(c, t, stride=ROW_TILES), slice(None))] for c in range(ROW_TILES)],
                           axis=1)


def _outproj_ln_body(a_ref, b_ref, w_ref, x_ref, g_ref, bb_ref, o_ref, o3_ref, wb_ref):
    @pl.when(pl.program_id(0) == 0)
    def _():
        wb_ref[...] = w_ref[...].astype(BF16)

    half = a_ref.shape[1]
    hmix = _dot(a_ref[...], wb_ref[0:half, :]) + _dot(b_ref[...], wb_ref[half:, :])
    out = _layer_norm(DEEPNORM_ALPHA * x_ref[...] + hmix, g_ref[...], bb_ref[...])
    o_ref[...] = out
    _store_tile_rows(o3_ref, out)


def _outproj_ln(a, a_blk, b, b_blk, w, x, g, bb, tm=512):
    n, d = x.shape
    tm = min(tm, n)
    half = d // 2
    row = pl.BlockSpec((1, d), lambda i: (0, 0))
    return pl.pallas_call(
        _outproj_ln_body,
        grid=(n // tm,),
        in_specs=[pl.BlockSpec((tm, half), lambda i: (i, a_blk)),
                  pl.BlockSpec((tm, half), lambda i: (i, b_blk)),
                  pl.BlockSpec((d, d), lambda i: (0, 0)),
                  pl.BlockSpec((tm, d), lambda i: (i, 0)), row, row],
        out_specs=[pl.BlockSpec((tm, d), lambda i: (i, 0)),
                   pl.BlockSpec((tm * ROW_TILES, LANES), lambda i: (i, 0))],
        out_shape=[jax.ShapeDtypeStruct((n, d), F32), jax.ShapeDtypeStruct((n * ROW_TILES, LANES), F32)],
        scratch_shapes=[pltpu.VMEM((d, d), BF16)],
        compiler_params=_cparams(("arbitrary",)),
        name="outproj_ln",
    )(a, b, w, x, g, bb)


def _router_body(tm, x_ref, w_ref, b_ref, idx_ref, gate_ref, rank_ref, cnt_ref, carry_ref):
    @pl.when(pl.program_id(0) == 0)
    def _():
        carry_ref[...] = jnp.zeros_like(carry_ref)

    logits = _dot(x_ref[...], w_ref[...], precision=HIGHEST) + b_ref[...]
    lg = jnp.transpose(logits)[0:N_EXPERTS, :]
    e_iota = lax.broadcasted_iota(I32, (N_EXPERTS, tm), 0).astype(F32)
    vals, idxs, hots = [], [], []
    for _ in range(TOP_K):
        m = jnp.max(lg, axis=0, keepdims=True)
        idx = jnp.min(jnp.where(lg == m, e_iota, float(N_EXPERTS)), axis=0, keepdims=True)
        hot = e_iota == idx
        lg = jnp.where(hot, NEG_INF, lg)
        vals.append(m)
        idxs.append(idx)
        hots.append(hot)
    es = [jnp.exp(v - vals[0]) for v in vals]
    den = es[0] + es[1] + es[2] + es[3]
    sel = jnp.zeros((N_EXPERTS, tm), F32)
    for hot in hots:
        sel = sel + jnp.where(hot, 1.0, 0.0)
    before = jnp.where(lax.broadcasted_iota(I32, (tm, tm), 0) < lax.broadcasted_iota(I32, (tm, tm), 1),
                       1.0, 0.0).astype(BF16)
    cum = _dot(sel.astype(BF16), before) + carry_ref[:, 0:1]
    ranks = [jnp.sum(jnp.where(hot, cum, 0.0), axis=0, keepdims=True) for hot in hots]
    total = carry_ref[...] + jnp.sum(sel, axis=1, keepdims=True)
    carry_ref[...] = total
    idx_ref[...] = jnp.concatenate(idxs, axis=0).astype(I32)
    gate_ref[...] = jnp.concatenate([e / den for e in es], axis=0)
    rank_ref[...] = jnp.concatenate(ranks, axis=0).astype(I32)
    cnt_ref[...] = total


def _router(x, w_pad, b_pad, tm=512):
    n, d = x.shape
    tm = min(tm, n)
    out4 = lambda dt: jax.ShapeDtypeStruct((TOP_K, n), dt)
    blk4 = pl.BlockSpec((TOP_K, tm), lambda i: (0, i))
    return pl.pallas_call(
        functools.partial(_router_body, tm),
        grid=(n // tm,),
        in_specs=[pl.BlockSpec((tm, d), lambda i: (i, 0)),
                  pl.BlockSpec((d, LANES), lambda i: (0, 0)),
                  pl.BlockSpec((1, LANES), lambda i: (0, 0))],
        out_specs=[blk4, blk4, blk4, pl.BlockSpec((N_EXPERTS, LANES), lambda i: (0, 0))],
        out_shape=[out4(I32), out4(F32), out4(I32), jax.ShapeDtypeStruct((N_EXPERTS, LANES), F32)],
        scratch_shapes=[pltpu.VMEM((N_EXPERTS, LANES), F32)],
        compiler_params=_cparams(("arbitrary",)),
        name="moe_router",
    )(x, w_pad, b_pad)


def _row_slab(row):
    return pl.ds(pl.multiple_of(row * ROW_TILES, ROW_TILES), ROW_TILES)


def _start_row_dmas(n_rows, make_copy):
    def issue(r, carry):
        for k in range(TOP_K):
            make_copy(k, r).start()
        return carry

    lax.fori_loop(0, n_rows, issue, 0, unroll=4)


def _wait_row_dmas(n_rows, make_copy):
    def drain(r, carry):
        for k in range(TOP_K):
            make_copy(k, r).wait()
        return carry

    lax.fori_loop(0, n_rows, drain, 0, unroll=8)


def _row_dma_loops(n_rows, make_copy):
    _start_row_dmas(n_rows, make_copy)
    _wait_row_dmas(n_rows, make_copy)


def _dispatch_body(td, dest_ref, fill_ref, x3_ref, xg_out, zero_ref, sem, zsem):
    @pl.when(pl.program_id(0) == 0)
    def _():
        zero_ref[...] = jnp.zeros_like(zero_ref)

        def zero_copy(r):
            return pltpu.make_async_copy(zero_ref, xg_out.at[_row_slab(r)], zsem)

        def each_expert(fn):
            def expert(e, carry):
                lax.fori_loop(fill_ref[0, e], fill_ref[1, e], lambda r, c: (fn(r), c)[1], 0)
                return carry
            lax.fori_loop(0, N_EXPERTS, expert, 0)

        each_expert(lambda r: zero_copy(r).start())
        each_expert(lambda r: zero_copy(r).wait())

    _row_dma_loops(td, lambda k, r: pltpu.make_async_copy(
        x3_ref.at[_row_slab(r)], xg_out.at[_row_slab(dest_ref[k, r])], sem))


def _smem_rows(tile):
    return pl.BlockSpec((TOP_K, tile), lambda i: (0, i), memory_space=pltpu.SMEM)


def _dispatch(x3, dest, fill, n_rows, td=512):
    n = x3.shape[0] // ROW_TILES
    td = min(td, n)
    return pl.pallas_call(
        functools.partial(_dispatch_body, td),
        grid=(n // td,),
        in_specs=[_smem_rows(td),
                  pl.BlockSpec(memory_space=pltpu.SMEM),
                  pl.BlockSpec((td * ROW_TILES, LANES), lambda i: (i, 0))],
        out_specs=pl.BlockSpec(memory_space=pl.ANY),
        out_shape=jax.ShapeDtypeStruct((n_rows * ROW_TILES, LANES), x3.dtype),
        scratch_shapes=[pltpu.VMEM((ROW_TILES, LANES), x3.dtype), pltpu.SemaphoreType.DMA(()),
                        pltpu.SemaphoreType.DMA(())],
        compiler_params=_cparams(("arbitrary",)),
        name="moe_dispatch",
    )(dest, fill, x3)


def _expert_body(be_ref, nu_ref, x_ref, wgu_ref, bgu_ref, wd_ref, bd_ref, o_ref, wgu_b, wd_b):
    i = pl.program_id(0)
    prev = be_ref[jnp.maximum(i - 1, 0)]
    fresh = jnp.logical_or(i == 0, be_ref[i] != prev)

    @pl.when(jnp.logical_and(i < nu_ref[0], fresh))
    def _():
        wgu_b[...] = wgu_ref[...].astype(BF16)
        wd_b[...] = wd_ref[...].astype(BF16)

    @pl.when(i < nu_ref[0])
    def _():
        x = _load_tile_rows(x_ref, MOE_ROWS).astype(BF16)
        hcat = _dot(x, wgu_b[...]) + bgu_ref[...]
        g = jnp.minimum(hcat[:, :D_FF], SWIGLU_LIMIT)
        u = jnp.clip(hcat[:, D_FF:], -SWIGLU_LIMIT, SWIGLU_LIMIT)
        act = g * _sigmoid(SWIGLU_ALPHA * g) * (u + 1.0)
        _store_tile_rows(o_ref, _dot(act.astype(BF16), wd_b[...]) + bd_ref[...])

    @pl.when(i >= nu_ref[0])
    def _():
        o_ref[...] = jnp.zeros_like(o_ref)


def _experts(xg, layer, block_expert, n_used, w_gu, b_gu, w_down, b_down):
    d = D_MODEL
    nb = xg.shape[0] // (MOE_ROWS * ROW_TILES)
    blk = pl.BlockSpec((MOE_ROWS * ROW_TILES, LANES), lambda i, be, nu: (i, 0))
    grid_spec = pltpu.PrefetchScalarGridSpec(
        num_scalar_prefetch=2,
        grid=(nb,),
        in_specs=[
            pl.BlockSpec((MOE_ROWS * ROW_TILES, LANES), lambda i, be, nu: (jnp.minimum(i, nu[0] - 1), 0)),
            pl.BlockSpec((None, None, d, 2 * D_FF), lambda i, be, nu: (layer, be[i], 0, 0)),
            pl.BlockSpec((None, None, 1, 2 * D_FF), lambda i, be, nu: (layer, be[i], 0, 0)),
            pl.BlockSpec((None, None, D_FF, d), lambda i, be, nu: (layer, be[i], 0, 0)),
            pl.BlockSpec((None, None, 1, d), lambda i, be, nu: (layer, be[i], 0, 0)),
        ],
        out_specs=blk,
        scratch_shapes=[pltpu.VMEM((d, 2 * D_FF), BF16), pltpu.VMEM((D_FF, d), BF16)],
    )
    return pl.pallas_call(
        _expert_body,
        grid_spec=grid_spec,
        out_shape=jax.ShapeDtypeStruct(xg.shape, F32),
        compiler_params=_cparams(("arbitrary",)),
        name="moe_experts",
    )(block_expert, n_used, xg, w_gu, b_gu[:, :, None, :], w_down, b_down[:, :, None, :])


def _combine_ln_body(tc, dest_ref, dest_next_ref, yg_hbm, gate_ref, x_ref, g_ref, b_ref, o_ref, ybuf, sems):
    i = pl.program_id(0)
    slot = i % 2

    def gather(rows_ref, s):
        return lambda k, r: pltpu.make_async_copy(
            yg_hbm.at[_row_slab(rows_ref[k, r])], ybuf.at[s, k, _row_slab(r)], sems.at[s])

    @pl.when(i == 0)
    def _():
        _start_row_dmas(tc, gather(dest_ref, slot))

    @pl.when(i + 1 < pl.num_programs(0))
    def _():
        _start_row_dmas(tc, gather(dest_next_ref, 1 - slot))

    _wait_row_dmas(tc, gather(dest_ref, slot))
    gate = gate_ref[...]
    hmoe = gate[:, 0:1] * _load_tile_rows(ybuf, tc, (slot, 0))
    for k in range(1, TOP_K):
        hmoe = hmoe + gate[:, k:k + 1] * _load_tile_rows(ybuf, tc, (slot, k))
    o_ref[...] = _layer_norm(DEEPNORM_ALPHA * x_ref[...] + hmoe, g_ref[...], b_ref[...])


def _combine_ln(yg, dest, gate_t, x, g, b, tc=256):
    n, d = x.shape
    tc = min(tc, n)
    steps = n // tc
    row = pl.BlockSpec((1, d), lambda i: (0, 0))
    return pl.pallas_call(
        functools.partial(_combine_ln_body, tc),
        grid=(steps,),
        in_specs=[_smem_rows(tc),
                  pl.BlockSpec((TOP_K, tc), lambda i: (0, jnp.minimum(i + 1, steps - 1)), memory_space=pltpu.SMEM),
                  pl.BlockSpec(memory_space=pl.ANY),
                  pl.BlockSpec((tc, TOP_K), lambda i: (i, 0)),
                  pl.BlockSpec((tc, d), lambda i: (i, 0)), row, row],
        out_specs=pl.BlockSpec((tc, d), lambda i: (i, 0)),
        out_shape=jax.ShapeDtypeStruct((n, d), F32),
        scratch_shapes=[pltpu.VMEM((2, TOP_K, tc * ROW_TILES, LANES), F32), pltpu.SemaphoreType.DMA((2,))],
        compiler_params=_cparams(("arbitrary",)),
        name="moe_combine_ln",
    )(dest, dest, yg, gate_t, x, g, b)


def _moe_ln(x, x3, layer, router_w, router_b, w_gu, b_gu, w_down, b_down, ln_g, ln_b):
    n, d = x.shape
    idx, gate, rank, cnt = _router(x, _pad_lanes(router_w), _pad_lanes(router_b[None, :]))
    counts = cnt[:, 0].astype(I32)
    padded = (counts + MOE_ROWS - 1) // MOE_ROWS * MOE_ROWS
    pad_end = jnp.cumsum(padded)
    pad_start = pad_end - padded
    hot = idx[:, :, None] == jnp.arange(N_EXPERTS, dtype=I32)[None, None, :]
    dest = jnp.sum(jnp.where(hot, pad_start[None, None, :], 0), axis=-1) + rank
    n_blocks = -(-(n * TOP_K) // MOE_ROWS) + N_EXPERTS
    block_row0 = jnp.arange(n_blocks, dtype=I32) * MOE_ROWS
    block_expert = jnp.minimum(jnp.sum((pad_end[None, :] <= block_row0[:, None]).astype(I32), axis=1),
                               N_EXPERTS - 1)
    n_used = (pad_end[-1:] // MOE_ROWS).astype(I32)
    fill = jnp.stack([pad_start + counts, pad_end])
    xg = _dispatch(x3, dest, fill, n_blocks * MOE_ROWS)
    yg = _experts(xg, layer, block_expert, n_used, w_gu, b_gu, w_down, b_down)
    return _combine_ln(yg, dest, gate.T, x, ln_g[None, :], ln_b[None, :])


def _even_mixer_ln(x, bsz, seq, w_in, w_out, lam_params, subln_w, conv_w, a_log, dt_bias, gdn_norm_w,
                   t5_bias, lam_init, ln_g, ln_b):
    main = 3 * A_HEADS * LANES + 4 * B_HEADS * B_HEAD_DIM
    n_q = A_HEADS * 2 * A_HEAD_DIM
    col_scale = jnp.concatenate([jnp.full((1, n_q), A_HEAD_DIM ** -0.5 * LOG2E, F32),
                                 jnp.ones((1, main - n_q), F32)], axis=1)
    proj = _proj(x, w_in[:, :main], col_scale)
    zeros4 = jnp.zeros((B_HEADS,), F32)
    p0 = _pad_lanes(jnp.concatenate([zeros4, a_log.astype(F32)])[None, :])
    p1 = _pad_lanes(jnp.concatenate([zeros4, dt_bias.astype(F32)])[None, :])
    gates = _gates("even", x, _pad_lanes(w_in[:, main:]), p0, p1, seq)
    t = min(ATTN_TILE, seq)
    strips, far = _t5_strips(t5_bias, seq, t)
    ao = _diff_attention(proj, bsz, seq, strips, far, lam_params.astype(F32), subln_w[None, :].astype(F32),
                         lam_init, t)
    g8 = gates[:, :2 * B_HEADS].reshape(bsz, seq, 2, B_HEADS)
    gates_row = g8.transpose(0, 3, 2, 1).reshape(bsz * B_HEADS, 2, seq)
    cw = conv_w.astype(F32).reshape(CONV_WIDTH, 3, B_HEADS, B_HEAD_DIM).transpose(2, 1, 0, 3)
    cw = cw.reshape(B_HEADS, 3 * CONV_WIDTH, B_HEAD_DIM)
    bo = _gdn(proj, gates, gates_row, cw, gdn_norm_w[None, :].astype(F32), bsz, seq, min(GDN_TILE, seq))
    return _outproj_ln(ao, 0, bo, 0, w_out, x, ln_g[None, :], ln_b[None, :])


def _odd_mixer_ln(x, bsz, seq, w_in, w_out, qk_norm_w, forget_b, ln_g, ln_b):
    main = 4 * C_HEADS * C_HEAD_DIM
    width = C_HEADS * C_HEAD_DIM
    scale = jnp.asarray([C_HEAD_DIM ** -0.5 * LOG2E, 1.0], F32)[:, None]
    wqk = qk_norm_w.astype(F32) * scale
    col_scale = jnp.concatenate([jnp.tile(wqk[0], C_HEADS), jnp.tile(wqk[1], C_HEADS),
                                 jnp.ones((main - 2 * width,), F32)])[None, :]
    proj = _proj(x, w_in[:, :main], col_scale, norm_chunks=range(2 * width // PROJ_CHUNK))
    fb =_pad_lanes(forget_b.astype(F32)[None, :])
    cum = _gates("odd", x, _pad_lanes(w_in[:, main:]), fb, fb, seq)
    cum_t = cum[:, :C_HEADS].reshape(bsz, seq, C_HEADS // 2, 2).transpose(0, 2, 3, 1)
    logit_bound = 1.02 * C_HEAD_DIM * jnp.max(jnp.abs(wqk[0])) * jnp.max(jnp.abs(wqk[1]))
    o = _fox_attention(proj, proj, cum_t, logit_bound, bsz, seq, min(ATTN_TILE, seq))
    return _outproj_ln(o, 0, o, 1, w_out, x, ln_g[None, :], ln_b[None, :])


def kernel(x, t5_bias, even_w_in, even_w_out, diff_lambda, diff_subln_w, gdn_conv_w, gdn_a_log, gdn_dt_bias, gdn_norm_w, odd_w_in, odd_w_out, fox_qk_norm_w, fox_forget_b, router_w, router_b, moe_w_gate_up, moe_b_gate_up, moe_w_down, moe_b_down, ln_mix_g, ln_mix_b, ln_ffn_g, ln_ffn_b):
    bsz, seq, d = x.shape
    xf = x.reshape(bsz * seq, d)
    for layer in range(DEPTH):
        i = layer // 2
        if layer % 2 == 0:
            lam_init = 0.8 - 0.6 * math.exp(-0.3 * layer)
            xf, x3 = _even_mixer_ln(xf, bsz, seq, even_w_in[i], even_w_out[i], diff_lambda[i], diff_subln_w[i],
                                    gdn_conv_w[i], gdn_a_log[i], gdn_dt_bias[i], gdn_norm_w[i], t5_bias, lam_init,
                                    ln_mix_g[layer], ln_mix_b[layer])
        else:
            xf, x3 = _odd_mixer_ln(xf, bsz, seq, odd_w_in[i], odd_w_out[i], fox_qk_norm_w[i], fox_forget_b[i],
                                   ln_mix_g[layer], ln_mix_b[layer])
        xf = _moe_ln(xf, x3, layer, router_w[layer], router_b[layer], moe_w_gate_up, moe_b_gate_up,
                     moe_w_down, moe_b_down, ln_ffn_g[layer], ln_ffn_b[layer])
    return xf.reshape(bsz, seq, d)
```

```python
import functools
import math

import numpy as np
import jax
import jax.numpy as jnp
from jax import lax
from jax.experimental import pallas as pl
from jax.experimental.pallas import tpu as pltpu

F32 = jnp.float32
BF16 = jnp.bfloat16
I32 = jnp.int32
HIGHEST = lax.Precision.HIGHEST

D_MODEL = 1024
DEPTH = 4
A_HEADS = 4
A_HEAD_DIM = 64
B_HEADS = 4
B_HEAD_DIM = 128
CONV_WIDTH = 4
C_HEADS = 16
C_HEAD_DIM = 64
T5_BUCKETS = 32
T5_MAX_DISTANCE = 2048
N_EXPERTS = 32
TOP_K = 4
D_FF = D_MODEL
SWIGLU_LIMIT = 7.0
SWIGLU_ALPHA = 1.702
DEEPNORM_ALPHA = (2 * DEPTH) ** 0.25
LN_EPS = 1e-5
RMS_EPS = 1e-6

LANES = 128
SUBLANES = 8
VMEM_LIMIT = 56 * 1024 * 1024

ATTN_TILE = 1024
ATTN_ROWS = 1024
GDN_CHUNK = 128
GDN_TILE = 512
GDN_HEADS_PER_STEP = 4
MOE_ROWS = 512
FOX_SKIP_MARGIN = 170.0
NEG_INF = float("-inf")
LOG2E = math.log2(math.e)


def _cparams(sem, vmem=VMEM_LIMIT):
    return pltpu.CompilerParams(dimension_semantics=sem, vmem_limit_bytes=vmem)


def _dot(a, b, **kw):
    return jnp.dot(a, b, preferred_element_type=F32, **kw)


def _dot_nt(a, b, **kw):
    return lax.dot_general(a, b, (((1,), (1,)), ((), ())), preferred_element_type=F32, **kw)


def _dot_tn(a, b, **kw):
    return lax.dot_general(a, b, (((0,), (0,)), ((), ())), preferred_element_type=F32, **kw)


def _sigmoid(x):
    return 1.0 / (1.0 + jnp.exp(-x))


def _softplus(x):
    return jnp.maximum(x, 0.0) + jnp.log(1.0 + jnp.exp(-jnp.abs(x)))


def _layer_norm(xf, g, b):
    mu = jnp.mean(xf, axis=-1, keepdims=True)
    xc = xf - mu
    var = jnp.mean(xc * xc, axis=-1, keepdims=True)
    return xc * lax.rsqrt(var + LN_EPS) * g + b


PROJ_CHUNK = 512


def _proj_body(norm_chunks, x_ref, w_ref, cs_ref, gsum_ref, o_ref):
    xb = x_ref[...].astype(BF16)
    for c in range(w_ref.shape[1] // PROJ_CHUNK):
        cols = slice(c * PROJ_CHUNK, (c + 1) * PROJ_CHUNK)
        acc = _dot(xb, w_ref[:, cols])
        if c in norm_chunks:
            ss = _dot((acc * acc).astype(BF16), gsum_ref[...])
            acc = acc * lax.rsqrt(ss * (1.0 / C_HEAD_DIM) + RMS_EPS)
        o_ref[:, cols] = (acc * cs_ref[:, cols]).astype(o_ref.dtype)


def _proj(x, w, col_scale, norm_chunks=(), tm=512):
    n, k = x.shape
    m = w.shape[1]
    tm = min(tm, n)
    head = jnp.arange(PROJ_CHUNK, dtype=I32) // C_HEAD_DIM
    gsum = (head[:, None] == head[None, :]).astype(BF16)
    return pl.pallas_call(
        functools.partial(_proj_body, tuple(norm_chunks)),
        grid=(n // tm,),
        in_specs=[pl.BlockSpec((tm, k), lambda i: (i, 0)),
                  pl.BlockSpec((k, m), lambda i: (0, 0)),
                  pl.BlockSpec((1, m), lambda i: (0, 0)),
                  pl.BlockSpec((PROJ_CHUNK, PROJ_CHUNK), lambda i: (0, 0))],
        out_specs=pl.BlockSpec((tm, m), lambda i: (i, 0)),
        out_shape=jax.ShapeDtypeStruct((n, m), BF16),
        compiler_params=_cparams(("parallel",)),
        name="proj",
    )(x, w.astype(BF16), col_scale, gsum)


def _gates_body(mode, tm, steps_per_seq, x_ref, w_ref, p0_ref, p1_ref, o_ref, carry_ref):
    z = _dot(x_ref[...], w_ref[...], precision=HIGHEST)
    ri = lax.broadcasted_iota(I32, (tm, tm), 0)
    ci = lax.broadcasted_iota(I32, (tm, tm), 1)
    if mode == "even":
        lane = lax.broadcasted_iota(I32, (tm, LANES), 1)
        beta = _sigmoid(z)
        g = -jnp.exp(p0_ref[...]) * _softplus(z + p1_ref[...])
        shift = int(math.log2(GDN_CHUNK))
        same_chunk = jnp.right_shift(ri, shift) == jnp.right_shift(ci, shift)
        tri = jnp.where(jnp.logical_and(same_chunk, ci <= ri), 1.0, 0.0)
        gc = _dot(tri, g, precision=HIGHEST)
        o_ref[...] = jnp.where(lane < B_HEADS, beta, gc)
    else:
        @pl.when(pl.program_id(0) % steps_per_seq == 0)
        def _():
            carry_ref[...] = jnp.zeros_like(carry_ref)

        logf = -_softplus(-(z + p0_ref[...]))
        tri = jnp.where(ci <= ri, 1.0, 0.0)
        cum = _dot(tri, logf, precision=HIGHEST) + carry_ref[0:1, :]
        o_ref[...] = cum
        carry_ref[...] = jnp.broadcast_to(cum[tm - 1:tm, :], carry_ref.shape)


def _gates(mode, x, w_small, p0, p1, seq_len, tm=512):
    n, k = x.shape
    tm = min(tm, seq_len)
    row = pl.BlockSpec((1, LANES), lambda i: (0, 0))
    return pl.pallas_call(
        functools.partial(_gates_body, mode, tm, seq_len // tm),
        grid=(n // tm,),
        in_specs=[pl.BlockSpec((tm, k), lambda i: (i, 0)),
                  pl.BlockSpec((k, LANES), lambda i: (0, 0)), row, row],
        out_specs=pl.BlockSpec((tm, LANES), lambda i: (i, 0)),
        out_shape=jax.ShapeDtypeStruct((n, LANES), F32),
        scratch_shapes=[pltpu.VMEM((SUBLANES, LANES), F32)],
        compiler_params=_cparams(("arbitrary",)),
        name="gates_" + mode,
    )(x, w_small, p0, p1)


def _pad_lanes(a):
    return jnp.pad(a, [(0, 0)] * (a.ndim - 1) + [(0, LANES - a.shape[-1])])


def _attn_body(mode, t, lam_init, *rest):
    if mode == "diff":
        q_ref, k_ref, v_ref, strip_ref, lam_ref, subln_ref, o_ref, qs_ref, m_ref, acc_ref, l_ref, tile_ref = rest
    else:
        jmin_ref, q_ref, k_ref, v_ref, ck_ref, gate_ref, o_ref, qs_ref, m_ref, acc_ref = rest
    i = pl.program_id(2)
    rc = min(ATTN_ROWS, t)

    q = q_ref[...]
    lane = lax.broadcasted_iota(I32, q.shape, 1)
    zero = jnp.zeros_like(q)
    qs_ref[0:t, :] = jnp.where(lane < 64, q, zero)
    qs_ref[t:2 * t, :] = jnp.where(lane < 64, zero, q)
    m_ref[...] = jnp.full(m_ref.shape, NEG_INF, F32)
    acc_ref[...] = jnp.zeros_like(acc_ref)
    if mode == "diff":
        l_ref[...] = jnp.zeros_like(l_ref)

        @pl.when(i == 0)
        def _():
            for d in range(strip_ref.shape[0]):
                x = jnp.broadcast_to(strip_ref[d], (t, 2 * t))
                tile_ref[d] = pltpu.roll(x, t + 1, 1, stride=1, stride_axis=0)[:, :t]
    else:
        c0 = ck_ref[:, pl.ds(pl.multiple_of(i * t, t), LANES)][:, 0:1]

    def step(kind, j, d=0):
        keys = pl.ds(pl.multiple_of(j * t, t), t)
        k = k_ref[keys, :]
        v = v_ref[keys, :]
        if mode == "fox":
            lane = lax.broadcasted_iota(I32, v.shape, 1)
            one = jnp.ones_like(v)
            rhs = (jnp.where(lane < 64, v, one), jnp.where(lane < 64, one, v))
            bias = (c0 - ck_ref[:, keys]) * LOG2E
        for c in range(2 * t // rc):
            r0 = c * rc
            half = r0 // t
            h0 = r0 - half * t
            rows = slice(r0, r0 + rc)
            s = _dot_nt(qs_ref[rows, :], k)
            if mode == "fox":
                s = s + bias[half:half + 1]
            elif kind != "far":
                s = s + tile_ref[d, h0:h0 + rc, :]
            if kind == "diag":
                rr = lax.broadcasted_iota(I32, (rc, t), 0) + h0
                cc = lax.broadcasted_iota(I32, (rc, t), 1)
                s = jnp.where(rr >= cc, s, NEG_INF)
            m_prev = m_ref[rows, :]
            m_new = jnp.maximum(m_prev, jnp.max(s, axis=1, keepdims=True))
            alpha = jnp.exp2(m_prev - m_new)
            pr = jnp.exp2(s - pltpu.repeat(m_new, t // LANES, 1))
            if mode == "diff":
                l_ref[rows, :] = alpha * l_ref[rows, :] + jnp.sum(pr, axis=1, keepdims=True)
                pv = _dot(pr.astype(BF16), v)
            else:
                pv = _dot(pr.astype(BF16), rhs[half])
            acc_ref[rows, :] = alpha * acc_ref[rows, :] + pv
            m_ref[rows, :] = m_new

    def each_block(kind, lo, hi):
        lax.fori_loop(lo, hi, lambda j, carry: (step(kind, j), carry)[1], 0)

    if mode == "diff":
        n_near = strip_ref.shape[0] - 1
        each_block("far", 0, jnp.maximum(i - n_near, 0))
        for d in range(n_near, 0, -1):
            pl.when(i >= d)(functools.partial(step, "near", i - d, d))
    else:
        row = (pl.program_id(0) * pl.num_programs(1) + pl.program_id(1)) * pl.num_programs(2)
        each_block("off", jmin_ref[row + i], i)
    step("diag", i)

    acc = acc_ref[...]
    if mode == "diff":
        o = acc * (1.0 / l_ref[...])
        lp = lam_ref[...]
        lam = (jnp.exp(jnp.sum(lp[0:1] * lp[1:2], axis=-1, keepdims=True))
               - jnp.exp(jnp.sum(lp[2:3] * lp[3:4], axis=-1, keepdims=True)) + lam_init)
        dlt = o[0:t] - lam * o[t:2 * t]
        ms = jnp.mean(dlt * dlt, axis=-1, keepdims=True)
        out = dlt * lax.rsqrt(ms + RMS_EPS) * subln_ref[...] * (1.0 - lam_init)
    else:
        lane = lax.broadcasted_iota(I32, (t, LANES), 1)
        lo = acc[0:t]
        hi = acc[t:2 * t]
        out = jnp.where(lane < 64, lo * (1.0 / lo[:, 64:65]), hi * (1.0 / hi[:, 0:1]))
        out = out * _sigmoid(gate_ref[...].astype(F32))
    o_ref[...] = out.astype(o_ref.dtype)


def _attn_scratch(mode, t, n_tiles=0):
    base = [pltpu.VMEM((2 * t, LANES), BF16), pltpu.VMEM((2 * t, LANES), F32), pltpu.VMEM((2 * t, LANES), F32)]
    if mode == "diff":
        base += [pltpu.VMEM((2 * t, LANES), F32), pltpu.VMEM((n_tiles, t, t), F32)]
    return base


def _diff_attention(proj, bsz, seq, strips, lam_params, subln_w, lam_init, t):
    nq = seq // t
    nd = strips.shape[1]
    h_ = A_HEADS
    return pl.pallas_call(
        functools.partial(_attn_body, "diff", t, lam_init),
        grid=(bsz, h_, nq),
        in_specs=[
            pl.BlockSpec((t, LANES), lambda b, h, i: (b * nq + i, h)),
            pl.BlockSpec((seq, LANES), lambda b, h, i: (b, h_ + h)),
            pl.BlockSpec((seq, LANES), lambda b, h, i: (b, 2 * h_ + h)),
            pl.BlockSpec((None, nd, 1, 2 * t), lambda b, h, i: (h, 0, 0, 0)),
            pl.BlockSpec(lam_params.shape, lambda b, h, i: (0, 0)),
            pl.BlockSpec((1, LANES), lambda b, h, i: (0, 0)),
        ],
        out_specs=pl.BlockSpec((t, LANES), lambda b, h, i: (b * nq + i, h)),
        out_shape=jax.ShapeDtypeStruct((bsz * seq, h_ * LANES), BF16),
        scratch_shapes=_attn_scratch("diff", t, nd),
        compiler_params=_cparams(("parallel", "parallel", "arbitrary")),
        name="diff_attn",
    )(proj, proj, proj, strips, lam_params, subln_w)


def _fox_first_live_block(cum_t, logit_bound, t):
    bsz, hp, _, seq = cum_t.shape
    nq = seq // t
    c_start = cum_t[..., 0::t]
    c_end = cum_t[..., t - 1::t]
    gap = jnp.max(c_start[..., :, None] - c_end[..., None, :], axis=2) * LOG2E
    dead = gap < -(FOX_SKIP_MARGIN + 2.0 * logit_bound)
    dead = jnp.logical_and(dead, jnp.arange(nq)[None, :] < jnp.arange(nq)[:, None])
    return jnp.sum(jnp.cumprod(dead.astype(I32), axis=-1), axis=-1).reshape(-1)


def _fox_attention(qk, proj, cum_t, logit_bound, bsz, seq, t):
    nq = seq // t
    hp = C_HEADS // 2
    jmin = _fox_first_live_block(cum_t, logit_bound, t)
    grid_spec = pltpu.PrefetchScalarGridSpec(
        num_scalar_prefetch=1,
        grid=(bsz, hp, nq),
        in_specs=[
            pl.BlockSpec((t, LANES), lambda b, h, i, jm: (b * nq + i, h)),
            pl.BlockSpec((seq, LANES), lambda b, h, i, jm: (b, hp + h)),
            pl.BlockSpec((seq, LANES), lambda b, h, i, jm: (b, 2 * hp + h)),
            pl.BlockSpec((None, None, 2, seq), lambda b, h, i, jm: (b, h, 0, 0)),
            pl.BlockSpec((t, LANES), lambda b, h, i, jm: (b * nq + i, 3 * hp + h)),
        ],
        out_specs=pl.BlockSpec((t, LANES), lambda b, h, i, jm: (b * nq + i, h)),
        scratch_shapes=_attn_scratch("fox", t),
    )
    return pl.pallas_call(
        functools.partial(_attn_body, "fox", t, 0.0),
        grid_spec=grid_spec,
        out_shape=jax.ShapeDtypeStruct((bsz * seq, hp * LANES), BF16),
        compiler_params=_cparams(("parallel", "parallel", "arbitrary")),
        name="fox_attn",
    )(jmin, qk, qk, proj, cum_t, proj)


def _t5_bias_by_distance(t5_bias, seq):
    n = jnp.arange(seq, dtype=I32)
    max_exact = T5_BUCKETS // 2
    nf = jnp.maximum(n, 1).astype(F32)
    large = max_exact + (jnp.log(nf / max_exact) / math.log(T5_MAX_DISTANCE / max_exact)
                         * (T5_BUCKETS - max_exact)).astype(I32)
    large = jnp.minimum(large, T5_BUCKETS - 1)
    bucket = jnp.where(n < max_exact, n, large)
    return t5_bias.astype(F32).T[:, bucket]


def _t5_strips(t5_bias, seq, t):
    last = t5_bias.astype(F32).T[:, T5_BUCKETS - 1:T5_BUCKETS]
    vec = (_t5_bias_by_distance(t5_bias, seq) - last) * LOG2E
    nd = 1
    while nd * t - (t - 1) < T5_MAX_DISTANCE and nd < seq // t:
        nd += 1
    d = np.arange(nd)[:, None]
    c = np.arange(2 * t)[None, :]
    rel = d * t + t - 1 - c
    rel = np.clip(rel, 0, seq - 1)
    strips = vec[:, jnp.asarray(rel.astype(np.int32))]
    return strips[:, :, None, :]


def _gdn_body(tb, xq_ref, xk_ref, xv_ref, z_ref, gcol_ref, grow_ref, cw_ref, nw_ref, o_ref, s_ref, carry_ref):
    @pl.when(pl.program_id(2) == 0)
    def _():
        s_ref[...] = jnp.zeros_like(s_ref)
        carry_ref[...] = jnp.zeros_like(carry_ref)

    n_heads = GDN_HEADS_PER_STEP
    c_ = GDN_CHUNK
    nc = tb // c_
    gates = gcol_ref[...]
    lane = lax.broadcasted_iota(I32, gates.shape, 1)

    def conv_silu(g, idx, x_ref):
        x = x_ref[:, g * LANES:(g + 1) * LANES].astype(F32)
        xe = jnp.concatenate([carry_ref[g, idx], x], axis=0)
        taps = cw_ref[g, 4 * idx:4 * idx + 4, :]
        y = taps[3:4] * x
        for tap in range(CONV_WIDTH - 1):
            y = y + taps[tap:tap + 1] * pltpu.roll(xe, CONV_WIDTH - 1 - tap, 0)[SUBLANES:]
        carry_ref[g, idx] = x[tb - SUBLANES:]
        return y * _sigmoid(y)

    q_l, k_l, v_l, beta_l, gc_l, gcr_l = [], [], [], [], [], []
    for g in range(n_heads):
        h = pl.program_id(1) * n_heads + g
        q = conv_silu(g, 0, xq_ref)
        k = conv_silu(g, 1, xk_ref)
        v_l.append(conv_silu(g, 2, xv_ref))
        q_l.append(q * lax.rsqrt(jnp.sum(q * q, axis=-1, keepdims=True) + RMS_EPS) * (B_HEAD_DIM ** -0.5))
        k_l.append(k * lax.rsqrt(jnp.sum(k * k, axis=-1, keepdims=True) + RMS_EPS))
        beta_l.append(jnp.sum(jnp.where(lane == h, gates, 0.0), axis=1, keepdims=True))
        gc_l.append(jnp.sum(jnp.where(lane == B_HEADS + h, gates, 0.0), axis=1, keepdims=True))
        gc_row = grow_ref[g, 1:2, :]
        gcr_l.extend(gc_row[None, :, c * c_:(c + 1) * c_] for c in range(nc))

    def chunked(xs):
        return jnp.concatenate([x.reshape(nc, c_, x.shape[-1]) for x in xs], axis=0)

    def bdot(a, b):
        return lax.dot_general(a.astype(BF16), b.astype(BF16), (((2,), (1,)), ((0,), (0,))),
                               preferred_element_type=F32)

    def bdot_nt(a, b):
        return lax.dot_general(a.astype(BF16), b.astype(BF16), (((2,), (2,)), ((0,), (0,))),
                               preferred_element_type=F32)

    q3, k3, v3, beta3, gc3 = chunked(q_l), chunked(k_l), chunked(v_l), chunked(beta_l), chunked(gc_l)
    gcr3 = jnp.concatenate(gcr_l, axis=0)
    ri = lax.broadcasted_iota(I32, (1, c_, c_), 1)
    ci = lax.broadcasted_iota(I32, (1, c_, c_), 2)
    decay = jnp.exp(jnp.where(ri >= ci, gc3 - gcr3, NEG_INF))
    egc = jnp.exp(gc3)
    kb3 = k3 * beta3
    rhs3 = jnp.concatenate([v3 * beta3, kb3 * egc], axis=2)
    g_last = gc3[:, c_ - 1:c_, :]
    qd3 = q3 * egc
    kd3 = k3 * jnp.exp(g_last - gc3)
    gl3 = jnp.exp(g_last)
    kk = bdot_nt(kb3, k3)
    qk3 = bdot_nt(q3, k3) * decay
    x = -(kk * jnp.where(ri > ci, decay, 0.0))
    r = x
    pw = x
    for _ in range(int(math.log2(c_)) - 1):
        pw = bdot(pw, pw)
        r = r + pw + bdot(r, pw)
    sol = rhs3 + bdot(r, rhs3)

    states = [s_ref[g] for g in range(n_heads)]
    outs = [[] for _ in range(n_heads)]
    for c in range(nc):
        for g in range(n_heads):
            n = g * nc + c
            sb = states[g].astype(BF16)
            v_new = sol[n, :, :B_HEAD_DIM] - _dot(sol[n, :, B_HEAD_DIM:].astype(BF16), sb)
            vnb = v_new.astype(BF16)
            outs[g].append(_dot(qd3[n].astype(BF16), sb) + _dot(qk3[n].astype(BF16), vnb))
            states[g] = states[g] * gl3[n] + _dot_tn(kd3[n].astype(BF16), vnb)
    for g in range(n_heads):
        cols = slice(g * LANES, (g + 1) * LANES)
        s_ref[g] = states[g]
        o = jnp.concatenate(outs[g], axis=0)
        o = o * lax.rsqrt(jnp.mean(o * o, axis=-1, keepdims=True) + RMS_EPS) * nw_ref[...]
        z = z_ref[:, cols].astype(F32)
        o_ref[:, cols] = (o * (z * _sigmoid(z))).astype(o_ref.dtype)


def _gdn(proj, gates, gates_row, conv_w, norm_w, bsz, seq, tb):
    nt = seq // tb
    h_ = B_HEADS
    gh = GDN_HEADS_PER_STEP
    col0 = 3 * A_HEADS
    assert h_ % gh == 0 and col0 % gh == 0
    blk = lambda off: pl.BlockSpec((tb, gh * LANES), lambda b, h, s: (b * nt + s, (col0 + off * h_) // gh + h))
    return pl.pallas_call(
        functools.partial(_gdn_body, tb),
        grid=(bsz, h_ // gh, nt),
        in_specs=[blk(0), blk(1), blk(2), blk(3),
                  pl.BlockSpec((tb, LANES), lambda b, h, s: (b * nt + s, 0)),
                  pl.BlockSpec((gh, 2, tb), lambda b, h, s: (b * (h_ // gh) + h, 0, s)),
                  pl.BlockSpec((gh, 3 * CONV_WIDTH, LANES), lambda b, h, s: (h, 0, 0)),
                  pl.BlockSpec((1, LANES), lambda b, h, s: (0, 0))],
        out_specs=pl.BlockSpec((tb, gh * LANES), lambda b, h, s: (b * nt + s, h)),
        out_shape=jax.ShapeDtypeStruct((bsz * seq, h_ * LANES), BF16),
        scratch_shapes=[pltpu.VMEM((gh, B_HEAD_DIM, B_HEAD_DIM), F32),
                        pltpu.VMEM((gh, 3, SUBLANES, LANES), F32)],
        compiler_params=_cparams(("parallel", "parallel", "arbitrary")),
        name="gdn",
    )(proj, proj, proj, proj, gates, gates_row, conv_w, norm_w)


ROW_TILES = D_MODEL // LANES


def _store_tile_rows(ref, y):
    t = y.shape[0]
    for c in range(ROW_TILES):
        ref[pl.ds(c, t, stride=ROW_TILES), :] = y[:, c * LANES:(c + 1) * LANES]


def _load_tile_rows(ref, t, lead=()):
    return jnp.concatenate([ref[lead + (pl.ds(c, t, stride=ROW_TILES), slice(None))] for c in range(ROW_TILES)],
                           axis=1)


def _outproj_ln_body(a_ref, b_ref, w_ref, x_ref, g_ref, bb_ref, o_ref, o3_ref, wb_ref):
    @pl.when(pl.program_id(0) == 0)
    def _():
        wb_ref[...] = w_ref[...].astype(BF16)

    half = a_ref.shape[1]
    hmix = _dot(a_ref[...], wb_ref[0:half, :]) + _dot(b_ref[...], wb_ref[half:, :])
    out = _layer_norm(DEEPNORM_ALPHA * x_ref[...] + hmix, g_ref[...], bb_ref[...])
    o_ref[...] = out
    _store_tile_rows(o3_ref, out)


def _outproj_ln(a, a_blk, b, b_blk, w, x, g, bb, tm=512):
    n, d = x.shape
    tm = min(tm, n)
    half = d // 2
    row = pl.BlockSpec((1, d), lambda i: (0, 0))
    return pl.pallas_call(
        _outproj_ln_body,
        grid=(n // tm,),
        in_specs=[pl.BlockSpec((tm, half), lambda i: (i, a_blk)),
                  pl.BlockSpec((tm, half), lambda i: (i, b_blk)),
                  pl.BlockSpec((d, d), lambda i: (0, 0)),
                  pl.BlockSpec((tm, d), lambda i: (i, 0)), row, row],
        out_specs=[pl.BlockSpec((tm, d), lambda i: (i, 0)),
                   pl.BlockSpec((tm * ROW_TILES, LANES), lambda i: (i, 0))],
        out_shape=[jax.ShapeDtypeStruct((n, d), F32), jax.ShapeDtypeStruct((n * ROW_TILES, LANES), F32)],
        scratch_shapes=[pltpu.VMEM((d, d), BF16)],
        compiler_params=_cparams(("arbitrary",)),
        name="outproj_ln",
    )(a, b, w, x, g, bb)


def _router_body(tm, x_ref, w_ref, b_ref, idx_ref, gate_ref, rank_ref, cnt_ref, carry_ref):
    @pl.when(pl.program_id(0) == 0)
    def _():
        carry_ref[...] = jnp.zeros_like(carry_ref)

    logits = _dot(x_ref[...], w_ref[...], precision=HIGHEST) + b_ref[...]
    lg = jnp.transpose(logits)[0:N_EXPERTS, :]
    e_iota = lax.broadcasted_iota(I32, (N_EXPERTS, tm), 0).astype(F32)
    vals, idxs, hots = [], [], []
    for _ in range(TOP_K):
        m = jnp.max(lg, axis=0, keepdims=True)
        idx = jnp.min(jnp.where(lg == m, e_iota, float(N_EXPERTS)), axis=0, keepdims=True)
        hot = e_iota == idx
        lg = jnp.where(hot, NEG_INF, lg)
        vals.append(m)
        idxs.append(idx)
        hots.append(hot)
    es = [jnp.exp(v - vals[0]) for v in vals]
    den = es[0] + es[1] + es[2] + es[3]
    sel = jnp.zeros((N_EXPERTS, tm), F32)
    for hot in hots:
        sel = sel + jnp.where(hot, 1.0, 0.0)
    before = jnp.where(lax.broadcasted_iota(I32, (tm, tm), 0) < lax.broadcasted_iota(I32, (tm, tm), 1),
                       1.0, 0.0).astype(BF16)
    cum = _dot(sel.astype(BF16), before) + carry_ref[:, 0:1]
    ranks = [jnp.sum(jnp.where(hot, cum, 0.0), axis=0, keepdims=True) for hot in hots]
    total = carry_ref[...] + jnp.sum(sel, axis=1, keepdims=True)
    carry_ref[...] = total
    idx_ref[...] = jnp.concatenate(idxs, axis=0).astype(I32)
    gate_ref[...] = jnp.concatenate([e / den for e in es], axis=0)
    rank_ref[...] = jnp.concatenate(ranks, axis=0).astype(I32)
    cnt_ref[...] = total


def _router(x, w_pad, b_pad, tm=512):
    n, d = x.shape
    tm = min(tm, n)
    out4 = lambda dt: jax.ShapeDtypeStruct((TOP_K, n), dt)
    blk4 = pl.BlockSpec((TOP_K, tm), lambda i: (0, i))
    return pl.pallas_call(
        functools.partial(_router_body, tm),
        grid=(n // tm,),
        in_specs=[pl.BlockSpec((tm, d), lambda i: (i, 0)),
                  pl.BlockSpec((d, LANES), lambda i: (0, 0)),
                  pl.BlockSpec((1, LANES), lambda i: (0, 0))],
        out_specs=[blk4, blk4, blk4, pl.BlockSpec((N_EXPERTS, LANES), lambda i: (0, 0))],
        out_shape=[out4(I32), out4(F32), out4(I32), jax.ShapeDtypeStruct((N_EXPERTS, LANES), F32)],
        scratch_shapes=[pltpu.VMEM((N_EXPERTS, LANES), F32)],
        compiler_params=_cparams(("arbitrary",)),
        name="moe_router",
    )(x, w_pad, b_pad)


def _row_slab(row):
    return pl.ds(pl.multiple_of(row * ROW_TILES, ROW_TILES), ROW_TILES)


def _start_row_dmas(n_rows, make_copy):
    def issue(r, carry):
        for k in range(TOP_K):
            make_copy(k, r).start(priority=k % 2)
        return carry

    lax.fori_loop(0, n_rows, issue, 0, unroll=4)


def _wait_row_dmas(n_rows, make_copy):
    def drain(r, carry):
        for k in range(TOP_K):
            make_copy(k, r).wait()
        return carry

    lax.fori_loop(0, n_rows, drain, 0, unroll=8)


def _row_dma_loops(n_rows, make_copy):
    _start_row_dmas(n_rows, make_copy)
    _wait_row_dmas(n_rows, make_copy)


def _dispatch_body(td, dest_ref, fill_ref, x3_ref, xg_out, zero_ref, sem, zsem):
    @pl.when(pl.program_id(0) == 0)
    def _():
        zero_ref[...] = jnp.zeros_like(zero_ref)

        def zero_copy(r):
            return pltpu.make_async_copy(zero_ref, xg_out.at[_row_slab(r)], zsem)

        def each_expert(fn):
            def expert(e, carry):
                lax.fori_loop(fill_ref[0, e], fill_ref[1, e], lambda r, c: (fn(r), c)[1], 0)
                return carry
            lax.fori_loop(0, N_EXPERTS, expert, 0)

        each_expert(lambda r: zero_copy(r).start())
        each_expert(lambda r: zero_copy(r).wait())

    _row_dma_loops(td, lambda k, r: pltpu.make_async_copy(
        x3_ref.at[_row_slab(r)], xg_out.at[_row_slab(dest_ref[k, r])], sem))


def _smem_rows(tile):
    return pl.BlockSpec((TOP_K, tile), lambda i: (0, i), memory_space=pltpu.SMEM)


def _dispatch(x3, dest, fill, n_rows, td=512):
    n = x3.shape[0] // ROW_TILES
    td = min(td, n)
    return pl.pallas_call(
        functools.partial(_dispatch_body, td),
        grid=(n // td,),
        in_specs=[_smem_rows(td),
                  pl.BlockSpec(memory_space=pltpu.SMEM),
                  pl.BlockSpec((td * ROW_TILES, LANES), lambda i: (i, 0))],
        out_specs=pl.BlockSpec(memory_space=pl.ANY),
        out_shape=jax.ShapeDtypeStruct((n_rows * ROW_TILES, LANES), x3.dtype),
        scratch_shapes=[pltpu.VMEM((ROW_TILES, LANES), x3.dtype), pltpu.SemaphoreType.DMA(()),
                        pltpu.SemaphoreType.DMA(())],
        compiler_params=_cparams(("arbitrary",)),
        name="moe_dispatch",
    )(dest, fill, x3)


def _expert_body(be_ref, nu_ref, x_ref, wgu_ref, bgu_ref, wd_ref, bd_ref, o_ref, wgu_b, wd_b):
    i = pl.program_id(0)
    prev = be_ref[jnp.maximum(i - 1, 0)]
    fresh = jnp.logical_or(i == 0, be_ref[i] != prev)

    @pl.when(jnp.logical_and(i < nu_ref[0], fresh))
    def _():
        wgu_b[...] = wgu_ref[...].astype(BF16)
        wd_b[...] = wd_ref[...].astype(BF16)

    @pl.when(i < nu_ref[0])
    def _():
        x = _load_tile_rows(x_ref, MOE_ROWS).astype(BF16)
        hcat = _dot(x, wgu_b[...]) + bgu_ref[...]
        g = jnp.minimum(hcat[:, :D_FF], SWIGLU_LIMIT)
        u = jnp.clip(hcat[:, D_FF:], -SWIGLU_LIMIT, SWIGLU_LIMIT)
        act = g * _sigmoid(SWIGLU_ALPHA * g) * (u + 1.0)
        _store_tile_rows(o_ref, _dot(act.astype(BF16), wd_b[...]) + bd_ref[...])

    @pl.when(i >= nu_ref[0])
    def _():
        o_ref[...] = jnp.zeros_like(o_ref)


def _experts(xg, layer, block_expert, n_used, w_gu, b_gu, w_down, b_down):
    d = D_MODEL
    nb = xg.shape[0] // (MOE_ROWS * ROW_TILES)
    blk = pl.BlockSpec((MOE_ROWS * ROW_TILES, LANES), lambda i, be, nu: (i, 0))
    grid_spec = pltpu.PrefetchScalarGridSpec(
        num_scalar_prefetch=2,
        grid=(nb,),
        in_specs=[
            pl.BlockSpec((MOE_ROWS * ROW_TILES, LANES), lambda i, be, nu: (jnp.minimum(i, nu[0] - 1), 0)),
            pl.BlockSpec((None, None, d, 2 * D_FF), lambda i, be, nu: (layer, be[i], 0, 0)),
            pl.BlockSpec((None, None, 1, 2 * D_FF), lambda i, be, nu: (layer, be[i], 0, 0)),
            pl.BlockSpec((None, None, D_FF, d), lambda i, be, nu: (layer, be[i], 0, 0)),
            pl.BlockSpec((None, None, 1, d), lambda i, be, nu: (layer, be[i], 0, 0)),
        ],
        out_specs=blk,
        scratch_shapes=[pltpu.VMEM((d, 2 * D_FF), BF16), pltpu.VMEM((D_FF, d), BF16)],
    )
    return pl.pallas_call(
        _expert_body,
        grid_spec=grid_spec,
        out_shape=jax.ShapeDtypeStruct(xg.shape, F32),
        compiler_params=_cparams(("arbitrary",)),
        name="moe_experts",
    )(block_expert, n_used, xg, w_gu, b_gu[:, :, None, :], w_down, b_down[:, :, None, :])


def _combine_ln_body(tc, dest_ref, dest_next_ref, yg_hbm, gate_ref, x_ref, g_ref, b_ref, o_ref, ybuf, sems):
    i = pl.program_id(0)
    slot = i % 2

    def gather(rows_ref, s):
        return lambda k, r: pltpu.make_async_copy(
            yg_hbm.at[_row_slab(rows_ref[k, r])], ybuf.at[s, k, _row_slab(r)], sems.at[s])

    @pl.when(i == 0)
    def _():
        _start_row_dmas(tc, gather(dest_ref, slot))

    @pl.when(i + 1 < pl.num_programs(0))
    def _():
        _start_row_dmas(tc, gather(dest_next_ref, 1 - slot))

    _wait_row_dmas(tc, gather(dest_ref, slot))
    gate = gate_ref[...]
    hmoe = gate[:, 0:1] * _load_tile_rows(ybuf, tc, (slot, 0))
    for k in range(1, TOP_K):
        hmoe = hmoe + gate[:, k:k + 1] * _load_tile_rows(ybuf, tc, (slot, k))
    o_ref[...] = _layer_norm(DEEPNORM_ALPHA * x_ref[...] + hmoe, g_ref[...], b_ref[...])


def _combine_ln(yg, dest, gate_t, x, g, b, tc=256):
    n, d = x.shape
    tc = min(tc, n)
    steps = n // tc
    row = pl.BlockSpec((1, d), lambda i: (0, 0))
    return pl.pallas_call(
        functools.partial(_combine_ln_body, tc),
        grid=(steps,),
        in_specs=[_smem_rows(tc),
                  pl.BlockSpec((TOP_K, tc), lambda i: (0, jnp.minimum(i + 1, steps - 1)), memory_space=pltpu.SMEM),
                  pl.BlockSpec(memory_space=pl.ANY),
                  pl.BlockSpec((tc, TOP_K), lambda i: (i, 0)),
                  pl.BlockSpec((tc, d), lambda i: (i, 0)), row, row],
        out_specs=pl.BlockSpec((tc, d), lambda i: (i, 0)),
        out_shape=jax.ShapeDtypeStruct((n, d), F32),
        scratch_shapes=[pltpu.VMEM((2, TOP_K, tc * ROW_TILES, LANES), F32), pltpu.SemaphoreType.DMA((2,))],
        compiler_params=_cparams(("arbitrary",)),
        name="moe_combine_ln",
    )(dest, dest, yg, gate_t, x, g, b)


def _moe_ln(x, x3, layer, router_w, router_b, w_gu, b_gu, w_down, b_down, ln_g, ln_b):
    n, d = x.shape
    idx, gate, rank, cnt = _router(x, _pad_lanes(router_w), _pad_lanes(router_b[None, :]))
    counts = cnt[:, 0].astype(I32)
    padded = (counts + MOE_ROWS - 1) // MOE_ROWS * MOE_ROWS
    pad_end = jnp.cumsum(padded)
    pad_start = pad_end - padded
    hot = idx[:, :, None] == jnp.arange(N_EXPERTS, dtype=I32)[None, None, :]
    dest = jnp.sum(jnp.where(hot, pad_start[None, None, :], 0), axis=-1) + rank
    n_blocks = -(-(n * TOP_K) // MOE_ROWS) + N_EXPERTS
    block_row0 = jnp.arange(n_blocks, dtype=I32) * MOE_ROWS
    block_expert = jnp.minimum(jnp.sum((pad_end[None, :] <= block_row0[:, None]).astype(I32), axis=1),
                               N_EXPERTS - 1)
    n_used = (pad_end[-1:] // MOE_ROWS).astype(I32)
    fill = jnp.stack([pad_start + counts, pad_end])
    xg = _dispatch(x3, dest, fill, n_blocks * MOE_ROWS)
    yg = _experts(xg, layer, block_expert, n_used, w_gu, b_gu, w_down, b_down)
    return _combine_ln(yg, dest, gate.T, x, ln_g[None, :], ln_b[None, :])


def _even_mixer_ln(x, bsz, seq, w_in, w_out, lam_params, subln_w, conv_w, a_log, dt_bias, gdn_norm_w,
                   t5_bias, lam_init, ln_g, ln_b):
    main = 3 * A_HEADS * LANES + 4 * B_HEADS * B_HEAD_DIM
    n_q = A_HEADS * 2 * A_HEAD_DIM
    col_scale = jnp.concatenate([jnp.full((1, n_q), A_HEAD_DIM ** -0.5 * LOG2E, F32),
                                 jnp.ones((1, main - n_q), F32)], axis=1)
    proj = _proj(x, w_in[:, :main], col_scale)
    zeros4 = jnp.zeros((B_HEADS,), F32)
    p0 = _pad_lanes(jnp.concatenate([zeros4, a_log.astype(F32)])[None, :])
    p1 = _pad_lanes(jnp.concatenate([zeros4, dt_bias.astype(F32)])[None, :])
    gates = _gates("even", x, _pad_lanes(w_in[:, main:]), p0, p1, seq)
    t = min(ATTN_TILE, seq)
    strips = _t5_strips(t5_bias, seq, t)
    ao = _diff_attention(proj, bsz, seq, strips, lam_params.astype(F32), subln_w[None, :].astype(F32),
                         lam_init, t)
    g8 = gates[:, :2 * B_HEADS].reshape(bsz, seq, 2, B_HEADS)
    gates_row = g8.transpose(0, 3, 2, 1).reshape(bsz * B_HEADS, 2, seq)
    cw = conv_w.astype(F32).reshape(CONV_WIDTH, 3, B_HEADS, B_HEAD_DIM).transpose(2, 1, 0, 3)
    cw = cw.reshape(B_HEADS, 3 * CONV_WIDTH, B_HEAD_DIM)
    bo = _gdn(proj, gates, gates_row, cw, gdn_norm_w[None, :].astype(F32), bsz, seq, min(GDN_TILE, seq))
    return _outproj_ln(ao, 0, bo, 0, w_out, x, ln_g[None, :], ln_b[None, :])


def _odd_mixer_ln(x, bsz, seq, w_in, w_out, qk_norm_w, forget_b, ln_g, ln_b):
    main = 4 * C_HEADS * C_HEAD_DIM
    width = C_HEADS * C_HEAD_DIM
    scale = jnp.asarray([C_HEAD_DIM ** -0.5 * LOG2E, 1.0], F32)[:, None]
    wqk = qk_norm_w.astype(F32) * scale
    col_scale = jnp.concatenate([jnp.tile(wqk[0], C_HEADS), jnp.tile(wqk[1], C_HEADS),
                                 jnp.ones((main - 2 * width,), F32)])[None, :]
    proj = _proj(x, w_in[:, :main], col_scale, norm_chunks=range(2 * width // PROJ_CHUNK))
    fb =_pad_lanes(forget_b.astype(F32)[None, :])
    cum = _gates("odd", x, _pad_lanes(w_in[:, main:]), fb, fb, seq)
    cum_t = cum[:, :C_HEADS].reshape(bsz, seq, C_HEADS // 2, 2).transpose(0, 2, 3, 1)
    logit_bound = 1.02 * C_HEAD_DIM * jnp.max(jnp.abs(wqk[0])) * jnp.max(jnp.abs(wqk[1]))
    o = _fox_attention(proj, proj, cum_t, logit_bound, bsz, seq, min(ATTN_TILE, seq))
    return _outproj_ln(o, 0, o, 1, w_out, x, ln_g[None, :], ln_b[None, :])


def kernel(x, t5_bias, even_w_in, even_w_out, diff_lambda, diff_subln_w, gdn_conv_w, gdn_a_log, gdn_dt_bias, gdn_norm_w, odd_w_in, odd_w_out, fox_qk_norm_w, fox_forget_b, router_w, router_b, moe_w_gate_up, moe_b_gate_up, moe_w_down, moe_b_down, ln_mix_g, ln_mix_b, ln_ffn_g, ln_ffn_b):
    bsz, seq, d = x.shape
    xf = x.reshape(bsz * seq, d)
    for layer in range(DEPTH):
        i = layer // 2
        if layer % 2 == 0:
            lam_init = 0.8 - 0.6 * math.exp(-0.3 * layer)
            xf, x3 = _even_mixer_ln(xf, bsz, seq, even_w_in[i], even_w_out[i], diff_lambda[i], diff_subln_w[i],
                                    gdn_conv_w[i], gdn_a_log[i], gdn_dt_bias[i], gdn_norm_w[i], t5_bias, lam_init,
                                    ln_mix_g[layer], ln_mix_b[layer])
        else:
            xf, x3 = _odd_mixer_ln(xf, bsz, seq, odd_w_in[i], odd_w_out[i], fox_qk_norm_w[i], fox_forget_b[i],
                                   ln_mix_g[layer], ln_mix_b[layer])
        xf = _moe_ln(xf, x3, layer, router_w[layer], router_b[layer], moe_w_gate_up, moe_b_gate_up,
                     moe_w_down, moe_b_down, ln_ffn_g[layer], ln_ffn_b[layer])
    return xf.reshape(bsz, seq, d)
```

```python
import functools
import math

import numpy as np
import jax
import jax.numpy as jnp
from jax import lax
from jax.experimental import pallas as pl
from jax.experimental.pallas import tpu as pltpu

F32 = jnp.float32
BF16 = jnp.bfloat16
I32 = jnp.int32
HIGHEST = lax.Precision.HIGHEST

D_MODEL = 1024
DEPTH = 4
A_HEADS = 4
A_HEAD_DIM = 64
B_HEADS = 4
B_HEAD_DIM = 128
CONV_WIDTH = 4
C_HEADS = 16
C_HEAD_DIM = 64
T5_BUCKETS = 32
T5_MAX_DISTANCE = 2048
N_EXPERTS = 32
TOP_K = 4
D_FF = D_MODEL
SWIGLU_LIMIT = 7.0
SWIGLU_ALPHA = 1.702
DEEPNORM_ALPHA = (2 * DEPTH) ** 0.25
LN_EPS = 1e-5
RMS_EPS = 1e-6

LANES = 128
SUBLANES = 8
VMEM_LIMIT = 56 * 1024 * 1024

ATTN_TILE = 1024
ATTN_ROWS = 1024
GDN_CHUNK = 128
GDN_TILE = 512
GDN_HEADS_PER_STEP = 4
MOE_ROWS = 512
FOX_SKIP_MARGIN = 170.0
NEG_INF = float("-inf")
LOG2E = math.log2(math.e)


def _cparams(sem, vmem=VMEM_LIMIT):
    return pltpu.CompilerParams(dimension_semantics=sem, vmem_limit_bytes=vmem)


def _dot(a, b, **kw):
    return jnp.dot(a, b, preferred_element_type=F32, **kw)


def _dot_nt(a, b, **kw):
    return lax.dot_general(a, b, (((1,), (1,)), ((), ())), preferred_element_type=F32, **kw)


def _dot_tn(a, b, **kw):
    return lax.dot_general(a, b, (((0,), (0,)), ((), ())), preferred_element_type=F32, **kw)


def _sigmoid(x):
    return 1.0 / (1.0 + jnp.exp(-x))


def _softplus(x):
    return jnp.maximum(x, 0.0) + jnp.log(1.0 + jnp.exp(-jnp.abs(x)))


def _layer_norm(xf, g, b):
    mu = jnp.mean(xf, axis=-1, keepdims=True)
    xc = xf - mu
    var = jnp.mean(xc * xc, axis=-1, keepdims=True)
    return xc * lax.rsqrt(var + LN_EPS) * g + b


PROJ_CHUNK = 512


def _proj_body(norm_chunks, x_ref, w_ref, cs_ref, gsum_ref, o_ref):
    xb = x_ref[...].astype(BF16)
    for c in range(w_ref.shape[1] // PROJ_CHUNK):
        cols = slice(c * PROJ_CHUNK, (c + 1) * PROJ_CHUNK)
        acc = _dot(xb, w_ref[:, cols])
        if c in norm_chunks:
            ss = _dot((acc * acc).astype(BF16), gsum_ref[...])
            acc = acc * lax.rsqrt(ss * (1.0 / C_HEAD_DIM) + RMS_EPS)
        o_ref[:, cols] = (acc * cs_ref[:, cols]).astype(o_ref.dtype)


def _proj(x, w, col_scale, norm_chunks=(), tm=512):
    n, k = x.shape
    m = w.shape[1]
    tm = min(tm, n)
    head = jnp.arange(PROJ_CHUNK, dtype=I32) // C_HEAD_DIM
    gsum = (head[:, None] == head[None, :]).astype(BF16)
    return pl.pallas_call(
        functools.partial(_proj_body, tuple(norm_chunks)),
        grid=(n // tm,),
        in_specs=[pl.BlockSpec((tm, k), lambda i: (i, 0)),
                  pl.BlockSpec((k, m), lambda i: (0, 0)),
                  pl.BlockSpec((1, m), lambda i: (0, 0)),
                  pl.BlockSpec((PROJ_CHUNK, PROJ_CHUNK), lambda i: (0, 0))],
        out_specs=pl.BlockSpec((tm, m), lambda i: (i, 0)),
        out_shape=jax.ShapeDtypeStruct((n, m), BF16),
        compiler_params=_cparams(("parallel",)),
        name="proj",
    )(x, w.astype(BF16), col_scale, gsum)


def _split_bf16(a, parts):
    out = []
    for _ in range(parts):
        hi = a.astype(BF16)
        out.append(hi)
        a = a - hi.astype(F32)
    return out


def _tri_sum(tri, a):
    tri = tri.astype(BF16)
    return sum(_dot(tri, part) for part in _split_bf16(a, 3))


def _gates_body(mode, tm, steps_per_seq, x_ref, w_ref, p0_ref, p1_ref, o_ref, carry_ref):
    xh, xl = _split_bf16(x_ref[...], 2)
    wh, wl = _split_bf16(w_ref[...], 2)
    z = _dot(xh, wh) + (_dot(xl, wh) + _dot(xh, wl))
    ri = lax.broadcasted_iota(I32, (tm, tm), 0)
    ci = lax.broadcasted_iota(I32, (tm, tm), 1)
    if mode == "even":
        lane = lax.broadcasted_iota(I32, (tm, LANES), 1)
        beta = _sigmoid(z)
        g = -jnp.exp(p0_ref[...]) * _softplus(z + p1_ref[...])
        shift = int(math.log2(GDN_CHUNK))
        same_chunk = jnp.right_shift(ri, shift) == jnp.right_shift(ci, shift)
        tri = jnp.where(jnp.logical_and(same_chunk, ci <= ri), 1.0, 0.0)
        gc = _tri_sum(tri, g)
        o_ref[...] = jnp.where(lane < B_HEADS, beta, gc)
    else:
        @pl.when(pl.program_id(0) % steps_per_seq == 0)
        def _():
            carry_ref[...] = jnp.zeros_like(carry_ref)

        logf = -_softplus(-(z + p0_ref[...]))
        tri = jnp.where(ci <= ri, 1.0, 0.0)
        cum = _tri_sum(tri, logf) + carry_ref[0:1, :]
        o_ref[...] = cum
        carry_ref[...] = jnp.broadcast_to(cum[tm - 1:tm, :], carry_ref.shape)


def _gates(mode, x, w_small, p0, p1, seq_len, tm=512):
    n, k = x.shape
    tm = min(tm, seq_len)
    row = pl.BlockSpec((1, LANES), lambda i: (0, 0))
    return pl.pallas_call(
        functools.partial(_gates_body, mode, tm, seq_len // tm),
        grid=(n // tm,),
        in_specs=[pl.BlockSpec((tm, k), lambda i: (i, 0)),
                  pl.BlockSpec((k, LANES), lambda i: (0, 0)), row, row],
        out_specs=pl.BlockSpec((tm, LANES), lambda i: (i, 0)),
        out_shape=jax.ShapeDtypeStruct((n, LANES), F32),
        scratch_shapes=[pltpu.VMEM((SUBLANES, LANES), F32)],
        compiler_params=_cparams(("arbitrary",)),
        name="gates_" + mode,
    )(x, w_small, p0, p1)


def _pad_lanes(a):
    return jnp.pad(a, [(0, 0)] * (a.ndim - 1) + [(0, LANES - a.shape[-1])])


def _attn_body(mode, t, lam_init, *rest):
    if mode == "diff":
        q_ref, k_ref, v_ref, strip_ref, lam_ref, subln_ref, o_ref, qs_ref, m_ref, acc_ref, l_ref, tile_ref = rest
    else:
        jmin_ref, q_ref, k_ref, v_ref, ck_ref, gate_ref, o_ref, qs_ref, m_ref, acc_ref = rest
    i = pl.program_id(2)
    rc = min(ATTN_ROWS, t)

    q = q_ref[...]
    lane = lax.broadcasted_iota(I32, q.shape, 1)
    zero = jnp.zeros_like(q)
    qs_ref[0:t, :] = jnp.where(lane < 64, q, zero)
    qs_ref[t:2 * t, :] = jnp.where(lane < 64, zero, q)
    m_ref[...] = jnp.full(m_ref.shape, NEG_INF, F32)
    acc_ref[...] = jnp.zeros_like(acc_ref)
    if mode == "diff":
        l_ref[...] = jnp.zeros_like(l_ref)

        @pl.when(i == 0)
        def _():
            for d in range(strip_ref.shape[0]):
                x = jnp.broadcast_to(strip_ref[d], (t, 2 * t))
                tile_ref[d] = pltpu.roll(x, t + 1, 1, stride=1, stride_axis=0)[:, :t]
    else:
        c0 = ck_ref[:, pl.ds(pl.multiple_of(i * t, t), LANES)][:, 0:1]

    def step(kind, j, d=0):
        keys = pl.ds(pl.multiple_of(j * t, t), t)
        k = k_ref[keys, :]
        v = v_ref[keys, :]
        if mode == "fox":
            lane = lax.broadcasted_iota(I32, v.shape, 1)
            one = jnp.ones_like(v)
            rhs = (jnp.where(lane < 64, v, one), jnp.where(lane < 64, one, v))
            bias = (c0 - ck_ref[:, keys]) * LOG2E
        if kind == "diag" and t >= 2 * LANES:
            chunks = [(half, h0, t // 2, h0 + t // 2) for half in range(2) for h0 in (0, t // 2)]
        else:
            chunks = [(r0 // t, r0 % t, rc, t) for r0 in range(0, 2 * t, rc)]
        for half, h0, nr, nk in chunks:
            rows = slice(half * t + h0, half * t + h0 + nr)
            s = _dot_nt(qs_ref[rows, :], k[:nk])
            if mode == "fox":
                s = s + bias[half:half + 1, :nk]
            elif kind != "far":
                s = s + tile_ref[d, h0:h0 + nr, :nk]
            if kind == "diag":
                rr = lax.broadcasted_iota(I32, (nr, nk), 0) + h0
                cc = lax.broadcasted_iota(I32, (nr, nk), 1)
                s = jnp.where(rr >= cc, s, NEG_INF)
            m_prev = m_ref[rows, :]
            m_new = jnp.maximum(m_prev, jnp.max(s, axis=1, keepdims=True))
            alpha = jnp.exp2(m_prev - m_new)
            pr = jnp.exp2(s - pltpu.repeat(m_new, nk // LANES, 1))
            if mode == "diff":
                l_ref[rows, :] = alpha * l_ref[rows, :] + jnp.sum(pr, axis=1, keepdims=True)
                pv = _dot(pr.astype(BF16), v[:nk])
            else:
                pv = _dot(pr.astype(BF16), rhs[half][:nk])
            acc_ref[rows, :] = alpha * acc_ref[rows, :] + pv
            m_ref[rows, :] = m_new

    def each_block(kind, lo, hi):
        lax.fori_loop(lo, hi, lambda j, carry: (step(kind, j), carry)[1], 0)

    if mode == "diff":
        n_near = strip_ref.shape[0] - 1
        each_block("far", 0, jnp.maximum(i - n_near, 0))
        for d in range(n_near, 0, -1):
            pl.when(i >= d)(functools.partial(step, "near", i - d, d))
    else:
        row = (pl.program_id(0) * pl.num_programs(1) + pl.program_id(1)) * pl.num_programs(2)
        each_block("off", jmin_ref[row + i], i)
    step("diag", i)

    acc = acc_ref[...]
    if mode == "diff":
        o = acc * (1.0 / l_ref[...])
        lp = lam_ref[...]
        lam = (jnp.exp(jnp.sum(lp[0:1] * lp[1:2], axis=-1, keepdims=True))
               - jnp.exp(jnp.sum(lp[2:3] * lp[3:4], axis=-1, keepdims=True)) + lam_init)
        dlt = o[0:t] - lam * o[t:2 * t]
        ms = jnp.mean(dlt * dlt, axis=-1, keepdims=True)
        out = dlt * lax.rsqrt(ms + RMS_EPS) * subln_ref[...] * (1.0 - lam_init)
    else:
        lane = lax.broadcasted_iota(I32, (t, LANES), 1)
        lo = acc[0:t]
        hi = acc[t:2 * t]
        out = jnp.where(lane < 64, lo * (1.0 / lo[:, 64:65]), hi * (1.0 / hi[:, 0:1]))
        out = out * _sigmoid(gate_ref[...].astype(F32))
    o_ref[...] = out.astype(o_ref.dtype)


def _attn_scratch(mode, t, n_tiles=0):
    base = [pltpu.VMEM((2 * t, LANES), BF16), pltpu.VMEM((2 * t, LANES), F32), pltpu.VMEM((2 * t, LANES), F32)]
    if mode == "diff":
        base += [pltpu.VMEM((2 * t, LANES), F32), pltpu.VMEM((n_tiles, t, t), F32)]
    return base


def _diff_attention(proj, bsz, seq, strips, lam_params, subln_w, lam_init, t):
    nq = seq // t
    nd = strips.shape[1]
    h_ = A_HEADS
    return pl.pallas_call(
        functools.partial(_attn_body, "diff", t, lam_init),
        grid=(bsz, h_, nq),
        in_specs=[
            pl.BlockSpec((t, LANES), lambda b, h, i: (b * nq + i, h)),
            pl.BlockSpec((seq, LANES), lambda b, h, i: (b, h_ + h)),
            pl.BlockSpec((seq, LANES), lambda b, h, i: (b, 2 * h_ + h)),
            pl.BlockSpec((None, nd, 1, 2 * t), lambda b, h, i: (h, 0, 0, 0)),
            pl.BlockSpec(lam_params.shape, lambda b, h, i: (0, 0)),
            pl.BlockSpec((1, LANES), lambda b, h, i: (0, 0)),
        ],
        out_specs=pl.BlockSpec((t, LANES), lambda b, h, i: (b * nq + i, h)),
        out_shape=jax.ShapeDtypeStruct((bsz * seq, h_ * LANES), BF16),
        scratch_shapes=_attn_scratch("diff", t, nd),
        compiler_params=_cparams(("parallel", "parallel", "arbitrary")),
        name="diff_attn",
    )(proj, proj, proj, strips, lam_params, subln_w)


def _fox_first_live_block(cum_t, logit_bound, t):
    bsz, hp, _, seq = cum_t.shape
    nq = seq // t
    c_start = cum_t[..., 0::t]
    c_end = cum_t[..., t - 1::t]
    gap = jnp.max(c_start[..., :, None] - c_end[..., None, :], axis=2) * LOG2E
    dead = gap < -(FOX_SKIP_MARGIN + 2.0 * logit_bound)
    dead = jnp.logical_and(dead, jnp.arange(nq)[None, :] < jnp.arange(nq)[:, None])
    return jnp.sum(jnp.cumprod(dead.astype(I32), axis=-1), axis=-1).reshape(-1)


def _fox_attention(qk, proj, cum_t, logit_bound, bsz, seq, t):
    nq = seq // t
    hp = C_HEADS // 2
    jmin = _fox_first_live_block(cum_t, logit_bound, t)
    grid_spec = pltpu.PrefetchScalarGridSpec(
        num_scalar_prefetch=1,
        grid=(bsz, hp, nq),
        in_specs=[
            pl.BlockSpec((t, LANES), lambda b, h, i, jm: (b * nq + i, h)),
            pl.BlockSpec((seq, LANES), lambda b, h, i, jm: (b, hp + h)),
            pl.BlockSpec((seq, LANES), lambda b, h, i, jm: (b, 2 * hp + h)),
            pl.BlockSpec((None, None, 2, seq), lambda b, h, i, jm: (b, h, 0, 0)),
            pl.BlockSpec((t, LANES), lambda b, h, i, jm: (b * nq + i, 3 * hp + h)),
        ],
        out_specs=pl.BlockSpec((t, LANES), lambda b, h, i, jm: (b * nq + i, h)),
        scratch_shapes=_attn_scratch("fox", t),
    )
    return pl.pallas_call(
        functools.partial(_attn_body, "fox", t, 0.0),
        grid_spec=grid_spec,
        out_shape=jax.ShapeDtypeStruct((bsz * seq, hp * LANES), BF16),
        compiler_params=_cparams(("parallel", "parallel", "arbitrary")),
        name="fox_attn",
    )(jmin, qk, qk, proj, cum_t, proj)


def _t5_bias_by_distance(t5_bias, seq):
    n = jnp.arange(seq, dtype=I32)
    max_exact = T5_BUCKETS // 2
    nf = jnp.maximum(n, 1).astype(F32)
    large = max_exact + (jnp.log(nf / max_exact) / math.log(T5_MAX_DISTANCE / max_exact)
                         * (T5_BUCKETS - max_exact)).astype(I32)
    large = jnp.minimum(large, T5_BUCKETS - 1)
    bucket = jnp.where(n < max_exact, n, large)
    return t5_bias.astype(F32).T[:, bucket]


def _t5_strips(t5_bias, seq, t):
    last = t5_bias.astype(F32).T[:, T5_BUCKETS - 1:T5_BUCKETS]
    vec = (_t5_bias_by_distance(t5_bias, seq) - last) * LOG2E
    nd = 1
    while nd * t - (t - 1) < T5_MAX_DISTANCE and nd < seq // t:
        nd += 1
    d = np.arange(nd)[:, None]
    c = np.arange(2 * t)[None, :]
    rel = d * t + t - 1 - c
    rel = np.clip(rel, 0, seq - 1)
    strips = vec[:, jnp.asarray(rel.astype(np.int32))]
    return strips[:, :, None, :]


def _gdn_body(tb, xq_ref, xk_ref, xv_ref, z_ref, gcol_ref, grow_ref, cw_ref, nw_ref, o_ref, s_ref, carry_ref):
    @pl.when(pl.program_id(2) == 0)
    def _():
        s_ref[...] = jnp.zeros_like(s_ref)
        carry_ref[...] = jnp.zeros_like(carry_ref)

    n_heads = GDN_HEADS_PER_STEP
    c_ = GDN_CHUNK
    nc = tb // c_
    gates = gcol_ref[...]
    lane = lax.broadcasted_iota(I32, gates.shape, 1)

    def conv_silu(g, idx, x_ref):
        x = x_ref[:, g * LANES:(g + 1) * LANES].astype(F32)
        xe = jnp.concatenate([carry_ref[g, idx], x], axis=0)
        taps = cw_ref[g, 4 * idx:4 * idx + 4, :]
        y = taps[3:4] * x
        for tap in range(CONV_WIDTH - 1):
            y = y + taps[tap:tap + 1] * pltpu.roll(xe, CONV_WIDTH - 1 - tap, 0)[SUBLANES:]
        carry_ref[g, idx] = x[tb - SUBLANES:]
        return y * _sigmoid(y)

    q_l, k_l, v_l, beta_l, gc_l, gcr_l = [], [], [], [], [], []
    for g in range(n_heads):
        h = pl.program_id(1) * n_heads + g
        q = conv_silu(g, 0, xq_ref)
        k = conv_silu(g, 1, xk_ref)
        v_l.append(conv_silu(g, 2, xv_ref))
        q_l.append(q * lax.rsqrt(jnp.sum(q * q, axis=-1, keepdims=True) + RMS_EPS) * (B_HEAD_DIM ** -0.5))
        k_l.append(k * lax.rsqrt(jnp.sum(k * k, axis=-1, keepdims=True) + RMS_EPS))
        beta_l.append(jnp.sum(jnp.where(lane == h, gates, 0.0), axis=1, keepdims=True))
        gc_l.append(jnp.sum(jnp.where(lane == B_HEADS + h, gates, 0.0), axis=1, keepdims=True))
        gc_row = grow_ref[g, 1:2, :]
        gcr_l.extend(gc_row[None, :, c * c_:(c + 1) * c_] for c in range(nc))

    def chunked(xs):
        return jnp.concatenate([x.reshape(nc, c_, x.shape[-1]) for x in xs], axis=0)

    def bdot(a, b):
        return lax.dot_general(a.astype(BF16), b.astype(BF16), (((2,), (1,)), ((0,), (0,))),
                               preferred_element_type=F32)

    def bdot_nt(a, b):
        return lax.dot_general(a.astype(BF16), b.astype(BF16), (((2,), (2,)), ((0,), (0,))),
                               preferred_element_type=F32)

    q3, k3, v3, beta3, gc3 = chunked(q_l), chunked(k_l), chunked(v_l), chunked(beta_l), chunked(gc_l)
    gcr3 = jnp.concatenate(gcr_l, axis=0)
    ri = lax.broadcasted_iota(I32, (1, c_, c_), 1)
    ci = lax.broadcasted_iota(I32, (1, c_, c_), 2)
    decay = jnp.exp(jnp.where(ri >= ci, gc3 - gcr3, NEG_INF))
    egc = jnp.exp(gc3)
    kb3 = k3 * beta3
    rhs3 = jnp.concatenate([v3 * beta3, kb3 * egc], axis=2)
    g_last = gc3[:, c_ - 1:c_, :]
    qd3 = q3 * egc
    kd3 = k3 * jnp.exp(g_last - gc3)
    gl3 = jnp.exp(g_last)
    kk = bdot_nt(kb3, k3)
    qk3 = bdot_nt(q3, k3) * decay
    x = -(kk * jnp.where(ri > ci, decay, 0.0))
    r = x
    pw = x
    for _ in range(int(math.log2(c_)) - 1):
        pw = bdot(pw, pw)
        r = r + pw + bdot(r, pw)
    sol = rhs3 + bdot(r, rhs3)

    states = [s_ref[g] for g in range(n_heads)]
    outs = [[] for _ in range(n_heads)]
    for c in range(nc):
        for g in range(n_heads):
            n = g * nc + c
            sb = states[g].astype(BF16)
            v_new = sol[n, :, :B_HEAD_DIM] - _dot(sol[n, :, B_HEAD_DIM:].astype(BF16), sb)
            vnb = v_new.astype(BF16)
            outs[g].append(_dot(qd3[n].astype(BF16), sb) + _dot(qk3[n].astype(BF16), vnb))
            states[g] = states[g] * gl3[n] + _dot_tn(kd3[n].astype(BF16), vnb)
    for g in range(n_heads):
        cols = slice(g * LANES, (g + 1) * LANES)
        s_ref[g] = states[g]
        o = jnp.concatenate(outs[g], axis=0)
        o = o * lax.rsqrt(jnp.mean(o * o, axis=-1, keepdims=True) + RMS_EPS) * nw_ref[...]
        z = z_ref[:, cols].astype(F32)
        o_ref[:, cols] = (o * (z * _sigmoid(z))).astype(o_ref.dtype)


def _gdn(proj, gates, gates_row, conv_w, norm_w, bsz, seq, tb):
    nt = seq // tb
    h_ = B_HEADS
    gh = GDN_HEADS_PER_STEP
    col0 = 3 * A_HEADS
    assert h_ % gh == 0 and col0 % gh == 0
    blk = lambda off: pl.BlockSpec((tb, gh * LANES), lambda b, h, s: (b * nt + s, (col0 + off * h_) // gh + h))
    return pl.pallas_call(
        functools.partial(_gdn_body, tb),
        grid=(bsz, h_ // gh, nt),
        in_specs=[blk(0), blk(1), blk(2), blk(3),
                  pl.BlockSpec((tb, LANES), lambda b, h, s: (b * nt + s, 0)),
                  pl.BlockSpec((gh, 2, tb), lambda b, h, s: (b * (h_ // gh) + h, 0, s)),
                  pl.BlockSpec((gh, 3 * CONV_WIDTH, LANES), lambda b, h, s: (h, 0, 0)),
                  pl.BlockSpec((1, LANES), lambda b, h, s: (0, 0))],
        out_specs=pl.BlockSpec((tb, gh * LANES), lambda b, h, s: (b * nt + s, h)),
        out_shape=jax.ShapeDtypeStruct((bsz * seq, h_ * LANES), BF16),
        scratch_shapes=[pltpu.VMEM((gh, B_HEAD_DIM, B_HEAD_DIM), F32),
                        pltpu.VMEM((gh, 3, SUBLANES, LANES), F32)],
        compiler_params=_cparams(("parallel", "parallel", "arbitrary")),
        name="gdn",
    )(proj, proj, proj, proj, gates, gates_row, conv_w, norm_w)


ROW_TILES = D_MODEL // LANES


def _store_tile_rows(ref, y):
    t = y.shape[0]
    for c in range(ROW_TILES):
        ref[pl.ds(c, t, stride=ROW_TILES), :] = y[:, c * LANES:(c + 1) * LANES]


def _load_tile_rows(ref, t, lead=()):
    return jnp.concatenate([ref[lead + (pl.ds(c, t, stride=ROW_TILES), slice(None))] for c in range(ROW_TILES)],
                           axis=1)


def _outproj_ln_body(a_ref, b_ref, w_ref, x_ref, g_ref, bb_ref, o_ref, o3_ref, wb_ref):
    @pl.when(pl.program_id(0) == 0)
    def _():
        wb_ref[...] = w_ref[...].astype(BF16)

    half = a_ref.shape[1]
    hmix = _dot(a_ref[...], wb_ref[0:half, :]) + _dot(b_ref[...], wb_ref[half:, :])
    out = _layer_norm(DEEPNORM_ALPHA * x_ref[...] + hmix, g_ref[...], bb_ref[...])
    o_ref[...] = out
    _store_tile_rows(o3_ref, out)


def _outproj_ln(a, a_blk, b, b_blk, w, x, g, bb, tm=512):
    n, d = x.shape
    tm = min(tm, n)
    half = d // 2
    row = pl.BlockSpec((1, d), lambda i: (0, 0))
    return pl.pallas_call(
        _outproj_ln_body,
        grid=(n // tm,),
        in_specs=[pl.BlockSpec((tm, half), lambda i: (i, a_blk)),
                  pl.BlockSpec((tm, half), lambda i: (i, b_blk)),
                  pl.BlockSpec((d, d), lambda i: (0, 0)),
                  pl.BlockSpec((tm, d), lambda i: (i, 0)), row, row],
        out_specs=[pl.BlockSpec((tm, d), lambda i: (i, 0)),
                   pl.BlockSpec((tm * ROW_TILES, LANES), lambda i: (i, 0))],
        out_shape=[jax.ShapeDtypeStruct((n, d), F32), jax.ShapeDtypeStruct((n * ROW_TILES, LANES), F32)],
        scratch_shapes=[pltpu.VMEM((d, d), BF16)],
        compiler_params=_cparams(("arbitrary",)),
        name="outproj_ln",
    )(a, b, w, x, g, bb)


def _router_body(tm, x_ref, w_ref, b_ref, idx_ref, gate_ref, rank_ref, cnt_ref, carry_ref):
    @pl.when(pl.program_id(0) == 0)
    def _():
        carry_ref[...] = jnp.zeros_like(carry_ref)

    logits = _dot(x_ref[...], w_ref[...], precision=HIGHEST) + b_ref[...]
    lg = jnp.transpose(logits)[0:N_EXPERTS, :]
    e_iota = lax.broadcasted_iota(I32, (N_EXPERTS, tm), 0).astype(F32)
    vals, idxs, hots = [], [], []
    for _ in range(TOP_K):
        m = jnp.max(lg, axis=0, keepdims=True)
        idx = jnp.min(jnp.where(lg == m, e_iota, float(N_EXPERTS)), axis=0, keepdims=True)
        hot = e_iota == idx
        lg = jnp.where(hot, NEG_INF, lg)
        vals.append(m)
        idxs.append(idx)
        hots.append(hot)
    es = [jnp.exp(v - vals[0]) for v in vals]
    den = es[0] + es[1] + es[2] + es[3]
    sel = jnp.zeros((N_EXPERTS, tm), F32)
    for hot in hots:
        sel = sel + jnp.where(hot, 1.0, 0.0)
    before = jnp.where(lax.broadcasted_iota(I32, (tm, tm), 0) < lax.broadcasted_iota(I32, (tm, tm), 1),
                       1.0, 0.0).astype(BF16)
    cum = _dot(sel.astype(BF16), before) + carry_ref[:, 0:1]
    ranks = [jnp.sum(jnp.where(hot, cum, 0.0), axis=0, keepdims=True) for hot in hots]
    total = carry_ref[...] + jnp.sum(sel, axis=1, keepdims=True)
    carry_ref[...] = total
    idx_ref[...] = jnp.concatenate(idxs, axis=0).astype(I32)
    gate_ref[...] = jnp.concatenate([e / den for e in es], axis=0)
    rank_ref[...] = jnp.concatenate(ranks, axis=0).astype(I32)
    cnt_ref[...] = total


def _router(x, w_pad, b_pad, tm=512):
    n, d = x.shape
    tm = min(tm, n)
    out4 = lambda dt: jax.ShapeDtypeStruct((TOP_K, n), dt)
    blk4 = pl.BlockSpec((TOP_K, tm), lambda i: (0, i))
    return pl.pallas_call(
        functools.partial(_router_body, tm),
        grid=(n // tm,),
        in_specs=[pl.BlockSpec((tm, d), lambda i: (i, 0)),
                  pl.BlockSpec((d, LANES), lambda i: (0, 0)),
                  pl.BlockSpec((1, LANES), lambda i: (0, 0))],
        out_specs=[blk4, blk4, blk4, pl.BlockSpec((N_EXPERTS, LANES), lambda i: (0, 0))],
        out_shape=[out4(I32), out4(F32), out4(I32), jax.ShapeDtypeStruct((N_EXPERTS, LANES), F32)],
        scratch_shapes=[pltpu.VMEM((N_EXPERTS, LANES), F32)],
        compiler_params=_cparams(("arbitrary",)),
        name="moe_router",
    )(x, w_pad, b_pad)


def _row_slab(row):
    return pl.ds(pl.multiple_of(row * ROW_TILES, ROW_TILES), ROW_TILES)


def _start_row_dma(copy, k):
    copy.start(priority=k % 2)


def _start_row_dmas(n_rows, make_copy):
    def issue(r, carry):
        for k in range(TOP_K):
            _start_row_dma(make_copy(k, r), k)
        return carry

    lax.fori_loop(0, n_rows, issue, 0, unroll=4)


def _wait_row_dmas(n_rows, make_copy):
    def drain(r, carry):
        for k in range(TOP_K):
            make_copy(k, r).wait()
        return carry

    lax.fori_loop(0, n_rows, drain, 0, unroll=8)


def _row_dma_loops(n_rows, make_copy):
    _start_row_dmas(n_rows, make_copy)
    _wait_row_dmas(n_rows, make_copy)


def _dispatch_body(td, dest_ref, fill_ref, x3_ref, xg_out, zero_ref, sem, zsem):
    @pl.when(pl.program_id(0) == 0)
    def _():
        zero_ref[...] = jnp.zeros_like(zero_ref)

        def zero_copy(r):
            return pltpu.make_async_copy(zero_ref, xg_out.at[_row_slab(r)], zsem)

        def each_expert(fn):
            def expert(e, carry):
                lax.fori_loop(fill_ref[0, e], fill_ref[1, e], lambda r, c: (fn(r), c)[1], 0)
                return carry
            lax.fori_loop(0, N_EXPERTS, expert, 0)

        each_expert(lambda r: zero_copy(r).start())
        each_expert(lambda r: zero_copy(r).wait())

    _row_dma_loops(td, lambda k, r: pltpu.make_async_copy(
        x3_ref.at[_row_slab(r)], xg_out.at[_row_slab(dest_ref[k, r])], sem))


def _smem_rows(tile):
    return pl.BlockSpec((TOP_K, tile), lambda i: (0, i), memory_space=pltpu.SMEM)


def _dispatch(x3, dest, fill, n_rows, td=512):
    n = x3.shape[0] // ROW_TILES
    td = min(td, n)
    return pl.pallas_call(
        functools.partial(_dispatch_body, td),
        grid=(n // td,),
        in_specs=[_smem_rows(td),
                  pl.BlockSpec(memory_space=pltpu.SMEM),
                  pl.BlockSpec((td * ROW_TILES, LANES), lambda i: (i, 0))],
        out_specs=pl.BlockSpec(memory_space=pl.ANY),
        out_shape=jax.ShapeDtypeStruct((n_rows * ROW_TILES, LANES), x3.dtype),
        scratch_shapes=[pltpu.VMEM((ROW_TILES, LANES), x3.dtype), pltpu.SemaphoreType.DMA(()),
                        pltpu.SemaphoreType.DMA(())],
        compiler_params=_cparams(("arbitrary",)),
        name="moe_dispatch",
    )(dest, fill, x3)


def _expert_body(be_ref, nu_ref, x_ref, wgu_ref, bgu_ref, wd_ref, bd_ref, o_ref, wgu_b, wd_b):
    i = pl.program_id(0)
    prev = be_ref[jnp.maximum(i - 1, 0)]
    fresh = jnp.logical_or(i == 0, be_ref[i] != prev)

    @pl.when(jnp.logical_and(i < nu_ref[0], fresh))
    def _():
        wgu_b[...] = wgu_ref[...].astype(BF16)
        wd_b[...] = wd_ref[...].astype(BF16)

    @pl.when(i < nu_ref[0])
    def _():
        x = _load_tile_rows(x_ref, MOE_ROWS).astype(BF16)
        hcat = _dot(x, wgu_b[...]) + bgu_ref[...]
        g = jnp.minimum(hcat[:, :D_FF], SWIGLU_LIMIT)
        u = jnp.clip(hcat[:, D_FF:], -SWIGLU_LIMIT, SWIGLU_LIMIT)
        act = g * _sigmoid(SWIGLU_ALPHA * g) * (u + 1.0)
        _store_tile_rows(o_ref, _dot(act.astype(BF16), wd_b[...]) + bd_ref[...])

    @pl.when(i >= nu_ref[0])
    def _():
        o_ref[...] = jnp.zeros_like(o_ref)


def _experts(xg, layer, block_expert, n_used, w_gu, b_gu, w_down, b_down):
    d = D_MODEL
    nb = xg.shape[0] // (MOE_ROWS * ROW_TILES)
    blk = pl.BlockSpec((MOE_ROWS * ROW_TILES, LANES), lambda i, be, nu: (i, 0))
    grid_spec = pltpu.PrefetchScalarGridSpec(
        num_scalar_prefetch=2,
        grid=(nb,),
        in_specs=[
            pl.BlockSpec((MOE_ROWS * ROW_TILES, LANES), lambda i, be, nu: (jnp.minimum(i, nu[0] - 1), 0)),
            pl.BlockSpec((None, None, d, 2 * D_FF), lambda i, be, nu: (layer, be[i], 0, 0)),
            pl.BlockSpec((None, None, 1, 2 * D_FF), lambda i, be, nu: (layer, be[i], 0, 0)),
            pl.BlockSpec((None, None, D_FF, d), lambda i, be, nu: (layer, be[i], 0, 0)),
            pl.BlockSpec((None, None, 1, d), lambda i, be, nu: (layer, be[i], 0, 0)),
        ],
        out_specs=blk,
        scratch_shapes=[pltpu.VMEM((d, 2 * D_FF), BF16), pltpu.VMEM((D_FF, d), BF16)],
    )
    return pl.pallas_call(
        _expert_body,
        grid_spec=grid_spec,
        out_shape=jax.ShapeDtypeStruct(xg.shape, F32),
        compiler_params=_cparams(("arbitrary",)),
        name="moe_experts",
    )(block_expert, n_used, xg, w_gu, b_gu[:, :, None, :], w_down, b_down[:, :, None, :])


def _combine_ln_body(tc, dest_ref, dest_next_ref, yg_hbm, gate_ref, x_ref, g_ref, b_ref, o_ref, ybuf, sems):
    i = pl.program_id(0)

    def gather(rows_ref, s):
        return lambda k, r: pltpu.make_async_copy(
            yg_hbm.at[_row_slab(rows_ref[k, r])], ybuf.at[s, k, _row_slab(r)], sems.at[s])

    def run(s):
        @pl.when(i == 0)
        def _():
            _start_row_dmas(tc, gather(dest_ref, s))

        @pl.when(i + 1 < pl.num_programs(0))
        def _():
            _start_row_dmas(tc, gather(dest_next_ref, 1 - s))

        _wait_row_dmas(tc, gather(dest_ref, s))
        gate = gate_ref[...]
        hmoe = DEEPNORM_ALPHA * x_ref[...]
        for k in range(TOP_K):
            hmoe = hmoe + gate[:, k:k + 1] * _load_tile_rows(ybuf, tc, (s, k))
        o_ref[...] = _layer_norm(hmoe, g_ref[...], b_ref[...])

    pl.when(i % 2 == 0)(functools.partial(run, 0))
    pl.when(i % 2 == 1)(functools.partial(run, 1))


def _combine_ln(yg, dest, gate_t, x, g, b, tc=256):
    n, d = x.shape
    tc = min(tc, n)
    steps = n // tc
    row = pl.BlockSpec((1, d), lambda i: (0, 0))
    return pl.pallas_call(
        functools.partial(_combine_ln_body, tc),
        grid=(steps,),
        in_specs=[_smem_rows(tc),
                  pl.BlockSpec((TOP_K, tc), lambda i: (0, jnp.minimum(i + 1, steps - 1)), memory_space=pltpu.SMEM),
                  pl.BlockSpec(memory_space=pl.ANY),
                  pl.BlockSpec((tc, TOP_K), lambda i: (i, 0)),
                  pl.BlockSpec((tc, d), lambda i: (i, 0)), row, row],
        out_specs=pl.BlockSpec((tc, d), lambda i: (i, 0)),
        out_shape=jax.ShapeDtypeStruct((n, d), F32),
        scratch_shapes=[pltpu.VMEM((2, TOP_K, tc * ROW_TILES, LANES), F32), pltpu.SemaphoreType.DMA((2,))],
        compiler_params=_cparams(("arbitrary",)),
        name="moe_combine_ln",
    )(dest, dest, yg, gate_t, x, g, b)


def _moe_ln(x, x3, layer, router_w, router_b, w_gu, b_gu, w_down, b_down, ln_g, ln_b):
    n, d = x.shape
    idx, gate, rank, cnt = _router(x, _pad_lanes(router_w), _pad_lanes(router_b[None, :]))
    counts = cnt[:, 0].astype(I32)
    padded = (counts + MOE_ROWS - 1) // MOE_ROWS * MOE_ROWS
    pad_end = jnp.cumsum(padded)
    pad_start = pad_end - padded
    hot = idx[:, :, None] == jnp.arange(N_EXPERTS, dtype=I32)[None, None, :]
    dest = jnp.sum(jnp.where(hot, pad_start[None, None, :], 0), axis=-1) + rank
    n_blocks = -(-(n * TOP_K) // MOE_ROWS) + N_EXPERTS
    block_row0 = jnp.arange(n_blocks, dtype=I32) * MOE_ROWS
    block_expert = jnp.minimum(jnp.sum((pad_end[None, :] <= block_row0[:, None]).astype(I32), axis=1),
                               N_EXPERTS - 1)
    n_used = (pad_end[-1:] // MOE_ROWS).astype(I32)
    fill = jnp.stack([pad_start + counts, pad_end])
    xg = _dispatch(x3, dest, fill, n_blocks * MOE_ROWS)
    yg = _experts(xg, layer, block_expert, n_used, w_gu, b_gu, w_down, b_down)
    return _combine_ln(yg, dest, gate.T, x, ln_g[None, :], ln_b[None, :])


def _even_mixer_ln(x, bsz, seq, w_in, w_out, lam_params, subln_w, conv_w, a_log, dt_bias, gdn_norm_w,
                   t5_bias, lam_init, ln_g, ln_b):
    main = 3 * A_HEADS * LANES + 4 * B_HEADS * B_HEAD_DIM
    n_q = A_HEADS * 2 * A_HEAD_DIM
    col_scale = jnp.concatenate([jnp.full((1, n_q), A_HEAD_DIM ** -0.5 * LOG2E, F32),
                                 jnp.ones((1, main - n_q), F32)], axis=1)
    proj = _proj(x, w_in[:, :main], col_scale)
    zeros4 = jnp.zeros((B_HEADS,), F32)
    p0 = _pad_lanes(jnp.concatenate([zeros4, a_log.astype(F32)])[None, :])
    p1 = _pad_lanes(jnp.concatenate([zeros4, dt_bias.astype(F32)])[None, :])
    gates = _gates("even", x, _pad_lanes(w_in[:, main:]), p0, p1, seq)
    t = min(ATTN_TILE, seq)
    strips = _t5_strips(t5_bias, seq, t)
    ao = _diff_attention(proj, bsz, seq, strips, lam_params.astype(F32), subln_w[None, :].astype(F32),
                         lam_init, t)
    g8 = gates[:, :2 * B_HEADS].reshape(bsz, seq, 2, B_HEADS)
    gates_row = g8.transpose(0, 3, 2, 1).reshape(bsz * B_HEADS, 2, seq)
    cw = conv_w.astype(F32).reshape(CONV_WIDTH, 3, B_HEADS, B_HEAD_DIM).transpose(2, 1, 0, 3)
    cw = cw.reshape(B_HEADS, 3 * CONV_WIDTH, B_HEAD_DIM)
    bo = _gdn(proj, gates, gates_row, cw, gdn_norm_w[None, :].astype(F32), bsz, seq, min(GDN_TILE, seq))
    return _outproj_ln(ao, 0, bo, 0, w_out, x, ln_g[None, :], ln_b[None, :])


def _odd_mixer_ln(x, bsz, seq, w_in, w_out, qk_norm_w, forget_b, ln_g, ln_b):
    main = 4 * C_HEADS * C_HEAD_DIM
    width = C_HEADS * C_HEAD_DIM
    scale = jnp.asarray([C_HEAD_DIM ** -0.5 * LOG2E, 1.0], F32)[:, None]
    wqk = qk_norm_w.astype(F32) * scale
    col_scale = jnp.concatenate([jnp.tile(wqk[0], C_HEADS), jnp.tile(wqk[1], C_HEADS),
                                 jnp.ones((main - 2 * width,), F32)])[None, :]
    proj = _proj(x, w_in[:, :main], col_scale, norm_chunks=range(2 * width // PROJ_CHUNK))
    fb =_pad_lanes(forget_b.astype(F32)[None, :])
    cum = _gates("odd", x, _pad_lanes(w_in[:, main:]), fb, fb, seq)
    cum_t = cum[:, :C_HEADS].reshape(bsz, seq, C_HEADS // 2, 2).transpose(0, 2, 3, 1)
    logit_bound = 1.02 * C_HEAD_DIM * jnp.max(jnp.abs(wqk[0])) * jnp.max(jnp.abs(wqk[1]))
    o = _fox_attention(proj, proj, cum_t, logit_bound, bsz, seq, min(ATTN_TILE, seq))
    return _outproj_ln(o, 0, o, 1, w_out, x, ln_g[None, :], ln_b[None, :])


def kernel(x, t5_bias, even_w_in, even_w_out, diff_lambda, diff_subln_w, gdn_conv_w, gdn_a_log, gdn_dt_bias, gdn_norm_w, odd_w_in, odd_w_out, fox_qk_norm_w, fox_forget_b, router_w, router_b, moe_w_gate_up, moe_b_gate_up, moe_w_down, moe_b_down, ln_mix_g, ln_mix_b, ln_ffn_g, ln_ffn_b):
    bsz, seq, d = x.shape
    xf = x.reshape(bsz * seq, d)
    for layer in range(DEPTH):
        i = layer // 2
        if layer % 2 == 0:
            lam_init = 0.8 - 0.6 * math.exp(-0.3 * layer)
            xf, x3 = _even_mixer_ln(xf, bsz, seq, even_w_in[i], even_w_out[i], diff_lambda[i], diff_subln_w[i],
                                    gdn_conv_w[i], gdn_a_log[i], gdn_dt_bias[i], gdn_norm_w[i], t5_bias, lam_init,
                                    ln_mix_g[layer], ln_mix_b[layer])
        else:
            xf, x3 = _odd_mixer_ln(xf, bsz, seq, odd_w_in[i], odd_w_out[i], fox_qk_norm_w[i], fox_forget_b[i],
                                   ln_mix_g[layer], ln_mix_b[layer])
        xf = _moe_ln(xf, x3, layer, router_w[layer], router_b[layer], moe_w_gate_up, moe_b_gate_up,
                     moe_w_down, moe_b_down, ln_ffn_g[layer], ln_ffn_b[layer])
    return xf.reshape(bsz, seq, d)
```

```python
import functools
import math

import numpy as np
import jax
import jax.numpy as jnp
from jax import lax
from jax.experimental import pallas as pl
from jax.experimental.pallas import tpu as pltpu

F32 = jnp.float32
BF16 = jnp.bfloat16
I32 = jnp.int32
HIGHEST = lax.Precision.HIGHEST

D_MODEL = 1024
DEPTH = 4
A_HEADS = 4
A_HEAD_DIM = 64
B_HEADS = 4
B_HEAD_DIM = 128
CONV_WIDTH = 4
C_HEADS = 16
C_HEAD_DIM = 64
T5_BUCKETS = 32
T5_MAX_DISTANCE = 2048
N_EXPERTS = 32
TOP_K = 4
D_FF = D_MODEL
SWIGLU_LIMIT = 7.0
SWIGLU_ALPHA = 1.702
DEEPNORM_ALPHA = (2 * DEPTH) ** 0.25
LN_EPS = 1e-5
RMS_EPS = 1e-6

LANES = 128
SUBLANES = 8
VMEM_LIMIT = 56 * 1024 * 1024

ATTN_TILE = 1024
ATTN_ROWS = 1024
GDN_CHUNK = 128
GDN_TILE = 512
GDN_HEADS_PER_STEP = 4
MOE_ROWS = 512
FOX_SKIP_MARGIN = 170.0
NEG_INF = float("-inf")
LOG2E = math.log2(math.e)


def _cparams(sem, vmem=VMEM_LIMIT):
    return pltpu.CompilerParams(dimension_semantics=sem, vmem_limit_bytes=vmem)


def _dot(a, b, **kw):
    return jnp.dot(a, b, preferred_element_type=F32, **kw)


def _dot_nt(a, b, **kw):
    return lax.dot_general(a, b, (((1,), (1,)), ((), ())), preferred_element_type=F32, **kw)


def _dot_tn(a, b, **kw):
    return lax.dot_general(a, b, (((0,), (0,)), ((), ())), preferred_element_type=F32, **kw)


def _sigmoid(x):
    return 1.0 / (1.0 + jnp.exp(-x))


def _softplus(x):
    return jnp.maximum(x, 0.0) + jnp.log(1.0 + jnp.exp(-jnp.abs(x)))


def _layer_norm(xf, g, b):
    mu = jnp.mean(xf, axis=-1, keepdims=True)
    xc = xf - mu
    var = jnp.mean(xc * xc, axis=-1, keepdims=True)
    return xc * lax.rsqrt(var + LN_EPS) * g + b


PROJ_CHUNK = 512


def _proj_body(norm_chunks, x_ref, w_ref, cs_ref, gsum_ref, o_ref):
    xb = x_ref[...].astype(BF16)
    for c in range(w_ref.shape[1] // PROJ_CHUNK):
        cols = slice(c * PROJ_CHUNK, (c + 1) * PROJ_CHUNK)
        acc = _dot(xb, w_ref[:, cols])
        if c in norm_chunks:
            ss = _dot((acc * acc).astype(BF16), gsum_ref[...])
            acc = acc * lax.rsqrt(ss * (1.0 / C_HEAD_DIM) + RMS_EPS)
        o_ref[:, cols] = (acc * cs_ref[:, cols]).astype(o_ref.dtype)


def _proj(x, w, col_scale, norm_chunks=(), tm=512):
    n, k = x.shape
    m = w.shape[1]
    tm = min(tm, n)
    head = jnp.arange(PROJ_CHUNK, dtype=I32) // C_HEAD_DIM
    gsum = (head[:, None] == head[None, :]).astype(BF16)
    return pl.pallas_call(
        functools.partial(_proj_body, tuple(norm_chunks)),
        grid=(n // tm,),
        in_specs=[pl.BlockSpec((tm, k), lambda i: (i, 0)),
                  pl.BlockSpec((k, m), lambda i: (0, 0)),
                  pl.BlockSpec((1, m), lambda i: (0, 0)),
                  pl.BlockSpec((PROJ_CHUNK, PROJ_CHUNK), lambda i: (0, 0))],
        out_specs=pl.BlockSpec((tm, m), lambda i: (i, 0)),
        out_shape=jax.ShapeDtypeStruct((n, m), BF16),
        compiler_params=_cparams(("parallel",)),
        name="proj",
    )(x, w.astype(BF16), col_scale, gsum)


def _split_bf16(a, parts):
    out = []
    for _ in range(parts):
        hi = a.astype(BF16)
        out.append(hi)
        a = a - hi.astype(F32)
    return out


def _tri_sum(tri, a):
    tri = tri.astype(BF16)
    return sum(_dot(tri, part) for part in _split_bf16(a, 3))


def _gates_body(mode, tm, steps_per_seq, x_ref, w_ref, p0_ref, p1_ref, o_ref, carry_ref):
    xh, xl = _split_bf16(x_ref[...], 2)
    wh, wl = _split_bf16(w_ref[...], 2)
    z = _dot(xh, wh) + (_dot(xl, wh) + _dot(xh, wl))
    ri = lax.broadcasted_iota(I32, (tm, tm), 0)
    ci = lax.broadcasted_iota(I32, (tm, tm), 1)
    if mode == "even":
        lane = lax.broadcasted_iota(I32, (tm, LANES), 1)
        beta = _sigmoid(z)
        g = -jnp.exp(p0_ref[...]) * _softplus(z + p1_ref[...])
        shift = int(math.log2(GDN_CHUNK))
        same_chunk = jnp.right_shift(ri, shift) == jnp.right_shift(ci, shift)
        tri = jnp.where(jnp.logical_and(same_chunk, ci <= ri), 1.0, 0.0)
        gc = _tri_sum(tri, g)
        o_ref[...] = jnp.where(lane < B_HEADS, beta, gc)
    else:
        @pl.when(pl.program_id(0) % steps_per_seq == 0)
        def _():
            carry_ref[...] = jnp.zeros_like(carry_ref)

        logf = -_softplus(-(z + p0_ref[...]))
        tri = jnp.where(ci <= ri, 1.0, 0.0)
        cum = _tri_sum(tri, logf) + carry_ref[0:1, :]
        o_ref[...] = cum
        carry_ref[...] = jnp.broadcast_to(cum[tm - 1:tm, :], carry_ref.shape)


def _gates(mode, x, w_small, p0, p1, seq_len, tm=512):
    n, k = x.shape
    tm = min(tm, seq_len)
    row = pl.BlockSpec((1, LANES), lambda i: (0, 0))
    return pl.pallas_call(
        functools.partial(_gates_body, mode, tm, seq_len // tm),
        grid=(n // tm,),
        in_specs=[pl.BlockSpec((tm, k), lambda i: (i, 0)),
                  pl.BlockSpec((k, LANES), lambda i: (0, 0)), row, row],
        out_specs=pl.BlockSpec((tm, LANES), lambda i: (i, 0)),
        out_shape=jax.ShapeDtypeStruct((n, LANES), F32),
        scratch_shapes=[pltpu.VMEM((SUBLANES, LANES), F32)],
        compiler_params=_cparams(("arbitrary",)),
        name="gates_" + mode,
    )(x, w_small, p0, p1)


def _pad_lanes(a):
    return jnp.pad(a, [(0, 0)] * (a.ndim - 1) + [(0, LANES - a.shape[-1])])


def _attn_body(mode, t, lam_init, *rest):
    if mode == "diff":
        q_ref, k_ref, v_ref, strip_ref, lam_ref, subln_ref, o_ref, qs_ref, m_ref, acc_ref, l_ref, tile_ref = rest
    else:
        jmin_ref, q_ref, k_ref, v_ref, ck_ref, gate_ref, o_ref, qs_ref, m_ref, acc_ref = rest
    i = pl.program_id(2)
    rc = min(ATTN_ROWS, t)

    q = q_ref[...]
    lane = lax.broadcasted_iota(I32, q.shape, 1)
    zero = jnp.zeros_like(q)
    qs_ref[0:t, :] = jnp.where(lane < 64, q, zero)
    qs_ref[t:2 * t, :] = jnp.where(lane < 64, zero, q)
    m_ref[...] = jnp.full(m_ref.shape, NEG_INF, F32)
    acc_ref[...] = jnp.zeros_like(acc_ref)
    if mode == "diff":
        l_ref[...] = jnp.zeros_like(l_ref)

        @pl.when(i == 0)
        def _():
            for d in range(strip_ref.shape[0]):
                x = jnp.broadcast_to(strip_ref[d], (t, 2 * t))
                tile_ref[d] = pltpu.roll(x, t + 1, 1, stride=1, stride_axis=0)[:, :t]
    else:
        c0 = ck_ref[:, pl.ds(pl.multiple_of(i * t, t), LANES)][:, 0:1]

    def step(kind, j, d=0):
        keys = pl.ds(pl.multiple_of(j * t, t), t)
        k = k_ref[keys, :]
        v = v_ref[keys, :]
        if mode == "fox":
            lane = lax.broadcasted_iota(I32, v.shape, 1)
            one = jnp.ones_like(v)
            rhs = (jnp.where(lane < 64, v, one), jnp.where(lane < 64, one, v))
            bias = (c0 - ck_ref[:, keys]) * LOG2E
        if kind == "diag" and t >= 2 * LANES:
            chunks = [(half, h0, t // 2, h0 + t // 2) for half in range(2) for h0 in (0, t // 2)]
        else:
            chunks = [(r0 // t, r0 % t, rc, t) for r0 in range(0, 2 * t, rc)]
        for half, h0, nr, nk in chunks:
            rows = slice(half * t + h0, half * t + h0 + nr)
            s = _dot_nt(qs_ref[rows, :], k[:nk])
            if mode == "fox":
                s = s + bias[half:half + 1, :nk]
            elif kind != "far":
                s = s + tile_ref[d, h0:h0 + nr, :nk]
            if kind == "diag":
                rr = lax.broadcasted_iota(I32, (nr, nk), 0) + h0
                cc = lax.broadcasted_iota(I32, (nr, nk), 1)
                s = jnp.where(rr >= cc, s, NEG_INF)
            m_prev = m_ref[rows, :]
            m_new = jnp.maximum(m_prev, jnp.max(s, axis=1, keepdims=True))
            alpha = jnp.exp2(m_prev - m_new)
            pr = jnp.exp2(s - pltpu.repeat(m_new, nk // LANES, 1))
            if mode == "diff":
                l_ref[rows, :] = alpha * l_ref[rows, :] + jnp.sum(pr, axis=1, keepdims=True)
                pv = _dot(pr.astype(BF16), v[:nk])
            else:
                pv = _dot(pr.astype(BF16), rhs[half][:nk])
            acc_ref[rows, :] = alpha * acc_ref[rows, :] + pv
            m_ref[rows, :] = m_new

    def each_block(kind, lo, hi):
        lax.fori_loop(lo, hi, lambda j, carry: (step(kind, j), carry)[1], 0)

    if mode == "diff":
        n_near = strip_ref.shape[0] - 1
        each_block("far", 0, jnp.maximum(i - n_near, 0))
        for d in range(n_near, 0, -1):
            pl.when(i >= d)(functools.partial(step, "near", i - d, d))
    else:
        row = (pl.program_id(0) * pl.num_programs(1) + pl.program_id(1)) * pl.num_programs(2)
        each_block("off", jmin_ref[row + i], i)
    step("diag", i)

    acc = acc_ref[...]
    if mode == "diff":
        o = acc * (1.0 / l_ref[...])
        lp = lam_ref[...]
        lam = (jnp.exp(jnp.sum(lp[0:1] * lp[1:2], axis=-1, keepdims=True))
               - jnp.exp(jnp.sum(lp[2:3] * lp[3:4], axis=-1, keepdims=True)) + lam_init)
        dlt = o[0:t] - lam * o[t:2 * t]
        ms = jnp.mean(dlt * dlt, axis=-1, keepdims=True)
        out = dlt * lax.rsqrt(ms + RMS_EPS) * subln_ref[...] * (1.0 - lam_init)
    else:
        lane = lax.broadcasted_iota(I32, (t, LANES), 1)
        lo = acc[0:t]
        hi = acc[t:2 * t]
        out = jnp.where(lane < 64, lo * (1.0 / lo[:, 64:65]), hi * (1.0 / hi[:, 0:1]))
        out = out * _sigmoid(gate_ref[...].astype(F32))
    o_ref[...] = out.astype(o_ref.dtype)


def _attn_scratch(mode, t, n_tiles=0):
    base = [pltpu.VMEM((2 * t, LANES), BF16), pltpu.VMEM((2 * t, LANES), F32), pltpu.VMEM((2 * t, LANES), F32)]
    if mode == "diff":
        base += [pltpu.VMEM((2 * t, LANES), F32), pltpu.VMEM((n_tiles, t, t), F32)]
    return base


def _diff_attention(proj, bsz, seq, strips, lam_params, subln_w, lam_init, t):
    nq = seq // t
    nd = strips.shape[1]
    h_ = A_HEADS
    return pl.pallas_call(
        functools.partial(_attn_body, "diff", t, lam_init),
        grid=(bsz, h_, nq),
        in_specs=[
            pl.BlockSpec((t, LANES), lambda b, h, i: (b * nq + i, h)),
            pl.BlockSpec((seq, LANES), lambda b, h, i: (b, h_ + h)),
            pl.BlockSpec((seq, LANES), lambda b, h, i: (b, 2 * h_ + h)),
            pl.BlockSpec((None, nd, 1, 2 * t), lambda b, h, i: (h, 0, 0, 0)),
            pl.BlockSpec(lam_params.shape, lambda b, h, i: (0, 0)),
            pl.BlockSpec((1, LANES), lambda b, h, i: (0, 0)),
        ],
        out_specs=pl.BlockSpec((t, LANES), lambda b, h, i: (b * nq + i, h)),
        out_shape=jax.ShapeDtypeStruct((bsz * seq, h_ * LANES), BF16),
        scratch_shapes=_attn_scratch("diff", t, nd),
        compiler_params=_cparams(("parallel", "parallel", "arbitrary")),
        name="diff_attn",
    )(proj, proj, proj, strips, lam_params, subln_w)


def _fox_first_live_block(cum_t, logit_bound, t):
    bsz, hp, _, seq = cum_t.shape
    nq = seq // t
    c_start = cum_t[..., 0::t]
    c_end = cum_t[..., t - 1::t]
    gap = jnp.max(c_start[..., :, None] - c_end[..., None, :], axis=2) * LOG2E
    dead = gap < -(FOX_SKIP_MARGIN + 2.0 * logit_bound)
    dead = jnp.logical_and(dead, jnp.arange(nq)[None, :] < jnp.arange(nq)[:, None])
    return jnp.sum(jnp.cumprod(dead.astype(I32), axis=-1), axis=-1).reshape(-1)


def _fox_attention(qk, proj, cum_t, logit_bound, bsz, seq, t):
    nq = seq // t
    hp = C_HEADS // 2
    jmin = _fox_first_live_block(cum_t, logit_bound, t)
    grid_spec = pltpu.PrefetchScalarGridSpec(
        num_scalar_prefetch=1,
        grid=(bsz, hp, nq),
        in_specs=[
            pl.BlockSpec((t, LANES), lambda b, h, i, jm: (b * nq + i, h)),
            pl.BlockSpec((seq, LANES), lambda b, h, i, jm: (b, hp + h)),
            pl.BlockSpec((seq, LANES), lambda b, h, i, jm: (b, 2 * hp + h)),
            pl.BlockSpec((None, None, 2, seq), lambda b, h, i, jm: (b, h, 0, 0)),
            pl.BlockSpec((t, LANES), lambda b, h, i, jm: (b * nq + i, 3 * hp + h)),
        ],
        out_specs=pl.BlockSpec((t, LANES), lambda b, h, i, jm: (b * nq + i, h)),
        scratch_shapes=_attn_scratch("fox", t),
    )
    return pl.pallas_call(
        functools.partial(_attn_body, "fox", t, 0.0),
        grid_spec=grid_spec,
        out_shape=jax.ShapeDtypeStruct((bsz * seq, hp * LANES), BF16),
        compiler_params=_cparams(("parallel", "parallel", "arbitrary")),
        name="fox_attn",
    )(jmin, qk, qk, proj, cum_t, proj)


def _t5_bias_by_distance(t5_bias, seq):
    n = jnp.arange(seq, dtype=I32)
    max_exact = T5_BUCKETS // 2
    nf = jnp.maximum(n, 1).astype(F32)
    large = max_exact + (jnp.log(nf / max_exact) / math.log(T5_MAX_DISTANCE / max_exact)
                         * (T5_BUCKETS - max_exact)).astype(I32)
    large = jnp.minimum(large, T5_BUCKETS - 1)
    bucket = jnp.where(n < max_exact, n, large)
    return t5_bias.astype(F32).T[:, bucket]


def _t5_strips(t5_bias, seq, t):
    last = t5_bias.astype(F32).T[:, T5_BUCKETS - 1:T5_BUCKETS]
    vec = (_t5_bias_by_distance(t5_bias, seq) - last) * LOG2E
    nd = 1
    while nd * t - (t - 1) < T5_MAX_DISTANCE and nd < seq // t:
        nd += 1
    d = np.arange(nd)[:, None]
    c = np.arange(2 * t)[None, :]
    rel = d * t + t - 1 - c
    rel = np.clip(rel, 0, seq - 1)
    strips = vec[:, jnp.asarray(rel.astype(np.int32))]
    return strips[:, :, None, :]


def _gdn_body(tb, xq_ref, xk_ref, xv_ref, z_ref, gcol_ref, grow_ref, cw_ref, nw_ref, o_ref, s_ref, carry_ref):
    @pl.when(pl.program_id(2) == 0)
    def _():
        s_ref[...] = jnp.zeros_like(s_ref)
        carry_ref[...] = jnp.zeros_like(carry_ref)

    n_heads = GDN_HEADS_PER_STEP
    c_ = GDN_CHUNK
    nc = tb // c_
    gates = gcol_ref[...]
    lane = lax.broadcasted_iota(I32, gates.shape, 1)

    def conv_silu(g, idx, x_ref):
        x = x_ref[:, g * LANES:(g + 1) * LANES].astype(F32)
        xe = jnp.concatenate([carry_ref[g, idx], x], axis=0)
        taps = cw_ref[g, 4 * idx:4 * idx + 4, :]
        y = taps[3:4] * x
        for tap in range(CONV_WIDTH - 1):
            y = y + taps[tap:tap + 1] * pltpu.roll(xe, CONV_WIDTH - 1 - tap, 0)[SUBLANES:]
        carry_ref[g, idx] = x[tb - SUBLANES:]
        return y * _sigmoid(y)

    q_l, k_l, v_l, beta_l, gc_l, gcr_l = [], [], [], [], [], []
    for g in range(n_heads):
        h = pl.program_id(1) * n_heads + g
        q = conv_silu(g, 0, xq_ref)
        k = conv_silu(g, 1, xk_ref)
        v_l.append(conv_silu(g, 2, xv_ref))
        q_l.append(q * lax.rsqrt(jnp.sum(q * q, axis=-1, keepdims=True) + RMS_EPS) * (B_HEAD_DIM ** -0.5))
        k_l.append(k * lax.rsqrt(jnp.sum(k * k, axis=-1, keepdims=True) + RMS_EPS))
        beta_l.append(jnp.sum(jnp.where(lane == h, gates, 0.0), axis=1, keepdims=True))
        gc_l.append(jnp.sum(jnp.where(lane == B_HEADS + h, gates, 0.0), axis=1, keepdims=True))
        gc_row = grow_ref[g, 1:2, :]
        gcr_l.extend(gc_row[None, :, c * c_:(c + 1) * c_] for c in range(nc))

    def chunked(xs):
        return jnp.concatenate([x.reshape(nc, c_, x.shape[-1]) for x in xs], axis=0)

    def bdot(a, b):
        return lax.dot_general(a.astype(BF16), b.astype(BF16), (((2,), (1,)), ((0,), (0,))),
                               preferred_element_type=F32)

    def bdot_nt(a, b):
        return lax.dot_general(a.astype(BF16), b.astype(BF16), (((2,), (2,)), ((0,), (0,))),
                               preferred_element_type=F32)

    q3, k3, v3, beta3, gc3 = chunked(q_l), chunked(k_l), chunked(v_l), chunked(beta_l), chunked(gc_l)
    gcr3 = jnp.concatenate(gcr_l, axis=0)
    ri = lax.broadcasted_iota(I32, (1, c_, c_), 1)
    ci = lax.broadcasted_iota(I32, (1, c_, c_), 2)
    decay = jnp.exp(jnp.where(ri >= ci, gc3 - gcr3, NEG_INF))
    egc = jnp.exp(gc3)
    kb3 = k3 * beta3
    rhs3 = jnp.concatenate([v3 * beta3, kb3 * egc], axis=2)
    g_last = gc3[:, c_ - 1:c_, :]
    qd3 = q3 * egc
    kd3 = k3 * jnp.exp(g_last - gc3)
    gl3 = jnp.exp(g_last)
    kk = bdot_nt(kb3, k3)
    qk3 = bdot_nt(q3, k3) * decay
    x = -(kk * jnp.where(ri > ci, decay, 0.0))
    r = x
    pw = x
    for _ in range(int(math.log2(c_)) - 1):
        pw = bdot(pw, pw)
        r = r + pw + bdot(r, pw)
    sol = rhs3 + bdot(r, rhs3)

    states = [s_ref[g] for g in range(n_heads)]
    outs = [[] for _ in range(n_heads)]
    for c in range(nc):
        for g in range(n_heads):
            n = g * nc + c
            sb = states[g].astype(BF16)
            v_new = sol[n, :, :B_HEAD_DIM] - _dot(sol[n, :, B_HEAD_DIM:].astype(BF16), sb)
            vnb = v_new.astype(BF16)
            outs[g].append(_dot(qd3[n].astype(BF16), sb) + _dot(qk3[n].astype(BF16), vnb))
            states[g] = states[g] * gl3[n] + _dot_tn(kd3[n].astype(BF16), vnb)
    for g in range(n_heads):
        cols = slice(g * LANES, (g + 1) * LANES)
        s_ref[g] = states[g]
        o = jnp.concatenate(outs[g], axis=0)
        o = o * lax.rsqrt(jnp.mean(o * o, axis=-1, keepdims=True) + RMS_EPS) * nw_ref[...]
        z = z_ref[:, cols].astype(F32)
        o_ref[:, cols] = (o * (z * _sigmoid(z))).astype(o_ref.dtype)


def _gdn(proj, gates, gates_row, conv_w, norm_w, bsz, seq, tb):
    nt = seq // tb
    h_ = B_HEADS
    gh = GDN_HEADS_PER_STEP
    col0 = 3 * A_HEADS
    assert h_ % gh == 0 and col0 % gh == 0
    blk = lambda off: pl.BlockSpec((tb, gh * LANES), lambda b, h, s: (b * nt + s, (col0 + off * h_) // gh + h))
    return pl.pallas_call(
        functools.partial(_gdn_body, tb),
        grid=(bsz, h_ // gh, nt),
        in_specs=[blk(0), blk(1), blk(2), blk(3),
                  pl.BlockSpec((tb, LANES), lambda b, h, s: (b * nt + s, 0)),
                  pl.BlockSpec((gh, 2, tb), lambda b, h, s: (b * (h_ // gh) + h, 0, s)),
                  pl.BlockSpec((gh, 3 * CONV_WIDTH, LANES), lambda b, h, s: (h, 0, 0)),
                  pl.BlockSpec((1, LANES), lambda b, h, s: (0, 0))],
        out_specs=pl.BlockSpec((tb, gh * LANES), lambda b, h, s: (b * nt + s, h)),
        out_shape=jax.ShapeDtypeStruct((bsz * seq, h_ * LANES), BF16),
        scratch_shapes=[pltpu.VMEM((gh, B_HEAD_DIM, B_HEAD_DIM), F32),
                        pltpu.VMEM((gh, 3, SUBLANES, LANES), F32)],
        compiler_params=_cparams(("parallel", "parallel", "arbitrary")),
        name="gdn",
    )(proj, proj, proj, proj, gates, gates_row, conv_w, norm_w)


ROW_TILES = D_MODEL // LANES


def _store_tile_rows(ref, y):
    t = y.shape[0]
    for c in range(ROW_TILES):
        ref[pl.ds(c, t, stride=ROW_TILES), :] = y[:, c * LANES:(c + 1) * LANES]


def _load_tile_rows(ref, t, lead=()):
    return jnp.concatenate([ref[lead + (pl.ds(c, t, stride=ROW_TILES), slice(None))] for c in range(ROW_TILES)],
                           axis=1)


def _outproj_ln_body(a_ref, b_ref, w_ref, x_ref, g_ref, bb_ref, o_ref, o3_ref, wb_ref):
    @pl.when(pl.program_id(0) == 0)
    def _():
        wb_ref[...] = w_ref[...].astype(BF16)

    half = a_ref.shape[1]
    hmix = _dot(a_ref[...], wb_ref[0:half, :]) + _dot(b_ref[...], wb_ref[half:, :])
    out = _layer_norm(DEEPNORM_ALPHA * x_ref[...] + hmix, g_ref[...], bb_ref[...])
    o_ref[...] = out
    _store_tile_rows(o3_ref, out)


def _outproj_ln(a, a_blk, b, b_blk, w, x, g, bb, tm=512):
    n, d = x.shape
    tm = min(tm, n)
    half = d // 2
    row = pl.BlockSpec((1, d), lambda i: (0, 0))
    return pl.pallas_call(
        _outproj_ln_body,
        grid=(n // tm,),
        in_specs=[pl.BlockSpec((tm, half), lambda i: (i, a_blk)),
                  pl.BlockSpec((tm, half), lambda i: (i, b_blk)),
                  pl.BlockSpec((d, d), lambda i: (0, 0)),
                  pl.BlockSpec((tm, d), lambda i: (i, 0)), row, row],
        out_specs=[pl.BlockSpec((tm, d), lambda i: (i, 0)),
                   pl.BlockSpec((tm * ROW_TILES, LANES), lambda i: (i, 0))],
        out_shape=[jax.ShapeDtypeStruct((n, d), F32), jax.ShapeDtypeStruct((n * ROW_TILES, LANES), F32)],
        scratch_shapes=[pltpu.VMEM((d, d), BF16)],
        compiler_params=_cparams(("arbitrary",)),
        name="outproj_ln",
    )(a, b, w, x, g, bb)


def _router_body(tm, x_ref, w_ref, b_ref, idx_ref, gate_ref, rank_ref, cnt_ref, carry_ref):
    @pl.when(pl.program_id(0) == 0)
    def _():
        carry_ref[...] = jnp.zeros_like(carry_ref)

    logits = _dot(x_ref[...], w_ref[...], precision=HIGHEST) + b_ref[...]
    lg = jnp.transpose(logits)[0:N_EXPERTS, :]
    e_iota = lax.broadcasted_iota(I32, (N_EXPERTS, tm), 0).astype(F32)
    vals, idxs, hots = [], [], []
    for _ in range(TOP_K):
        m = jnp.max(lg, axis=0, keepdims=True)
        idx = jnp.min(jnp.where(lg == m, e_iota, float(N_EXPERTS)), axis=0, keepdims=True)
        hot = e_iota == idx
        lg = jnp.where(hot, NEG_INF, lg)
        vals.append(m)
        idxs.append(idx)
        hots.append(hot)
    es = [jnp.exp(v - vals[0]) for v in vals]
    den = es[0] + es[1] + es[2] + es[3]
    sel = jnp.zeros((N_EXPERTS, tm), F32)
    for hot in hots:
        sel = sel + jnp.where(hot, 1.0, 0.0)
    before = jnp.where(lax.broadcasted_iota(I32, (tm, tm), 0) < lax.broadcasted_iota(I32, (tm, tm), 1),
                       1.0, 0.0).astype(BF16)
    cum = _dot(sel.astype(BF16), before) + carry_ref[:, 0:1]
    ranks = [jnp.sum(jnp.where(hot, cum, 0.0), axis=0, keepdims=True) for hot in hots]
    total = carry_ref[...] + jnp.sum(sel, axis=1, keepdims=True)
    carry_ref[...] = total
    idx_ref[...] = jnp.concatenate(idxs, axis=0).astype(I32)
    gate_ref[...] = jnp.concatenate([e / den for e in es], axis=0)
    rank_ref[...] = jnp.concatenate(ranks, axis=0).astype(I32)
    cnt_ref[...] = total


def _router(x, w_pad, b_pad, tm=512):
    n, d = x.shape
    tm = min(tm, n)
    out4 = lambda dt: jax.ShapeDtypeStruct((TOP_K, n), dt)
    blk4 = pl.BlockSpec((TOP_K, tm), lambda i: (0, i))
    return pl.pallas_call(
        functools.partial(_router_body, tm),
        grid=(n // tm,),
        in_specs=[pl.BlockSpec((tm, d), lambda i: (i, 0)),
                  pl.BlockSpec((d, LANES), lambda i: (0, 0)),
                  pl.BlockSpec((1, LANES), lambda i: (0, 0))],
        out_specs=[blk4, blk4, blk4, pl.BlockSpec((N_EXPERTS, LANES), lambda i: (0, 0))],
        out_shape=[out4(I32), out4(F32), out4(I32), jax.ShapeDtypeStruct((N_EXPERTS, LANES), F32)],
        scratch_shapes=[pltpu.VMEM((N_EXPERTS, LANES), F32)],
        compiler_params=_cparams(("arbitrary",)),
        name="moe_router",
    )(x, w_pad, b_pad)


def _row_slab(row):
    return pl.ds(pl.multiple_of(row * ROW_TILES, ROW_TILES), ROW_TILES)


def _start_row_dma(copy, k):
    copy.start(priority=k % 2)


def _start_row_dmas(n_rows, make_copy):
    def issue(r, carry):
        for k in range(TOP_K):
            _start_row_dma(make_copy(k, r), k)
        return carry

    lax.fori_loop(0, n_rows, issue, 0, unroll=4)


def _wait_row_dmas(n_rows, make_copy):
    def drain(r, carry):
        for k in range(TOP_K):
            make_copy(k, r).wait()
        return carry

    lax.fori_loop(0, n_rows, drain, 0, unroll=8)


def _row_dma_loops(n_rows, make_copy):
    _start_row_dmas(n_rows, make_copy)
    _wait_row_dmas(n_rows, make_copy)


def _dispatch_body(td, dest_ref, fill_ref, x3_ref, xg_out, zero_ref, sem, zsem):
    @pl.when(pl.program_id(0) == 0)
    def _():
        zero_ref[...] = jnp.zeros_like(zero_ref)

        def zero_copy(r):
            return pltpu.make_async_copy(zero_ref, xg_out.at[_row_slab(r)], zsem)

        def each_expert(fn):
            def expert(e, carry):
                lax.fori_loop(fill_ref[0, e], fill_ref[1, e], lambda r, c: (fn(r), c)[1], 0)
                return carry
            lax.fori_loop(0, N_EXPERTS, expert, 0)

        each_expert(lambda r: zero_copy(r).start())
        each_expert(lambda r: zero_copy(r).wait())

    _row_dma_loops(td, lambda k, r: pltpu.make_async_copy(
        x3_ref.at[_row_slab(r)], xg_out.at[_row_slab(dest_ref[k, r])], sem))


def _smem_rows(tile):
    return pl.BlockSpec((TOP_K, tile), lambda i: (0, i), memory_space=pltpu.SMEM)


def _dispatch(x3, dest, fill, n_rows, td=1024):
    n = x3.shape[0] // ROW_TILES
    td = min(td, n)
    return pl.pallas_call(
        functools.partial(_dispatch_body, td),
        grid=(n // td,),
        in_specs=[_smem_rows(td),
                  pl.BlockSpec(memory_space=pltpu.SMEM),
                  pl.BlockSpec((td * ROW_TILES, LANES), lambda i: (i, 0))],
        out_specs=pl.BlockSpec(memory_space=pl.ANY),
        out_shape=jax.ShapeDtypeStruct((n_rows * ROW_TILES, LANES), x3.dtype),
        scratch_shapes=[pltpu.VMEM((ROW_TILES, LANES), x3.dtype), pltpu.SemaphoreType.DMA(()),
                        pltpu.SemaphoreType.DMA(())],
        compiler_params=_cparams(("arbitrary",)),
        name="moe_dispatch",
    )(dest, fill, x3)


def _expert_body(be_ref, nu_ref, x_ref, wgu_ref, bgu_ref, wd_ref, bd_ref, o_ref, wgu_b, wd_b):
    i = pl.program_id(0)
    prev = be_ref[jnp.maximum(i - 1, 0)]
    fresh = jnp.logical_or(i == 0, be_ref[i] != prev)

    @pl.when(jnp.logical_and(i < nu_ref[0], fresh))
    def _():
        wgu_b[...] = wgu_ref[...].astype(BF16)
        wd_b[...] = wd_ref[...].astype(BF16)

    @pl.when(i < nu_ref[0])
    def _():
        x = _load_tile_rows(x_ref, MOE_ROWS).astype(BF16)
        hcat = _dot(x, wgu_b[...]) + bgu_ref[...]
        g = jnp.minimum(hcat[:, :D_FF], SWIGLU_LIMIT)
        u = jnp.clip(hcat[:, D_FF:], -SWIGLU_LIMIT, SWIGLU_LIMIT)
        act = g * _sigmoid(SWIGLU_ALPHA * g) * (u + 1.0)
        _store_tile_rows(o_ref, _dot(act.astype(BF16), wd_b[...]) + bd_ref[...])

    @pl.when(i >= nu_ref[0])
    def _():
        o_ref[...] = jnp.zeros_like(o_ref)


def _experts(xg, layer, block_expert, n_used, w_gu, b_gu, w_down, b_down):
    d = D_MODEL
    nb = xg.shape[0] // (MOE_ROWS * ROW_TILES)
    blk = pl.BlockSpec((MOE_ROWS * ROW_TILES, LANES), lambda i, be, nu: (i, 0))
    grid_spec = pltpu.PrefetchScalarGridSpec(
        num_scalar_prefetch=2,
        grid=(nb,),
        in_specs=[
            pl.BlockSpec((MOE_ROWS * ROW_TILES, LANES), lambda i, be, nu: (jnp.minimum(i, nu[0] - 1), 0)),
            pl.BlockSpec((None, None, d, 2 * D_FF), lambda i, be, nu: (layer, be[i], 0, 0)),
            pl.BlockSpec((None, None, 1, 2 * D_FF), lambda i, be, nu: (layer, be[i], 0, 0)),
            pl.BlockSpec((None, None, D_FF, d), lambda i, be, nu: (layer, be[i], 0, 0)),
            pl.BlockSpec((None, None, 1, d), lambda i, be, nu: (layer, be[i], 0, 0)),
        ],
        out_specs=blk,
        scratch_shapes=[pltpu.VMEM((d, 2 * D_FF), BF16), pltpu.VMEM((D_FF, d), BF16)],
    )
    return pl.pallas_call(
        _expert_body,
        grid_spec=grid_spec,
        out_shape=jax.ShapeDtypeStruct(xg.shape, F32),
        compiler_params=_cparams(("arbitrary",)),
        name="moe_experts",
    )(block_expert, n_used, xg, w_gu, b_gu[:, :, None, :], w_down, b_down[:, :, None, :])


def _combine_ln_body(tc, dest_ref, dest_next_ref, yg_hbm, gate_ref, x_ref, g_ref, b_ref, o_ref, ybuf, sems):
    i = pl.program_id(0)

    def gather(rows_ref, s):
        return lambda k, r: pltpu.make_async_copy(
            yg_hbm.at[_row_slab(rows_ref[k, r])], ybuf.at[s, k, _row_slab(r)], sems.at[s])

    def run(s):
        @pl.when(i == 0)
        def _():
            _start_row_dmas(tc, gather(dest_ref, s))

        @pl.when(i + 1 < pl.num_programs(0))
        def _():
            _start_row_dmas(tc, gather(dest_next_ref, 1 - s))

        _wait_row_dmas(tc, gather(dest_ref, s))
        gate = gate_ref[...]
        hmoe = DEEPNORM_ALPHA * x_ref[...]
        for k in range(TOP_K):
            hmoe = hmoe + gate[:, k:k + 1] * _load_tile_rows(ybuf, tc, (s, k))
        o_ref[...] = _layer_norm(hmoe, g_ref[...], b_ref[...])

    pl.when(i % 2 == 0)(functools.partial(run, 0))
    pl.when(i % 2 == 1)(functools.partial(run, 1))


def _combine_ln(yg, dest, gate_t, x, g, b, tc=512):
    n, d = x.shape
    tc = min(tc, n)
    steps = n // tc
    row = pl.BlockSpec((1, d), lambda i: (0, 0))
    return pl.pallas_call(
        functools.partial(_combine_ln_body, tc),
        grid=(steps,),
        in_specs=[_smem_rows(tc),
                  pl.BlockSpec((TOP_K, tc), lambda i: (0, jnp.minimum(i + 1, steps - 1)), memory_space=pltpu.SMEM),
                  pl.BlockSpec(memory_space=pl.ANY),
                  pl.BlockSpec((tc, TOP_K), lambda i: (i, 0)),
                  pl.BlockSpec((tc, d), lambda i: (i, 0)), row, row],
        out_specs=pl.BlockSpec((tc, d), lambda i: (i, 0)),
        out_shape=jax.ShapeDtypeStruct((n, d), F32),
        scratch_shapes=[pltpu.VMEM((2, TOP_K, tc * ROW_TILES, LANES), F32), pltpu.SemaphoreType.DMA((2,))],
        compiler_params=_cparams(("arbitrary",)),
        name="moe_combine_ln",
    )(dest, dest, yg, gate_t, x, g, b)


def _moe_ln(x, x3, layer, router_w, router_b, w_gu, b_gu, w_down, b_down, ln_g, ln_b):
    n, d = x.shape
    idx, gate, rank, cnt = _router(x, _pad_lanes(router_w), _pad_lanes(router_b[None, :]))
    counts = cnt[:, 0].astype(I32)
    padded = (counts + MOE_ROWS - 1) // MOE_ROWS * MOE_ROWS
    pad_end = jnp.cumsum(padded)
    pad_start = pad_end - padded
    hot = idx[:, :, None] == jnp.arange(N_EXPERTS, dtype=I32)[None, None, :]
    dest = jnp.sum(jnp.where(hot, pad_start[None, None, :], 0), axis=-1) + rank
    n_blocks = -(-(n * TOP_K) // MOE_ROWS) + N_EXPERTS
    block_row0 = jnp.arange(n_blocks, dtype=I32) * MOE_ROWS
    block_expert = jnp.minimum(jnp.sum((pad_end[None, :] <= block_row0[:, None]).astype(I32), axis=1),
                               N_EXPERTS - 1)
    n_used = (pad_end[-1:] // MOE_ROWS).astype(I32)
    fill = jnp.stack([pad_start + counts, pad_end])
    xg = _dispatch(x3, dest, fill, n_blocks * MOE_ROWS)
    yg = _experts(xg, layer, block_expert, n_used, w_gu, b_gu, w_down, b_down)
    return _combine_ln(yg, dest, gate.T, x, ln_g[None, :], ln_b[None, :])


def _even_mixer_ln(x, bsz, seq, w_in, w_out, lam_params, subln_w, conv_w, a_log, dt_bias, gdn_norm_w,
                   t5_bias, lam_init, ln_g, ln_b):
    main = 3 * A_HEADS * LANES + 4 * B_HEADS * B_HEAD_DIM
    n_q = A_HEADS * 2 * A_HEAD_DIM
    col_scale = jnp.concatenate([jnp.full((1, n_q), A_HEAD_DIM ** -0.5 * LOG2E, F32),
                                 jnp.ones((1, main - n_q), F32)], axis=1)
    proj = _proj(x, w_in[:, :main], col_scale)
    zeros4 = jnp.zeros((B_HEADS,), F32)
    p0 = _pad_lanes(jnp.concatenate([zeros4, a_log.astype(F32)])[None, :])
    p1 = _pad_lanes(jnp.concatenate([zeros4, dt_bias.astype(F32)])[None, :])
    gates = _gates("even", x, _pad_lanes(w_in[:, main:]), p0, p1, seq)
    t = min(ATTN_TILE, seq)
    strips = _t5_strips(t5_bias, seq, t)
    ao = _diff_attention(proj, bsz, seq, strips, lam_params.astype(F32), subln_w[None, :].astype(F32),
                         lam_init, t)
    g8 = gates[:, :2 * B_HEADS].reshape(bsz, seq, 2, B_HEADS)
    gates_row = g8.transpose(0, 3, 2, 1).reshape(bsz * B_HEADS, 2, seq)
    cw = conv_w.astype(F32).reshape(CONV_WIDTH, 3, B_HEADS, B_HEAD_DIM).transpose(2, 1, 0, 3)
    cw = cw.reshape(B_HEADS, 3 * CONV_WIDTH, B_HEAD_DIM)
    bo = _gdn(proj, gates, gates_row, cw, gdn_norm_w[None, :].astype(F32), bsz, seq, min(GDN_TILE, seq))
    return _outproj_ln(ao, 0, bo, 0, w_out, x, ln_g[None, :], ln_b[None, :])


def _odd_mixer_ln(x, bsz, seq, w_in, w_out, qk_norm_w, forget_b, ln_g, ln_b):
    main = 4 * C_HEADS * C_HEAD_DIM
    width = C_HEADS * C_HEAD_DIM
    scale = jnp.asarray([C_HEAD_DIM ** -0.5 * LOG2E, 1.0], F32)[:, None]
    wqk = qk_norm_w.astype(F32) * scale
    col_scale = jnp.concatenate([jnp.tile(wqk[0], C_HEADS), jnp.tile(wqk[1], C_HEADS),
                                 jnp.ones((main - 2 * width,), F32)])[None, :]
    proj = _proj(x, w_in[:, :main], col_scale, norm_chunks=range(2 * width // PROJ_CHUNK))
    fb =_pad_lanes(forget_b.astype(F32)[None, :])
    cum = _gates("odd", x, _pad_lanes(w_in[:, main:]), fb, fb, seq)
    cum_t = cum[:, :C_HEADS].reshape(bsz, seq, C_HEADS // 2, 2).transpose(0, 2, 3, 1)
    logit_bound = 1.02 * C_HEAD_DIM * jnp.max(jnp.abs(wqk[0])) * jnp.max(jnp.abs(wqk[1]))
    o = _fox_attention(proj, proj, cum_t, logit_bound, bsz, seq, min(ATTN_TILE, seq))
    return _outproj_ln(o, 0, o, 1, w_out, x, ln_g[None, :], ln_b[None, :])


def kernel(x, t5_bias, even_w_in, even_w_out, diff_lambda, diff_subln_w, gdn_conv_w, gdn_a_log, gdn_dt_bias, gdn_norm_w, odd_w_in, odd_w_out, fox_qk_norm_w, fox_forget_b, router_w, router_b, moe_w_gate_up, moe_b_gate_up, moe_w_down, moe_b_down, ln_mix_g, ln_mix_b, ln_ffn_g, ln_ffn_b):
    bsz, seq, d = x.shape
    xf = x.reshape(bsz * seq, d)
    for layer in range(DEPTH):
        i = layer // 2
        if layer % 2 == 0:
            lam_init = 0.8 - 0.6 * math.exp(-0.3 * layer)
            xf, x3 = _even_mixer_ln(xf, bsz, seq, even_w_in[i], even_w_out[i], diff_lambda[i], diff_subln_w[i],
                                    gdn_conv_w[i], gdn_a_log[i], gdn_dt_bias[i], gdn_norm_w[i], t5_bias, lam_init,
                                    ln_mix_g[layer], ln_mix_b[layer])
        else:
            xf, x3 = _odd_mixer_ln(xf, bsz, seq, odd_w_in[i], odd_w_out[i], fox_qk_norm_w[i], fox_forget_b[i],
                                   ln_mix_g[layer], ln_mix_b[layer])
        xf = _moe_ln(xf, x3, layer, router_w[layer], router_b[layer], moe_w_gate_up, moe_b_gate_up,
                     moe_w_down, moe_b_down, ln_ffn_g[layer], ln_ffn_b[layer])
    return xf.reshape(bsz, seq, d)
```

```python
import functools
import math

import numpy as np
import jax
import jax.numpy as jnp
from jax import lax
from jax.experimental import pallas as pl
from jax.experimental.pallas import tpu as pltpu

F32 = jnp.float32
BF16 = jnp.bfloat16
I32 = jnp.int32

D_MODEL = 1024
DEPTH = 4
A_HEADS = 4
A_HEAD_DIM = 64
B_HEADS = 4
B_HEAD_DIM = 128
CONV_WIDTH = 4
C_HEADS = 16
C_HEAD_DIM = 64
T5_BUCKETS = 32
T5_MAX_DISTANCE = 2048
N_EXPERTS = 32
TOP_K = 4
D_FF = D_MODEL
SWIGLU_LIMIT = 7.0
SWIGLU_ALPHA = 1.702
DEEPNORM_ALPHA = (2 * DEPTH) ** 0.25
LN_EPS = 1e-5
RMS_EPS = 1e-6

LANES = 128
SUBLANES = 8
VMEM_LIMIT = 56 * 1024 * 1024

ATTN_TILE = 1024
ATTN_ROWS = 1024
ATTN_KEYS = 512
GDN_CHUNK = 128
GDN_TILE = 512
GDN_HEADS_PER_STEP = 4
MOE_ROWS = 512
FOX_SKIP_MARGIN = 170.0
NEG_INF = float("-inf")
LOG2E = math.log2(math.e)


def _cparams(sem, vmem=VMEM_LIMIT):
    return pltpu.CompilerParams(dimension_semantics=sem, vmem_limit_bytes=vmem)


def _dot(a, b, **kw):
    return jnp.dot(a, b, preferred_element_type=F32, **kw)


def _dot_nt(a, b, **kw):
    return lax.dot_general(a, b, (((1,), (1,)), ((), ())), preferred_element_type=F32, **kw)


def _dot_tn(a, b, **kw):
    return lax.dot_general(a, b, (((0,), (0,)), ((), ())), preferred_element_type=F32, **kw)


def _sigmoid(x):
    return 1.0 / (1.0 + jnp.exp(-x))


def _softplus(x):
    return jnp.maximum(x, 0.0) + jnp.log(1.0 + jnp.exp(-jnp.abs(x)))


def _layer_norm(xf, g, b):
    mu = jnp.mean(xf, axis=-1, keepdims=True)
    xc = xf - mu
    var = jnp.mean(xc * xc, axis=-1, keepdims=True)
    return xc * lax.rsqrt(var + LN_EPS) * g + b


PROJ_CHUNK = 512


def _proj_body(norm_chunks, x_ref, w_ref, cs_ref, gsum_ref, o_ref):
    xb = x_ref[...].astype(BF16)
    for c in range(w_ref.shape[1] // PROJ_CHUNK):
        cols = slice(c * PROJ_CHUNK, (c + 1) * PROJ_CHUNK)
        acc = _dot(xb, w_ref[:, cols])
        if c in norm_chunks:
            ss = _dot((acc * acc).astype(BF16), gsum_ref[...])
            acc = acc * lax.rsqrt(ss * (1.0 / C_HEAD_DIM) + RMS_EPS)
        o_ref[:, cols] = (acc * cs_ref[:, cols]).astype(o_ref.dtype)


def _proj(x, w, col_scale, norm_chunks=(), tm=512):
    n, k = x.shape
    m = w.shape[1]
    tm = min(tm, n)
    head = jnp.arange(PROJ_CHUNK, dtype=I32) // C_HEAD_DIM
    gsum = (head[:, None] == head[None, :]).astype(BF16)
    return pl.pallas_call(
        functools.partial(_proj_body, tuple(norm_chunks)),
        grid=(n // tm,),
        in_specs=[pl.BlockSpec((tm, k), lambda i: (i, 0)),
                  pl.BlockSpec((k, m), lambda i: (0, 0)),
                  pl.BlockSpec((1, m), lambda i: (0, 0)),
                  pl.BlockSpec((PROJ_CHUNK, PROJ_CHUNK), lambda i: (0, 0))],
        out_specs=pl.BlockSpec((tm, m), lambda i: (i, 0)),
        out_shape=jax.ShapeDtypeStruct((n, m), BF16),
        compiler_params=_cparams(("parallel",)),
        name="proj",
    )(x, w.astype(BF16), col_scale, gsum)


def _split_bf16(a, parts):
    out = []
    for _ in range(parts):
        hi = a.astype(BF16)
        out.append(hi)
        a = a - hi.astype(F32)
    return out


def _tri_sum(tri, a):
    tri = tri.astype(BF16)
    return sum(_dot(tri, part) for part in _split_bf16(a, 3))


def _dot_split(x, w):
    xh, xl = _split_bf16(x, 2)
    wh, wl = _split_bf16(w, 2)
    return _dot(xh, wh) + (_dot(xl, wh) + _dot(xh, wl))


def _gates_body(mode, tm, steps_per_seq, x_ref, w_ref, p0_ref, p1_ref, o_ref, carry_ref):
    z = _dot_split(x_ref[...], w_ref[...])
    ri = lax.broadcasted_iota(I32, (tm, tm), 0)
    ci = lax.broadcasted_iota(I32, (tm, tm), 1)
    if mode == "even":
        lane = lax.broadcasted_iota(I32, (tm, LANES), 1)
        beta = _sigmoid(z)
        g = -jnp.exp(p0_ref[...]) * _softplus(z + p1_ref[...])
        shift = int(math.log2(GDN_CHUNK))
        same_chunk = jnp.right_shift(ri, shift) == jnp.right_shift(ci, shift)
        tri = jnp.where(jnp.logical_and(same_chunk, ci <= ri), 1.0, 0.0)
        gc = _tri_sum(tri, g)
        o_ref[...] = jnp.where(lane < B_HEADS, beta, gc)
    else:
        @pl.when(pl.program_id(0) % steps_per_seq == 0)
        def _():
            carry_ref[...] = jnp.zeros_like(carry_ref)

        logf = -_softplus(-(z + p0_ref[...]))
        tri = jnp.where(ci <= ri, 1.0, 0.0)
        cum = _tri_sum(tri, logf) + carry_ref[0:1, :]
        o_ref[...] = cum
        carry_ref[...] = jnp.broadcast_to(cum[tm - 1:tm, :], carry_ref.shape)


def _gates(mode, x, w_small, p0, p1, seq_len, tm=512):
    n, k = x.shape
    tm = min(tm, seq_len)
    row = pl.BlockSpec((1, LANES), lambda i: (0, 0))
    return pl.pallas_call(
        functools.partial(_gates_body, mode, tm, seq_len // tm),
        grid=(n // tm,),
        in_specs=[pl.BlockSpec((tm, k), lambda i: (i, 0)),
                  pl.BlockSpec((k, LANES), lambda i: (0, 0)), row, row],
        out_specs=pl.BlockSpec((tm, LANES), lambda i: (i, 0)),
        out_shape=jax.ShapeDtypeStruct((n, LANES), F32),
        scratch_shapes=[pltpu.VMEM((SUBLANES, LANES), F32)],
        compiler_params=_cparams(("arbitrary",)),
        name="gates_" + mode,
    )(x, w_small, p0, p1)


def _pad_lanes(a):
    return jnp.pad(a, [(0, 0)] * (a.ndim - 1) + [(0, LANES - a.shape[-1])])


def _attn_body(mode, t, lam_init, *rest):
    if mode == "diff":
        q_ref, k_ref, v_ref, strip_ref, lam_ref, subln_ref, o_ref, qs_ref, m_ref, acc_ref, l_ref, tile_ref = rest
    else:
        jmin_ref, q_ref, k_ref, v_ref, ck_ref, gate_ref, o_ref, qs_ref, m_ref, acc_ref = rest
    i = pl.program_id(2)
    rc = min(ATTN_ROWS, t)

    q = q_ref[...]
    lane = lax.broadcasted_iota(I32, q.shape, 1)
    zero = jnp.zeros_like(q)
    qs_ref[0:t, :] = jnp.where(lane < 64, q, zero)
    qs_ref[t:2 * t, :] = jnp.where(lane < 64, zero, q)
    m_ref[...] = jnp.full(m_ref.shape, NEG_INF, F32)
    acc_ref[...] = jnp.zeros_like(acc_ref)
    if mode == "diff":
        l_ref[...] = jnp.zeros_like(l_ref)

        @pl.when(i == 0)
        def _():
            for d in range(strip_ref.shape[0]):
                x = jnp.broadcast_to(strip_ref[d], (t, 2 * t))
                tile_ref[d] = pltpu.roll(x, t + 1, 1, stride=1, stride_axis=0)[:, :t]
    else:
        c0 = ck_ref[:, pl.ds(pl.multiple_of(i * t, t), LANES)][:, 0:1]

    def step(kind, j, d=0):
        keys = pl.ds(pl.multiple_of(j * t, t), t)
        k = k_ref[keys, :]
        v = v_ref[keys, :]
        if mode == "fox":
            lane = lax.broadcasted_iota(I32, v.shape, 1)
            one = jnp.ones_like(v)
            rhs = (jnp.where(lane < 64, v, one), jnp.where(lane < 64, one, v))
            bias = (c0 - ck_ref[:, keys]) * LOG2E
        if kind == "diag" and t >= 2 * LANES:
            chunks = [(half, h0, t // 2, 0, h0 + t // 2) for half in range(2) for h0 in (0, t // 2)]
        else:
            kc = t if kind == "far" else min(ATTN_KEYS, t)
            chunks = [(r0 // t, r0 % t, rc, k0, kc) for r0 in range(0, 2 * t, rc) for k0 in range(0, t, kc)]
        for half, h0, nr, k0, nk in chunks:
            rows = slice(half * t + h0, half * t + h0 + nr)
            cols = slice(k0, k0 + nk)
            s = _dot_nt(qs_ref[rows, :], k[cols])
            if mode == "fox":
                s = s + bias[half:half + 1, cols]
            elif kind != "far":
                s = s + tile_ref[d, h0:h0 + nr, cols]
            if kind == "diag":
                rr = lax.broadcasted_iota(I32, (nr, nk), 0) + h0
                cc = lax.broadcasted_iota(I32, (nr, nk), 1)
                s = jnp.where(rr >= cc, s, NEG_INF)
            m_prev = m_ref[rows, :]
            m_new = jnp.maximum(m_prev, jnp.max(s, axis=1, keepdims=True))
            alpha = jnp.exp2(m_prev - m_new)
            pr = jnp.exp2(s - pltpu.repeat(m_new, nk // LANES, 1))
            if mode == "diff":
                l_ref[rows, :] = alpha * l_ref[rows, :] + jnp.sum(pr, axis=1, keepdims=True)
                pv = _dot(pr.astype(BF16), v[cols])
            else:
                pv = _dot(pr.astype(BF16), rhs[half][cols])
            acc_ref[rows, :] = alpha * acc_ref[rows, :] + pv
            m_ref[rows, :] = m_new

    def each_block(kind, lo, hi):
        lax.fori_loop(lo, hi, lambda j, carry: (step(kind, j), carry)[1], 0)

    if mode == "diff":
        n_near = strip_ref.shape[0] - 1
        each_block("far", 0, jnp.maximum(i - n_near, 0))
        for d in range(n_near, 0, -1):
            pl.when(i >= d)(functools.partial(step, "near", i - d, d))
    else:
        row = (pl.program_id(0) * pl.num_programs(1) + pl.program_id(1)) * pl.num_programs(2)
        each_block("off", jmin_ref[row + i], i)
    step("diag", i)

    acc = acc_ref[...]
    if mode == "diff":
        o = acc * (1.0 / l_ref[...])
        lp = lam_ref[...]
        lam = (jnp.exp(jnp.sum(lp[0:1] * lp[1:2], axis=-1, keepdims=True))
               - jnp.exp(jnp.sum(lp[2:3] * lp[3:4], axis=-1, keepdims=True)) + lam_init)
        dlt = o[0:t] - lam * o[t:2 * t]
        ms = jnp.mean(dlt * dlt, axis=-1, keepdims=True)
        out = dlt * lax.rsqrt(ms + RMS_EPS) * subln_ref[...] * (1.0 - lam_init)
    else:
        lane = lax.broadcasted_iota(I32, (t, LANES), 1)
        lo = acc[0:t]
        hi = acc[t:2 * t]
        out = jnp.where(lane < 64, lo * (1.0 / lo[:, 64:65]), hi * (1.0 / hi[:, 0:1]))
        out = out * _sigmoid(gate_ref[...].astype(F32))
    o_ref[...] = out.astype(o_ref.dtype)


def _attn_scratch(mode, t, n_tiles=0):
    base = [pltpu.VMEM((2 * t, LANES), BF16), pltpu.VMEM((2 * t, LANES), F32), pltpu.VMEM((2 * t, LANES), F32)]
    if mode == "diff":
        base += [pltpu.VMEM((2 * t, LANES), F32), pltpu.VMEM((n_tiles, t, t), F32)]
    return base


def _diff_attention(proj, bsz, seq, strips, lam_params, subln_w, lam_init, t):
    nq = seq // t
    nd = strips.shape[1]
    h_ = A_HEADS
    return pl.pallas_call(
        functools.partial(_attn_body, "diff", t, lam_init),
        grid=(bsz, h_, nq),
        in_specs=[
            pl.BlockSpec((t, LANES), lambda b, h, i: (b * nq + i, h)),
            pl.BlockSpec((seq, LANES), lambda b, h, i: (b, h_ + h)),
            pl.BlockSpec((seq, LANES), lambda b, h, i: (b, 2 * h_ + h)),
            pl.BlockSpec((None, nd, 1, 2 * t), lambda b, h, i: (h, 0, 0, 0)),
            pl.BlockSpec(lam_params.shape, lambda b, h, i: (0, 0)),
            pl.BlockSpec((1, LANES), lambda b, h, i: (0, 0)),
        ],
        out_specs=pl.BlockSpec((t, LANES), lambda b, h, i: (b * nq + i, h)),
        out_shape=jax.ShapeDtypeStruct((bsz * seq, h_ * LANES), BF16),
        scratch_shapes=_attn_scratch("diff", t, nd),
        compiler_params=_cparams(("parallel", "parallel", "arbitrary")),
        name="diff_attn",
    )(proj, proj, proj, strips, lam_params, subln_w)


def _fox_first_live_block(cum_t, logit_bound, t):
    bsz, hp, _, seq = cum_t.shape
    nq = seq // t
    c_start = cum_t[..., 0::t]
    c_end = cum_t[..., t - 1::t]
    gap = jnp.max(c_start[..., :, None] - c_end[..., None, :], axis=2) * LOG2E
    dead = gap < -(FOX_SKIP_MARGIN + 2.0 * logit_bound)
    dead = jnp.logical_and(dead, jnp.arange(nq)[None, :] < jnp.arange(nq)[:, None])
    return jnp.sum(jnp.cumprod(dead.astype(I32), axis=-1), axis=-1).reshape(-1)


def _fox_attention(qk, proj, cum_t, logit_bound, bsz, seq, t):
    nq = seq // t
    hp = C_HEADS // 2
    jmin = _fox_first_live_block(cum_t, logit_bound, t)
    grid_spec = pltpu.PrefetchScalarGridSpec(
        num_scalar_prefetch=1,
        grid=(bsz, hp, nq),
        in_specs=[
            pl.BlockSpec((t, LANES), lambda b, h, i, jm: (b * nq + i, h)),
            pl.BlockSpec((seq, LANES), lambda b, h, i, jm: (b, hp + h)),
            pl.BlockSpec((seq, LANES), lambda b, h, i, jm: (b, 2 * hp + h)),
            pl.BlockSpec((None, None, 2, seq), lambda b, h, i, jm: (b, h, 0, 0)),
            pl.BlockSpec((t, LANES), lambda b, h, i, jm: (b * nq + i, 3 * hp + h)),
        ],
        out_specs=pl.BlockSpec((t, LANES), lambda b, h, i, jm: (b * nq + i, h)),
        scratch_shapes=_attn_scratch("fox", t),
    )
    return pl.pallas_call(
        functools.partial(_attn_body, "fox", t, 0.0),
        grid_spec=grid_spec,
        out_shape=jax.ShapeDtypeStruct((bsz * seq, hp * LANES), BF16),
        compiler_params=_cparams(("parallel", "parallel", "arbitrary")),
        name="fox_attn",
    )(jmin, qk, qk, proj, cum_t, proj)


def _t5_bias_by_distance(t5_bias, seq):
    n = jnp.arange(seq, dtype=I32)
    max_exact = T5_BUCKETS // 2
    nf = jnp.maximum(n, 1).astype(F32)
    large = max_exact + (jnp.log(nf / max_exact) / math.log(T5_MAX_DISTANCE / max_exact)
                         * (T5_BUCKETS - max_exact)).astype(I32)
    large = jnp.minimum(large, T5_BUCKETS - 1)
    bucket = jnp.where(n < max_exact, n, large)
    return t5_bias.astype(F32).T[:, bucket]


def _t5_strips(t5_bias, seq, t):
    last = t5_bias.astype(F32).T[:, T5_BUCKETS - 1:T5_BUCKETS]
    vec = (_t5_bias_by_distance(t5_bias, seq) - last) * LOG2E
    nd = 1
    while nd * t - (t - 1) < T5_MAX_DISTANCE and nd < seq // t:
        nd += 1
    d = np.arange(nd)[:, None]
    c = np.arange(2 * t)[None, :]
    rel = d * t + t - 1 - c
    rel = np.clip(rel, 0, seq - 1)
    strips = vec[:, jnp.asarray(rel.astype(np.int32))]
    return strips[:, :, None, :]


def _gdn_body(tb, xq_ref, xk_ref, xv_ref, z_ref, gcol_ref, grow_ref, cw_ref, nw_ref, o_ref, s_ref, carry_ref):
    @pl.when(pl.program_id(2) == 0)
    def _():
        s_ref[...] = jnp.zeros_like(s_ref)
        carry_ref[...] = jnp.zeros_like(carry_ref)

    n_heads = GDN_HEADS_PER_STEP
    c_ = GDN_CHUNK
    nc = tb // c_
    gates = gcol_ref[...]
    lane = lax.broadcasted_iota(I32, gates.shape, 1)

    def conv_silu(g, idx, x_ref):
        x = x_ref[:, g * LANES:(g + 1) * LANES].astype(F32)
        xe = jnp.concatenate([carry_ref[g, idx], x], axis=0)
        taps = cw_ref[g, 4 * idx:4 * idx + 4, :]
        y = taps[3:4] * x
        for tap in range(CONV_WIDTH - 1):
            y = y + taps[tap:tap + 1] * pltpu.roll(xe, CONV_WIDTH - 1 - tap, 0)[SUBLANES:]
        carry_ref[g, idx] = x[tb - SUBLANES:]
        return y * _sigmoid(y)

    q_l, k_l, v_l, beta_l, gc_l, gcr_l = [], [], [], [], [], []
    for g in range(n_heads):
        h = pl.program_id(1) * n_heads + g
        q = conv_silu(g, 0, xq_ref)
        k = conv_silu(g, 1, xk_ref)
        v_l.append(conv_silu(g, 2, xv_ref))
        q_l.append(q * lax.rsqrt(jnp.sum(q * q, axis=-1, keepdims=True) + RMS_EPS) * (B_HEAD_DIM ** -0.5))
        k_l.append(k * lax.rsqrt(jnp.sum(k * k, axis=-1, keepdims=True) + RMS_EPS))
        beta_l.append(jnp.sum(jnp.where(lane == h, gates, 0.0), axis=1, keepdims=True))
        gc_l.append(jnp.sum(jnp.where(lane == B_HEADS + h, gates, 0.0), axis=1, keepdims=True))
        gc_row = grow_ref[g, 1:2, :]
        gcr_l.extend(gc_row[None, :, c * c_:(c + 1) * c_] for c in range(nc))

    def chunked(xs):
        return jnp.concatenate([x.reshape(nc, c_, x.shape[-1]) for x in xs], axis=0)

    def bdot(a, b):
        return lax.dot_general(a.astype(BF16), b.astype(BF16), (((2,), (1,)), ((0,), (0,))),
                               preferred_element_type=F32)

    def bdot_nt(a, b):
        return lax.dot_general(a.astype(BF16), b.astype(BF16), (((2,), (2,)), ((0,), (0,))),
                               preferred_element_type=F32)

    q3, k3, v3, beta3, gc3 = chunked(q_l), chunked(k_l), chunked(v_l), chunked(beta_l), chunked(gc_l)
    gcr3 = jnp.concatenate(gcr_l, axis=0)
    ri = lax.broadcasted_iota(I32, (1, c_, c_), 1)
    ci = lax.broadcasted_iota(I32, (1, c_, c_), 2)
    decay = jnp.exp(jnp.where(ri >= ci, gc3 - gcr3, NEG_INF))
    egc = jnp.exp(gc3)
    kb3 = k3 * beta3
    rhs3 = jnp.concatenate([v3 * beta3, kb3 * egc], axis=2)
    g_last = gc3[:, c_ - 1:c_, :]
    qd3 = q3 * egc
    kd3 = k3 * jnp.exp(g_last - gc3)
    gl3 = jnp.exp(g_last)
    kk = bdot_nt(kb3, k3)
    qk3 = bdot_nt(q3, k3) * decay
    x = -(kk * jnp.where(ri > ci, decay, 0.0))
    r = x
    pw = x
    for _ in range(int(math.log2(c_)) - 1):
        pw = bdot(pw, pw)
        r = r + pw + bdot(r, pw)
    sol = rhs3 + bdot(r, rhs3)

    states = [s_ref[g] for g in range(n_heads)]
    outs = [[] for _ in range(n_heads)]
    for c in range(nc):
        for g in range(n_heads):
            n = g * nc + c
            sb = states[g].astype(BF16)
            v_new = sol[n, :, :B_HEAD_DIM] - _dot(sol[n, :, B_HEAD_DIM:].astype(BF16), sb)
            vnb = v_new.astype(BF16)
            outs[g].append(_dot(qd3[n].astype(BF16), sb) + _dot(qk3[n].astype(BF16), vnb))
            states[g] = states[g] * gl3[n] + _dot_tn(kd3[n].astype(BF16), vnb)
    for g in range(n_heads):
        cols = slice(g * LANES, (g + 1) * LANES)
        s_ref[g] = states[g]
        o = jnp.concatenate(outs[g], axis=0)
        o = o * lax.rsqrt(jnp.mean(o * o, axis=-1, keepdims=True) + RMS_EPS) * nw_ref[...]
        z = z_ref[:, cols].astype(F32)
        o_ref[:, cols] = (o * (z * _sigmoid(z))).astype(o_ref.dtype)


def _gdn(proj, gates, gates_row, conv_w, norm_w, bsz, seq, tb):
    nt = seq // tb
    h_ = B_HEADS
    gh = GDN_HEADS_PER_STEP
    col0 = 3 * A_HEADS
    assert h_ % gh == 0 and col0 % gh == 0
    blk = lambda off: pl.BlockSpec((tb, gh * LANES), lambda b, h, s: (b * nt + s, (col0 + off * h_) // gh + h))
    return pl.pallas_call(
        functools.partial(_gdn_body, tb),
        grid=(bsz, h_ // gh, nt),
        in_specs=[blk(0), blk(1), blk(2), blk(3),
                  pl.BlockSpec((tb, LANES), lambda b, h, s: (b * nt + s, 0)),
                  pl.BlockSpec((gh, 2, tb), lambda b, h, s: (b * (h_ // gh) + h, 0, s)),
                  pl.BlockSpec((gh, 3 * CONV_WIDTH, LANES), lambda b, h, s: (h, 0, 0)),
                  pl.BlockSpec((1, LANES), lambda b, h, s: (0, 0))],
        out_specs=pl.BlockSpec((tb, gh * LANES), lambda b, h, s: (b * nt + s, h)),
        out_shape=jax.ShapeDtypeStruct((bsz * seq, h_ * LANES), BF16),
        scratch_shapes=[pltpu.VMEM((gh, B_HEAD_DIM, B_HEAD_DIM), F32),
                        pltpu.VMEM((gh, 3, SUBLANES, LANES), F32)],
        compiler_params=_cparams(("parallel", "parallel", "arbitrary")),
        name="gdn",
    )(proj, proj, proj, proj, gates, gates_row, conv_w, norm_w)


ROW_TILES = D_MODEL // LANES


def _store_tile_rows(ref, y):
    t = y.shape[0]
    for c in range(ROW_TILES):
        ref[pl.ds(c, t, stride=ROW_TILES), :] = y[:, c * LANES:(c + 1) * LANES]


def _load_tile_rows(ref, t, lead=()):
    return jnp.concatenate([ref[lead + (pl.ds(c, t, stride=ROW_TILES), slice(None))] for c in range(ROW_TILES)],
                           axis=1)


def _outproj_ln_body(a_ref, b_ref, w_ref, x_ref, g_ref, bb_ref, o_ref, o3_ref, wb_ref):
    @pl.when(pl.program_id(0) == 0)
    def _():
        wb_ref[...] = w_ref[...].astype(BF16)

    half = a_ref.shape[1]
    hmix = _dot(a_ref[...], wb_ref[0:half, :]) + _dot(b_ref[...], wb_ref[half:, :])
    out = _layer_norm(DEEPNORM_ALPHA * x_ref[...] + hmix, g_ref[...], bb_ref[...])
    o_ref[...] = out
    _store_tile_rows(o3_ref, out)


def _outproj_ln(a, a_blk, b, b_blk, w, x, g, bb, tm=512):
    n, d = x.shape
    tm = min(tm, n)
    half = d // 2
    row = pl.BlockSpec((1, d), lambda i: (0, 0))
    return pl.pallas_call(
        _outproj_ln_body,
        grid=(n // tm,),
        in_specs=[pl.BlockSpec((tm, half), lambda i: (i, a_blk)),
                  pl.BlockSpec((tm, half), lambda i: (i, b_blk)),
                  pl.BlockSpec((d, d), lambda i: (0, 0)),
                  pl.BlockSpec((tm, d), lambda i: (i, 0)), row, row],
        out_specs=[pl.BlockSpec((tm, d), lambda i: (i, 0)),
                   pl.BlockSpec((tm * ROW_TILES, LANES), lambda i: (i, 0))],
        out_shape=[jax.ShapeDtypeStruct((n, d), F32), jax.ShapeDtypeStruct((n * ROW_TILES, LANES), F32)],
        scratch_shapes=[pltpu.VMEM((d, d), BF16)],
        compiler_params=_cparams(("arbitrary",)),
        name="outproj_ln",
    )(a, b, w, x, g, bb)


def _router_body(tm, x_ref, w_ref, b_ref, idx_ref, gate_ref, rank_ref, cnt_ref, carry_ref):
    @pl.when(pl.program_id(0) == 0)
    def _():
        carry_ref[...] = jnp.zeros_like(carry_ref)

    logits = _dot_split(x_ref[...], w_ref[...]) + b_ref[...]
    lg = jnp.transpose(logits)[0:N_EXPERTS, :]
    e_iota = lax.broadcasted_iota(I32, (N_EXPERTS, tm), 0).astype(F32)
    vals, idxs, hots = [], [], []
    for _ in range(TOP_K):
        m = jnp.max(lg, axis=0, keepdims=True)
        idx = jnp.min(jnp.where(lg == m, e_iota, float(N_EXPERTS)), axis=0, keepdims=True)
        hot = e_iota == idx
        lg = jnp.where(hot, NEG_INF, lg)
        vals.append(m)
        idxs.append(idx)
        hots.append(hot)
    es = [jnp.exp(v - vals[0]) for v in vals]
    den = es[0] + es[1] + es[2] + es[3]
    sel = jnp.zeros((N_EXPERTS, tm), F32)
    for hot in hots:
        sel = sel + jnp.where(hot, 1.0, 0.0)
    before = jnp.where(lax.broadcasted_iota(I32, (tm, tm), 0) < lax.broadcasted_iota(I32, (tm, tm), 1),
                       1.0, 0.0).astype(BF16)
    cum = _dot(sel.astype(BF16), before) + carry_ref[:, 0:1]
    ranks = [jnp.sum(jnp.where(hot, cum, 0.0), axis=0, keepdims=True) for hot in hots]
    total = carry_ref[...] + jnp.sum(sel, axis=1, keepdims=True)
    carry_ref[...] = total
    idx_ref[...] = jnp.concatenate(idxs, axis=0).astype(I32)
    gate_ref[...] = jnp.concatenate([e / den for e in es], axis=0)
    rank_ref[...] = jnp.concatenate(ranks, axis=0).astype(I32)
    cnt_ref[...] = total


def _router(x, w_pad, b_pad, tm=512):
    n, d = x.shape
    tm = min(tm, n)
    out4 = lambda dt: jax.ShapeDtypeStruct((TOP_K, n), dt)
    blk4 = pl.BlockSpec((TOP_K, tm), lambda i: (0, i))
    return pl.pallas_call(
        functools.partial(_router_body, tm),
        grid=(n // tm,),
        in_specs=[pl.BlockSpec((tm, d), lambda i: (i, 0)),
                  pl.BlockSpec((d, LANES), lambda i: (0, 0)),
                  pl.BlockSpec((1, LANES), lambda i: (0, 0))],
        out_specs=[blk4, blk4, blk4, pl.BlockSpec((N_EXPERTS, LANES), lambda i: (0, 0))],
        out_shape=[out4(I32), out4(F32), out4(I32), jax.ShapeDtypeStruct((N_EXPERTS, LANES), F32)],
        scratch_shapes=[pltpu.VMEM((N_EXPERTS, LANES), F32)],
        compiler_params=_cparams(("arbitrary",)),
        name="moe_router",
    )(x, w_pad, b_pad)


def _row_slab(row):
    return pl.ds(pl.multiple_of(row * ROW_TILES, ROW_TILES), ROW_TILES)


def _start_row_dma(copy, k):
    copy.start(priority=k % 2)


def _start_row_dmas(n_rows, make_copy):
    def issue(r, carry):
        for k in range(TOP_K):
            _start_row_dma(make_copy(k, r), k)
        return carry

    lax.fori_loop(0, n_rows, issue, 0, unroll=4)


def _wait_row_dmas(n_rows, make_copy):
    def drain(r, carry):
        for k in range(TOP_K):
            make_copy(k, r).wait()
        return carry

    lax.fori_loop(0, n_rows, drain, 0, unroll=8)


def _row_dma_loops(n_rows, make_copy):
    _start_row_dmas(n_rows, make_copy)
    _wait_row_dmas(n_rows, make_copy)


def _dispatch_body(td, dest_ref, fill_ref, x3_ref, xg_out, zero_ref, sem, zsem):
    @pl.when(pl.program_id(0) == 0)
    def _():
        zero_ref[...] = jnp.zeros_like(zero_ref)

        def zero_copy(r):
            return pltpu.make_async_copy(zero_ref, xg_out.at[_row_slab(r)], zsem)

        def each_expert(fn):
            def expert(e, carry):
                lax.fori_loop(fill_ref[0, e], fill_ref[1, e], lambda r, c: (fn(r), c)[1], 0)
                return carry
            lax.fori_loop(0, N_EXPERTS, expert, 0)

        each_expert(lambda r: zero_copy(r).start())
        each_expert(lambda r: zero_copy(r).wait())

    _row_dma_loops(td, lambda k, r: pltpu.make_async_copy(
        x3_ref.at[_row_slab(r)], xg_out.at[_row_slab(dest_ref[k, r])], sem))


def _smem_rows(tile):
    return pl.BlockSpec((TOP_K, tile), lambda i: (0, i), memory_space=pltpu.SMEM)


def _dispatch(x3, dest, fill, n_rows, td=1024):
    n = x3.shape[0] // ROW_TILES
    td = min(td, n)
    return pl.pallas_call(
        functools.partial(_dispatch_body, td),
        grid=(n // td,),
        in_specs=[_smem_rows(td),
                  pl.BlockSpec(memory_space=pltpu.SMEM),
                  pl.BlockSpec((td * ROW_TILES, LANES), lambda i: (i, 0))],
        out_specs=pl.BlockSpec(memory_space=pl.ANY),
        out_shape=jax.ShapeDtypeStruct((n_rows * ROW_TILES, LANES), x3.dtype),
        scratch_shapes=[pltpu.VMEM((ROW_TILES, LANES), x3.dtype), pltpu.SemaphoreType.DMA(()),
                        pltpu.SemaphoreType.DMA(())],
        compiler_params=_cparams(("arbitrary",)),
        name="moe_dispatch",
    )(dest, fill, x3)


def _expert_body(be_ref, nu_ref, x_ref, wgu_ref, bgu_ref, wd_ref, bd_ref, o_ref, wgu_b, wd_b):
    i = pl.program_id(0)
    prev = be_ref[jnp.maximum(i - 1, 0)]
    fresh = jnp.logical_or(i == 0, be_ref[i] != prev)

    @pl.when(jnp.logical_and(i < nu_ref[0], fresh))
    def _():
        wgu_b[...] = wgu_ref[...].astype(BF16)
        wd_b[...] = wd_ref[...].astype(BF16)

    @pl.when(i < nu_ref[0])
    def _():
        x = _load_tile_rows(x_ref, MOE_ROWS).astype(BF16)
        hcat = _dot(x, wgu_b[...]) + bgu_ref[...]
        g = jnp.minimum(hcat[:, :D_FF], SWIGLU_LIMIT)
        u = jnp.clip(hcat[:, D_FF:], -SWIGLU_LIMIT, SWIGLU_LIMIT)
        act = g * _sigmoid(SWIGLU_ALPHA * g) * (u + 1.0)
        _store_tile_rows(o_ref, _dot(act.astype(BF16), wd_b[...]) + bd_ref[...])

    @pl.when(i >= nu_ref[0])
    def _():
        o_ref[...] = jnp.zeros_like(o_ref)


def _experts(xg, layer, block_expert, n_used, w_gu, b_gu, w_down, b_down):
    d = D_MODEL
    nb = xg.shape[0] // (MOE_ROWS * ROW_TILES)
    blk = pl.BlockSpec((MOE_ROWS * ROW_TILES, LANES), lambda i, be, nu: (i, 0))
    grid_spec = pltpu.PrefetchScalarGridSpec(
        num_scalar_prefetch=2,
        grid=(nb,),
        in_specs=[
            pl.BlockSpec((MOE_ROWS * ROW_TILES, LANES), lambda i, be, nu: (jnp.minimum(i, nu[0] - 1), 0)),
            pl.BlockSpec((None, None, d, 2 * D_FF), lambda i, be, nu: (layer, be[i], 0, 0)),
            pl.BlockSpec((None, None, 1, 2 * D_FF), lambda i, be, nu: (layer, be[i], 0, 0)),
            pl.BlockSpec((None, None, D_FF, d), lambda i, be, nu: (layer, be[i], 0, 0)),
            pl.BlockSpec((None, None, 1, d), lambda i, be, nu: (layer, be[i], 0, 0)),
        ],
        out_specs=blk,
        scratch_shapes=[pltpu.VMEM((d, 2 * D_FF), BF16), pltpu.VMEM((D_FF, d), BF16)],
    )
    return pl.pallas_call(
        _expert_body,
        grid_spec=grid_spec,
        out_shape=jax.ShapeDtypeStruct(xg.shape, F32),
        compiler_params=_cparams(("arbitrary",)),
        name="moe_experts",
    )(block_expert, n_used, xg, w_gu, b_gu[:, :, None, :], w_down, b_down[:, :, None, :])


def _combine_ln_body(tc, dest_ref, dest_next_ref, yg_hbm, gate_ref, x_ref, g_ref, b_ref, o_ref, ybuf, sems):
    i = pl.program_id(0)

    def gather(rows_ref, s):
        return lambda k, r: pltpu.make_async_copy(
            yg_hbm.at[_row_slab(rows_ref[k, r])], ybuf.at[s, k, _row_slab(r)], sems.at[s])

    def run(s):
        @pl.when(i == 0)
        def _():
            _start_row_dmas(tc, gather(dest_ref, s))

        @pl.when(i + 1 < pl.num_programs(0))
        def _():
            _start_row_dmas(tc, gather(dest_next_ref, 1 - s))

        _wait_row_dmas(tc, gather(dest_ref, s))
        gate = gate_ref[...]
        hmoe = DEEPNORM_ALPHA * x_ref[...]
        for k in range(TOP_K):
            hmoe = hmoe + gate[:, k:k + 1] * _load_tile_rows(ybuf, tc, (s, k))
        o_ref[...] = _layer_norm(hmoe, g_ref[...], b_ref[...])

    pl.when(i % 2 == 0)(functools.partial(run, 0))
    pl.when(i % 2 == 1)(functools.partial(run, 1))


def _combine_ln(yg, dest, gate_t, x, g, b, tc=512):
    n, d = x.shape
    tc = min(tc, n)
    steps = n // tc
    row = pl.BlockSpec((1, d), lambda i: (0, 0))
    return pl.pallas_call(
        functools.partial(_combine_ln_body, tc),
        grid=(steps,),
        in_specs=[_smem_rows(tc),
                  pl.BlockSpec((TOP_K, tc), lambda i: (0, jnp.minimum(i + 1, steps - 1)), memory_space=pltpu.SMEM),
                  pl.BlockSpec(memory_space=pl.ANY),
                  pl.BlockSpec((tc, TOP_K), lambda i: (i, 0)),
                  pl.BlockSpec((tc, d), lambda i: (i, 0)), row, row],
        out_specs=pl.BlockSpec((tc, d), lambda i: (i, 0)),
        out_shape=jax.ShapeDtypeStruct((n, d), F32),
        scratch_shapes=[pltpu.VMEM((2, TOP_K, tc * ROW_TILES, LANES), F32), pltpu.SemaphoreType.DMA((2,))],
        compiler_params=_cparams(("arbitrary",)),
        name="moe_combine_ln",
    )(dest, dest, yg, gate_t, x, g, b)


def _moe_ln(x, x3, layer, router_w, router_b, w_gu, b_gu, w_down, b_down, ln_g, ln_b):
    n, d = x.shape
    idx, gate, rank, cnt = _router(x, _pad_lanes(router_w), _pad_lanes(router_b[None, :]))
    counts = cnt[:, 0].astype(I32)
    padded = (counts + MOE_ROWS - 1) // MOE_ROWS * MOE_ROWS
    pad_end = jnp.cumsum(padded)
    pad_start = pad_end - padded
    hot = idx[:, :, None] == jnp.arange(N_EXPERTS, dtype=I32)[None, None, :]
    dest = jnp.sum(jnp.where(hot, pad_start[None, None, :], 0), axis=-1) + rank
    n_blocks = -(-(n * TOP_K) // MOE_ROWS) + N_EXPERTS
    block_row0 = jnp.arange(n_blocks, dtype=I32) * MOE_ROWS
    block_expert = jnp.minimum(jnp.sum((pad_end[None, :] <= block_row0[:, None]).astype(I32), axis=1),
                               N_EXPERTS - 1)
    n_used = (pad_end[-1:] // MOE_ROWS).astype(I32)
    fill = jnp.stack([pad_start + counts, pad_end])
    xg = _dispatch(x3, dest, fill, n_blocks * MOE_ROWS)
    yg = _experts(xg, layer, block_expert, n_used, w_gu, b_gu, w_down, b_down)
    return _combine_ln(yg, dest, gate.T, x, ln_g[None, :], ln_b[None, :])


def _even_mixer_ln(x, bsz, seq, w_in, w_out, lam_params, subln_w, conv_w, a_log, dt_bias, gdn_norm_w,
                   t5_bias, lam_init, ln_g, ln_b):
    main = 3 * A_HEADS * LANES + 4 * B_HEADS * B_HEAD_DIM
    n_q = A_HEADS * 2 * A_HEAD_DIM
    col_scale = jnp.concatenate([jnp.full((1, n_q), A_HEAD_DIM ** -0.5 * LOG2E, F32),
                                 jnp.ones((1, main - n_q), F32)], axis=1)
    proj = _proj(x, w_in[:, :main], col_scale)
    zeros4 = jnp.zeros((B_HEADS,), F32)
    p0 = _pad_lanes(jnp.concatenate([zeros4, a_log.astype(F32)])[None, :])
    p1 = _pad_lanes(jnp.concatenate([zeros4, dt_bias.astype(F32)])[None, :])
    gates = _gates("even", x, _pad_lanes(w_in[:, main:]), p0, p1, seq)
    t = min(ATTN_TILE, seq)
    strips = _t5_strips(t5_bias, seq, t)
    ao = _diff_attention(proj, bsz, seq, strips, lam_params.astype(F32), subln_w[None, :].astype(F32),
                         lam_init, t)
    g8 = gates[:, :2 * B_HEADS].reshape(bsz, seq, 2, B_HEADS)
    gates_row = g8.transpose(0, 3, 2, 1).reshape(bsz * B_HEADS, 2, seq)
    cw = conv_w.astype(F32).reshape(CONV_WIDTH, 3, B_HEADS, B_HEAD_DIM).transpose(2, 1, 0, 3)
    cw = cw.reshape(B_HEADS, 3 * CONV_WIDTH, B_HEAD_DIM)
    bo = _gdn(proj, gates, gates_row, cw, gdn_norm_w[None, :].astype(F32), bsz, seq, min(GDN_TILE, seq))
    return _outproj_ln(ao, 0, bo, 0, w_out, x, ln_g[None, :], ln_b[None, :])


def _odd_mixer_ln(x, bsz, seq, w_in, w_out, qk_norm_w, forget_b, ln_g, ln_b):
    main = 4 * C_HEADS * C_HEAD_DIM
    width = C_HEADS * C_HEAD_DIM
    scale = jnp.asarray([C_HEAD_DIM ** -0.5 * LOG2E, 1.0], F32)[:, None]
    wqk = qk_norm_w.astype(F32) * scale
    col_scale = jnp.concatenate([jnp.tile(wqk[0], C_HEADS), jnp.tile(wqk[1], C_HEADS),
                                 jnp.ones((main - 2 * width,), F32)])[None, :]
    proj = _proj(x, w_in[:, :main], col_scale, norm_chunks=range(2 * width // PROJ_CHUNK))
    fb =_pad_lanes(forget_b.astype(F32)[None, :])
    cum = _gates("odd", x, _pad_lanes(w_in[:, main:]), fb, fb, seq)
    cum_t = cum[:, :C_HEADS].reshape(bsz, seq, C_HEADS // 2, 2).transpose(0, 2, 3, 1)
    logit_bound = 1.02 * C_HEAD_DIM * jnp.max(jnp.abs(wqk[0])) * jnp.max(jnp.abs(wqk[1]))
    o = _fox_attention(proj, proj, cum_t, logit_bound, bsz, seq, min(ATTN_TILE, seq))
    return _outproj_ln(o, 0, o, 1, w_out, x, ln_g[None, :], ln_b[None, :])


def kernel(x, t5_bias, even_w_in, even_w_out, diff_lambda, diff_subln_w, gdn_conv_w, gdn_a_log, gdn_dt_bias, gdn_norm_w, odd_w_in, odd_w_out, fox_qk_norm_w, fox_forget_b, router_w, router_b, moe_w_gate_up, moe_b_gate_up, moe_w_down, moe_b_down, ln_mix_g, ln_mix_b, ln_ffn_g, ln_ffn_b):
    bsz, seq, d = x.shape
    xf = x.reshape(bsz * seq, d)
    for layer in range(DEPTH):
        i = layer // 2
        if layer % 2 == 0:
            lam_init = 0.8 - 0.6 * math.exp(-0.3 * layer)
            xf, x3 = _even_mixer_ln(xf, bsz, seq, even_w_in[i], even_w_out[i], diff_lambda[i], diff_subln_w[i],
                                    gdn_conv_w[i], gdn_a_log[i], gdn_dt_bias[i], gdn_norm_w[i], t5_bias, lam_init,
                                    ln_mix_g[layer], ln_mix_b[layer])
        else:
            xf, x3 = _odd_mixer_ln(xf, bsz, seq, odd_w_in[i], odd_w_out[i], fox_qk_norm_w[i], fox_forget_b[i],
                                   ln_mix_g[layer], ln_mix_b[layer])
        xf = _moe_ln(xf, x3, layer, router_w[layer], router_b[layer], moe_w_gate_up, moe_b_gate_up,
                     moe_w_down, moe_b_down, ln_ffn_g[layer], ln_ffn_b[layer])
    return xf.reshape(bsz, seq, d)
```

```python
import functools
import math

import numpy as np
import jax
import jax.numpy as jnp
from jax import lax
from jax.experimental import pallas as pl
from jax.experimental.pallas import tpu as pltpu

F32 = jnp.float32
BF16 = jnp.bfloat16
I32 = jnp.int32

D_MODEL = 1024
DEPTH = 4
A_HEADS = 4
A_HEAD_DIM = 64
B_HEADS = 4
B_HEAD_DIM = 128
CONV_WIDTH = 4
C_HEADS = 16
C_HEAD_DIM = 64
T5_BUCKETS = 32
T5_MAX_DISTANCE = 2048
N_EXPERTS = 32
TOP_K = 4
D_FF = D_MODEL
SWIGLU_LIMIT = 7.0
SWIGLU_ALPHA = 1.702
DEEPNORM_ALPHA = (2 * DEPTH) ** 0.25
LN_EPS = 1e-5
RMS_EPS = 1e-6

LANES = 128
SUBLANES = 8
VMEM_LIMIT = 56 * 1024 * 1024

ATTN_TILE = 1024
ATTN_ROWS = 1024
ATTN_KEYS = 512
GDN_CHUNK = 128
GDN_TILE = 512
GDN_HEADS_PER_STEP = 4
MOE_ROWS = 512
FOX_SKIP_MARGIN = 170.0
NEG_INF = float("-inf")
LOG2E = math.log2(math.e)


def _cparams(sem, vmem=VMEM_LIMIT):
    return pltpu.CompilerParams(dimension_semantics=sem, vmem_limit_bytes=vmem)


def _dot(a, b, **kw):
    return jnp.dot(a, b, preferred_element_type=F32, **kw)


def _dot_nt(a, b, **kw):
    return lax.dot_general(a, b, (((1,), (1,)), ((), ())), preferred_element_type=F32, **kw)


def _dot_tn(a, b, **kw):
    return lax.dot_general(a, b, (((0,), (0,)), ((), ())), preferred_element_type=F32, **kw)


def _sigmoid(x):
    return 1.0 / (1.0 + jnp.exp(-x))


def _softplus(x):
    return jnp.maximum(x, 0.0) + jnp.log(1.0 + jnp.exp(-jnp.abs(x)))


def _layer_norm(xf, g, b):
    mu = jnp.mean(xf, axis=-1, keepdims=True)
    xc = xf - mu
    var = jnp.mean(xc * xc, axis=-1, keepdims=True)
    return xc * lax.rsqrt(var + LN_EPS) * g + b


PROJ_CHUNK = 512


def _proj_body(norm_chunks, x_ref, w_ref, cs_ref, gsum_ref, o_ref):
    xb = x_ref[...].astype(BF16)
    for c in range(w_ref.shape[1] // PROJ_CHUNK):
        cols = slice(c * PROJ_CHUNK, (c + 1) * PROJ_CHUNK)
        acc = _dot(xb, w_ref[:, cols])
        if c in norm_chunks:
            ss = _dot((acc * acc).astype(BF16), gsum_ref[...])
            acc = acc * lax.rsqrt(ss * (1.0 / C_HEAD_DIM) + RMS_EPS)
        o_ref[:, cols] = (acc * cs_ref[:, cols]).astype(o_ref.dtype)


def _proj(x, w, col_scale, norm_chunks=(), tm=1024):
    n, k = x.shape
    m = w.shape[1]
    tm = min(tm, n)
    head = jnp.arange(PROJ_CHUNK, dtype=I32) // C_HEAD_DIM
    gsum = (head[:, None] == head[None, :]).astype(BF16)
    return pl.pallas_call(
        functools.partial(_proj_body, tuple(norm_chunks)),
        grid=(n // tm,),
        in_specs=[pl.BlockSpec((tm, k), lambda i: (i, 0)),
                  pl.BlockSpec((k, m), lambda i: (0, 0)),
                  pl.BlockSpec((1, m), lambda i: (0, 0)),
                  pl.BlockSpec((PROJ_CHUNK, PROJ_CHUNK), lambda i: (0, 0))],
        out_specs=pl.BlockSpec((tm, m), lambda i: (i, 0)),
        out_shape=jax.ShapeDtypeStruct((n, m), BF16),
        compiler_params=_cparams(("parallel",)),
        name="proj",
    )(x, w.astype(BF16), col_scale, gsum)


def _split_bf16(a, parts):
    out = []
    for _ in range(parts):
        hi = a.astype(BF16)
        out.append(hi)
        a = a - hi.astype(F32)
    return out


def _tri_sum(tri, a):
    tri = tri.astype(BF16)
    return sum(_dot(tri, part) for part in _split_bf16(a, 3))


def _dot_split(x, w):
    xh, xl = _split_bf16(x, 2)
    wh, wl = _split_bf16(w, 2)
    return _dot(xh, wh) + (_dot(xl, wh) + _dot(xh, wl))


def _gates_body(mode, tm, steps_per_seq, x_ref, w_ref, p0_ref, p1_ref, o_ref, carry_ref):
    z = _dot_split(x_ref[...], w_ref[...])
    ri = lax.broadcasted_iota(I32, (tm, tm), 0)
    ci = lax.broadcasted_iota(I32, (tm, tm), 1)
    if mode == "even":
        lane = lax.broadcasted_iota(I32, (tm, LANES), 1)
        beta = _sigmoid(z)
        g = -jnp.exp(p0_ref[...]) * _softplus(z + p1_ref[...])
        shift = int(math.log2(GDN_CHUNK))
        same_chunk = jnp.right_shift(ri, shift) == jnp.right_shift(ci, shift)
        tri = jnp.where(jnp.logical_and(same_chunk, ci <= ri), 1.0, 0.0)
        gc = _tri_sum(tri, g)
        o_ref[...] = jnp.where(lane < B_HEADS, beta, gc)
    else:
        @pl.when(pl.program_id(0) % steps_per_seq == 0)
        def _():
            carry_ref[...] = jnp.zeros_like(carry_ref)

        logf = -_softplus(-(z + p0_ref[...]))
        tri = jnp.where(ci <= ri, 1.0, 0.0)
        cum = _tri_sum(tri, logf) + carry_ref[0:1, :]
        o_ref[...] = cum
        carry_ref[...] = jnp.broadcast_to(cum[tm - 1:tm, :], carry_ref.shape)


def _gates(mode, x, w_small, p0, p1, seq_len, tm=512):
    n, k = x.shape
    tm = min(tm, seq_len)
    row = pl.BlockSpec((1, LANES), lambda i: (0, 0))
    return pl.pallas_call(
        functools.partial(_gates_body, mode, tm, seq_len // tm),
        grid=(n // tm,),
        in_specs=[pl.BlockSpec((tm, k), lambda i: (i, 0)),
                  pl.BlockSpec((k, LANES), lambda i: (0, 0)), row, row],
        out_specs=pl.BlockSpec((tm, LANES), lambda i: (i, 0)),
        out_shape=jax.ShapeDtypeStruct((n, LANES), F32),
        scratch_shapes=[pltpu.VMEM((SUBLANES, LANES), F32)],
        compiler_params=_cparams(("arbitrary",)),
        name="gates_" + mode,
    )(x, w_small, p0, p1)


def _pad_lanes(a):
    return jnp.pad(a, [(0, 0)] * (a.ndim - 1) + [(0, LANES - a.shape[-1])])


def _attn_body(mode, t, lam_init, *rest):
    if mode == "diff":
        q_ref, k_ref, v_ref, strip_ref, lam_ref, subln_ref, o_ref, qs_ref, m_ref, acc_ref, l_ref, tile_ref = rest
    else:
        jmin_ref, q_ref, k_ref, v_ref, ck_ref, gate_ref, o_ref, qs_ref, m_ref, acc_ref = rest
    i = pl.program_id(2)
    rc = min(ATTN_ROWS, t)

    q = q_ref[...]
    lane = lax.broadcasted_iota(I32, q.shape, 1)
    zero = jnp.zeros_like(q)
    qs_ref[0:t, :] = jnp.where(lane < 64, q, zero)
    qs_ref[t:2 * t, :] = jnp.where(lane < 64, zero, q)
    m_ref[...] = jnp.full(m_ref.shape, NEG_INF, F32)
    acc_ref[...] = jnp.zeros_like(acc_ref)
    if mode == "diff":
        l_ref[...] = jnp.zeros_like(l_ref)

        @pl.when(i == 0)
        def _():
            for d in range(strip_ref.shape[0]):
                x = jnp.broadcast_to(strip_ref[d], (t, 2 * t))
                tile_ref[d] = pltpu.roll(x, t + 1, 1, stride=1, stride_axis=0)[:, :t]
    else:
        c0 = ck_ref[:, pl.ds(pl.multiple_of(i * t, t), LANES)][:, 0:1]

    def step(kind, j, d=0):
        keys = pl.ds(pl.multiple_of(j * t, t), t)
        k = k_ref[keys, :]
        v = v_ref[keys, :]
        if mode == "fox":
            lane = lax.broadcasted_iota(I32, v.shape, 1)
            one = jnp.ones_like(v)
            rhs = (jnp.where(lane < 64, v, one), jnp.where(lane < 64, one, v))
            bias = (c0 - ck_ref[:, keys]) * LOG2E
        if kind == "diag" and t >= 2 * LANES:
            chunks = [(half, h0, t // 2, 0, h0 + t // 2) for half in range(2) for h0 in (0, t // 2)]
        else:
            kc = t if kind == "far" else min(ATTN_KEYS, t)
            chunks = [(r0 // t, r0 % t, rc, k0, kc) for r0 in range(0, 2 * t, rc) for k0 in range(0, t, kc)]
        for half, h0, nr, k0, nk in chunks:
            rows = slice(half * t + h0, half * t + h0 + nr)
            cols = slice(k0, k0 + nk)
            s = _dot_nt(qs_ref[rows, :], k[cols])
            if mode == "fox":
                s = s + bias[half:half + 1, cols]
            elif kind != "far":
                s = s + tile_ref[d, h0:h0 + nr, cols]
            if kind == "diag":
                rr = lax.broadcasted_iota(I32, (nr, nk), 0) + h0
                cc = lax.broadcasted_iota(I32, (nr, nk), 1)
                s = jnp.where(rr >= cc, s, NEG_INF)
            m_prev = m_ref[rows, :]
            m_new = jnp.maximum(m_prev, jnp.max(s, axis=1, keepdims=True))
            alpha = jnp.exp2(m_prev - m_new)
            pr = jnp.exp2(s - pltpu.repeat(m_new, nk // LANES, 1))
            if mode == "diff":
                l_ref[rows, :] = alpha * l_ref[rows, :] + jnp.sum(pr, axis=1, keepdims=True)
                pv = _dot(pr.astype(BF16), v[cols])
            else:
                pv = _dot(pr.astype(BF16), rhs[half][cols])
            acc_ref[rows, :] = alpha * acc_ref[rows, :] + pv
            m_ref[rows, :] = m_new

    def each_block(kind, lo, hi):
        lax.fori_loop(lo, hi, lambda j, carry: (step(kind, j), carry)[1], 0)

    if mode == "diff":
        n_near = strip_ref.shape[0] - 1
        each_block("far", 0, jnp.maximum(i - n_near, 0))
        for d in range(n_near, 0, -1):
            pl.when(i >= d)(functools.partial(step, "near", i - d, d))
    else:
        row = (pl.program_id(0) * pl.num_programs(1) + pl.program_id(1)) * pl.num_programs(2)
        each_block("off", jmin_ref[row + i], i)
    step("diag", i)

    acc = acc_ref[...]
    if mode == "diff":
        o = acc * (1.0 / l_ref[...])
        lp = lam_ref[...]
        lam = (jnp.exp(jnp.sum(lp[0:1] * lp[1:2], axis=-1, keepdims=True))
               - jnp.exp(jnp.sum(lp[2:3] * lp[3:4], axis=-1, keepdims=True)) + lam_init)
        dlt = o[0:t] - lam * o[t:2 * t]
        ms = jnp.mean(dlt * dlt, axis=-1, keepdims=True)
        out = dlt * lax.rsqrt(ms + RMS_EPS) * subln_ref[...] * (1.0 - lam_init)
    else:
        lane = lax.broadcasted_iota(I32, (t, LANES), 1)
        lo = acc[0:t]
        hi = acc[t:2 * t]
        out = jnp.where(lane < 64, lo * (1.0 / lo[:, 64:65]), hi * (1.0 / hi[:, 0:1]))
        out = out * _sigmoid(gate_ref[...].astype(F32))
    o_ref[...] = out.astype(o_ref.dtype)


def _attn_scratch(mode, t, n_tiles=0):
    base = [pltpu.VMEM((2 * t, LANES), BF16), pltpu.VMEM((2 * t, LANES), F32), pltpu.VMEM((2 * t, LANES), F32)]
    if mode == "diff":
        base += [pltpu.VMEM((2 * t, LANES), F32), pltpu.VMEM((n_tiles, t, t), F32)]
    return base


def _diff_attention(proj, bsz, seq, strips, lam_params, subln_w, lam_init, t):
    nq = seq // t
    nd = strips.shape[1]
    h_ = A_HEADS
    return pl.pallas_call(
        functools.partial(_attn_body, "diff", t, lam_init),
        grid=(bsz, h_, nq),
        in_specs=[
            pl.BlockSpec((t, LANES), lambda b, h, i: (b * nq + i, h)),
            pl.BlockSpec((seq, LANES), lambda b, h, i: (b, h_ + h)),
            pl.BlockSpec((seq, LANES), lambda b, h, i: (b, 2 * h_ + h)),
            pl.BlockSpec((None, nd, 1, 2 * t), lambda b, h, i: (h, 0, 0, 0)),
            pl.BlockSpec(lam_params.shape, lambda b, h, i: (0, 0)),
            pl.BlockSpec((1, LANES), lambda b, h, i: (0, 0)),
        ],
        out_specs=pl.BlockSpec((t, LANES), lambda b, h, i: (b * nq + i, h)),
        out_shape=jax.ShapeDtypeStruct((bsz * seq, h_ * LANES), BF16),
        scratch_shapes=_attn_scratch("diff", t, nd),
        compiler_params=_cparams(("parallel", "parallel", "arbitrary")),
        name="diff_attn",
    )(proj, proj, proj, strips, lam_params, subln_w)


def _fox_first_live_block(cum_t, logit_bound, t):
    bsz, hp, _, seq = cum_t.shape
    nq = seq // t
    c_start = cum_t[..., 0::t]
    c_end = cum_t[..., t - 1::t]
    gap = jnp.max(c_start[..., :, None] - c_end[..., None, :], axis=2) * LOG2E
    dead = gap < -(FOX_SKIP_MARGIN + 2.0 * logit_bound)
    dead = jnp.logical_and(dead, jnp.arange(nq)[None, :] < jnp.arange(nq)[:, None])
    return jnp.sum(jnp.cumprod(dead.astype(I32), axis=-1), axis=-1).reshape(-1)


def _fox_attention(qk, proj, cum_t, logit_bound, bsz, seq, t):
    nq = seq // t
    hp = C_HEADS // 2
    jmin = _fox_first_live_block(cum_t, logit_bound, t)
    grid_spec = pltpu.PrefetchScalarGridSpec(
        num_scalar_prefetch=1,
        grid=(bsz, hp, nq),
        in_specs=[
            pl.BlockSpec((t, LANES), lambda b, h, i, jm: (b * nq + i, h)),
            pl.BlockSpec((seq, LANES), lambda b, h, i, jm: (b, hp + h)),
            pl.BlockSpec((seq, LANES), lambda b, h, i, jm: (b, 2 * hp + h)),
            pl.BlockSpec((None, None, 2, seq), lambda b, h, i, jm: (b, h, 0, 0)),
            pl.BlockSpec((t, LANES), lambda b, h, i, jm: (b * nq + i, 3 * hp + h)),
        ],
        out_specs=pl.BlockSpec((t, LANES), lambda b, h, i, jm: (b * nq + i, h)),
        scratch_shapes=_attn_scratch("fox", t),
    )
    return pl.pallas_call(
        functools.partial(_attn_body, "fox", t, 0.0),
        grid_spec=grid_spec,
        out_shape=jax.ShapeDtypeStruct((bsz * seq, hp * LANES), BF16),
        compiler_params=_cparams(("parallel", "parallel", "arbitrary")),
        name="fox_attn",
    )(jmin, qk, qk, proj, cum_t, proj)


def _t5_bias_by_distance(t5_bias, seq):
    n = jnp.arange(seq, dtype=I32)
    max_exact = T5_BUCKETS // 2
    nf = jnp.maximum(n, 1).astype(F32)
    large = max_exact + (jnp.log(nf / max_exact) / math.log(T5_MAX_DISTANCE / max_exact)
                         * (T5_BUCKETS - max_exact)).astype(I32)
    large = jnp.minimum(large, T5_BUCKETS - 1)
    bucket = jnp.where(n < max_exact, n, large)
    return t5_bias.astype(F32).T[:, bucket]


def _t5_strips(t5_bias, seq, t):
    last = t5_bias.astype(F32).T[:, T5_BUCKETS - 1:T5_BUCKETS]
    vec = (_t5_bias_by_distance(t5_bias, seq) - last) * LOG2E
    nd = 1
    while nd * t - (t - 1) < T5_MAX_DISTANCE and nd < seq // t:
        nd += 1
    d = np.arange(nd)[:, None]
    c = np.arange(2 * t)[None, :]
    rel = d * t + t - 1 - c
    rel = np.clip(rel, 0, seq - 1)
    strips = vec[:, jnp.asarray(rel.astype(np.int32))]
    return strips[:, :, None, :]


def _gdn_body(tb, xq_ref, xk_ref, xv_ref, z_ref, gcol_ref, grow_ref, cw_ref, nw_ref, o_ref, s_ref, carry_ref):
    @pl.when(pl.program_id(2) == 0)
    def _():
        s_ref[...] = jnp.zeros_like(s_ref)
        carry_ref[...] = jnp.zeros_like(carry_ref)

    n_heads = GDN_HEADS_PER_STEP
    c_ = GDN_CHUNK
    nc = tb // c_
    gates = gcol_ref[...]
    lane = lax.broadcasted_iota(I32, gates.shape, 1)

    def conv_silu(g, idx, x_ref):
        x = x_ref[:, g * LANES:(g + 1) * LANES].astype(F32)
        xe = jnp.concatenate([carry_ref[g, idx], x], axis=0)
        taps = cw_ref[g, 4 * idx:4 * idx + 4, :]
        y = taps[3:4] * x
        for tap in range(CONV_WIDTH - 1):
            y = y + taps[tap:tap + 1] * pltpu.roll(xe, CONV_WIDTH - 1 - tap, 0)[SUBLANES:]
        carry_ref[g, idx] = x[tb - SUBLANES:]
        return y * _sigmoid(y)

    q_l, k_l, v_l, beta_l, gc_l, gcr_l = [], [], [], [], [], []
    for g in range(n_heads):
        h = pl.program_id(1) * n_heads + g
        q = conv_silu(g, 0, xq_ref)
        k = conv_silu(g, 1, xk_ref)
        v_l.append(conv_silu(g, 2, xv_ref))
        q_l.append(q * lax.rsqrt(jnp.sum(q * q, axis=-1, keepdims=True) + RMS_EPS) * (B_HEAD_DIM ** -0.5))
        k_l.append(k * lax.rsqrt(jnp.sum(k * k, axis=-1, keepdims=True) + RMS_EPS))
        beta_l.append(jnp.sum(jnp.where(lane == h, gates, 0.0), axis=1, keepdims=True))
        gc_l.append(jnp.sum(jnp.where(lane == B_HEADS + h, gates, 0.0), axis=1, keepdims=True))
        gc_row = grow_ref[g, 1:2, :]
        gcr_l.extend(gc_row[None, :, c * c_:(c + 1) * c_] for c in range(nc))

    def chunked(xs):
        return jnp.concatenate([x.reshape(nc, c_, x.shape[-1]) for x in xs], axis=0)

    def bdot(a, b):
        return lax.dot_general(a.astype(BF16), b.astype(BF16), (((2,), (1,)), ((0,), (0,))),
                               preferred_element_type=F32)

    def bdot_nt(a, b):
        return lax.dot_general(a.astype(BF16), b.astype(BF16), (((2,), (2,)), ((0,), (0,))),
                               preferred_element_type=F32)

    q3, k3, v3, beta3, gc3 = chunked(q_l), chunked(k_l), chunked(v_l), chunked(beta_l), chunked(gc_l)
    gcr3 = jnp.concatenate(gcr_l, axis=0)
    ri = lax.broadcasted_iota(I32, (1, c_, c_), 1)
    ci = lax.broadcasted_iota(I32, (1, c_, c_), 2)
    decay = jnp.exp(jnp.where(ri >= ci, gc3 - gcr3, NEG_INF))
    egc = jnp.exp(gc3)
    kb3 = k3 * beta3
    rhs3 = jnp.concatenate([v3 * beta3, kb3 * egc], axis=2)
    g_last = gc3[:, c_ - 1:c_, :]
    qd3 = q3 * egc
    kd3 = k3 * jnp.exp(g_last - gc3)
    gl3 = jnp.exp(g_last)
    kk = bdot_nt(kb3, k3)
    qk3 = bdot_nt(q3, k3) * decay
    x = -(kk * jnp.where(ri > ci, decay, 0.0))
    r = x
    pw = x
    for _ in range(int(math.log2(c_)) - 1):
        pw = bdot(pw, pw)
        r = r + pw + bdot(r, pw)
    sol = rhs3 + bdot(r, rhs3)

    states = [s_ref[g] for g in range(n_heads)]
    outs = [[] for _ in range(n_heads)]
    for c in range(nc):
        for g in range(n_heads):
            n = g * nc + c
            sb = states[g].astype(BF16)
            v_new = sol[n, :, :B_HEAD_DIM] - _dot(sol[n, :, B_HEAD_DIM:].astype(BF16), sb)
            vnb = v_new.astype(BF16)
            outs[g].append(_dot(qd3[n].astype(BF16), sb) + _dot(qk3[n].astype(BF16), vnb))
            states[g] = states[g] * gl3[n] + _dot_tn(kd3[n].astype(BF16), vnb)
    for g in range(n_heads):
        cols = slice(g * LANES, (g + 1) * LANES)
        s_ref[g] = states[g]
        o = jnp.concatenate(outs[g], axis=0)
        o = o * lax.rsqrt(jnp.mean(o * o, axis=-1, keepdims=True) + RMS_EPS) * nw_ref[...]
        z = z_ref[:, cols].astype(F32)
        o_ref[:, cols] = (o * (z * _sigmoid(z))).astype(o_ref.dtype)


def _gdn(proj, gates, gates_row, conv_w, norm_w, bsz, seq, tb):
    nt = seq // tb
    h_ = B_HEADS
    gh = GDN_HEADS_PER_STEP
    col0 = 3 * A_HEADS
    assert h_ % gh == 0 and col0 % gh == 0
    blk = lambda off: pl.BlockSpec((tb, gh * LANES), lambda b, h, s: (b * nt + s, (col0 + off * h_) // gh + h))
    return pl.pallas_call(
        functools.partial(_gdn_body, tb),
        grid=(bsz, h_ // gh, nt),
        in_specs=[blk(0), blk(1), blk(2), blk(3),
                  pl.BlockSpec((tb, LANES), lambda b, h, s: (b * nt + s, 0)),
                  pl.BlockSpec((gh, 2, tb), lambda b, h, s: (b * (h_ // gh) + h, 0, s)),
                  pl.BlockSpec((gh, 3 * CONV_WIDTH, LANES), lambda b, h, s: (h, 0, 0)),
                  pl.BlockSpec((1, LANES), lambda b, h, s: (0, 0))],
        out_specs=pl.BlockSpec((tb, gh * LANES), lambda b, h, s: (b * nt + s, h)),
        out_shape=jax.ShapeDtypeStruct((bsz * seq, h_ * LANES), BF16),
        scratch_shapes=[pltpu.VMEM((gh, B_HEAD_DIM, B_HEAD_DIM), F32),
                        pltpu.VMEM((gh, 3, SUBLANES, LANES), F32)],
        compiler_params=_cparams(("parallel", "parallel", "arbitrary")),
        name="gdn",
    )(proj, proj, proj, proj, gates, gates_row, conv_w, norm_w)


ROW_TILES = D_MODEL // LANES


def _store_tile_rows(ref, y):
    t = y.shape[0]
    for c in range(ROW_TILES):
        ref[pl.ds(c, t, stride=ROW_TILES), :] = y[:, c * LANES:(c + 1) * LANES]


def _load_tile_rows(ref, t, lead=()):
    return jnp.concatenate([ref[lead + (pl.ds(c, t, stride=ROW_TILES), slice(None))] for c in range(ROW_TILES)],
                           axis=1)


def _outproj_ln_body(a_ref, b_ref, w_ref, x_ref, g_ref, bb_ref, o_ref, o3_ref, wb_ref):
    @pl.when(pl.program_id(0) == 0)
    def _():
        wb_ref[...] = w_ref[...].astype(BF16)

    half = a_ref.shape[1]
    hmix = _dot(a_ref[...], wb_ref[0:half, :]) + _dot(b_ref[...], wb_ref[half:, :])
    out = _layer_norm(DEEPNORM_ALPHA * x_ref[...] + hmix, g_ref[...], bb_ref[...])
    o_ref[...] = out
    _store_tile_rows(o3_ref, out)


def _outproj_ln(a, a_blk, b, b_blk, w, x, g, bb, tm=1024):
    n, d = x.shape
    tm = min(tm, n)
    half = d // 2
    row = pl.BlockSpec((1, d), lambda i: (0, 0))
    return pl.pallas_call(
        _outproj_ln_body,
        grid=(n // tm,),
        in_specs=[pl.BlockSpec((tm, half), lambda i: (i, a_blk)),
                  pl.BlockSpec((tm, half), lambda i: (i, b_blk)),
                  pl.BlockSpec((d, d), lambda i: (0, 0)),
                  pl.BlockSpec((tm, d), lambda i: (i, 0)), row, row],
        out_specs=[pl.BlockSpec((tm, d), lambda i: (i, 0)),
                   pl.BlockSpec((tm * ROW_TILES, LANES), lambda i: (i, 0))],
        out_shape=[jax.ShapeDtypeStruct((n, d), F32), jax.ShapeDtypeStruct((n * ROW_TILES, LANES), F32)],
        scratch_shapes=[pltpu.VMEM((d, d), BF16)],
        compiler_params=_cparams(("arbitrary",)),
        name="outproj_ln",
    )(a, b, w, x, g, bb)


def _router_body(tm, x_ref, w_ref, b_ref, idx_ref, gate_ref, rank_ref, cnt_ref, carry_ref):
    @pl.when(pl.program_id(0) == 0)
    def _():
        carry_ref[...] = jnp.zeros_like(carry_ref)

    logits = _dot_split(x_ref[...], w_ref[...]) + b_ref[...]
    lg = jnp.transpose(logits)[0:N_EXPERTS, :]
    e_iota = lax.broadcasted_iota(I32, (N_EXPERTS, tm), 0).astype(F32)
    vals, idxs, hots = [], [], []
    for _ in range(TOP_K):
        m = jnp.max(lg, axis=0, keepdims=True)
        idx = jnp.min(jnp.where(lg == m, e_iota, float(N_EXPERTS)), axis=0, keepdims=True)
        hot = e_iota == idx
        lg = jnp.where(hot, NEG_INF, lg)
        vals.append(m)
        idxs.append(idx)
        hots.append(hot)
    es = [jnp.exp(v - vals[0]) for v in vals]
    den = es[0] + es[1] + es[2] + es[3]
    sel = jnp.zeros((N_EXPERTS, tm), F32)
    for hot in hots:
        sel = sel + jnp.where(hot, 1.0, 0.0)
    before = jnp.where(lax.broadcasted_iota(I32, (tm, tm), 0) < lax.broadcasted_iota(I32, (tm, tm), 1),
                       1.0, 0.0).astype(BF16)
    cum = _dot(sel.astype(BF16), before) + carry_ref[:, 0:1]
    ranks = [jnp.sum(jnp.where(hot, cum, 0.0), axis=0, keepdims=True) for hot in hots]
    total = carry_ref[...] + jnp.sum(sel, axis=1, keepdims=True)
    carry_ref[...] = total
    idx_ref[...] = jnp.concatenate(idxs, axis=0).astype(I32)
    gate_ref[...] = jnp.concatenate([e / den for e in es], axis=0)
    rank_ref[...] = jnp.concatenate(ranks, axis=0).astype(I32)
    cnt_ref[...] = total


def _router(x, w_pad, b_pad, tm=512):
    n, d = x.shape
    tm = min(tm, n)
    out4 = lambda dt: jax.ShapeDtypeStruct((TOP_K, n), dt)
    blk4 = pl.BlockSpec((TOP_K, tm), lambda i: (0, i))
    return pl.pallas_call(
        functools.partial(_router_body, tm),
        grid=(n // tm,),
        in_specs=[pl.BlockSpec((tm, d), lambda i: (i, 0)),
                  pl.BlockSpec((d, LANES), lambda i: (0, 0)),
                  pl.BlockSpec((1, LANES), lambda i: (0, 0))],
        out_specs=[blk4, blk4, blk4, pl.BlockSpec((N_EXPERTS, LANES), lambda i: (0, 0))],
        out_shape=[out4(I32), out4(F32), out4(I32), jax.ShapeDtypeStruct((N_EXPERTS, LANES), F32)],
        scratch_shapes=[pltpu.VMEM((N_EXPERTS, LANES), F32)],
        compiler_params=_cparams(("arbitrary",)),
        name="moe_router",
    )(x, w_pad, b_pad)


def _row_slab(row):
    return pl.ds(pl.multiple_of(row * ROW_TILES, ROW_TILES), ROW_TILES)


def _start_row_dma(copy, k):
    copy.start(priority=k % 2)


def _start_row_dmas(n_rows, make_copy):
    def issue(r, carry):
        for k in range(TOP_K):
            _start_row_dma(make_copy(k, r), k)
        return carry

    lax.fori_loop(0, n_rows, issue, 0, unroll=4)


def _wait_row_dmas(n_rows, make_copy):
    def drain(r, carry):
        for k in range(TOP_K):
            make_copy(k, r).wait()
        return carry

    lax.fori_loop(0, n_rows, drain, 0, unroll=8)


def _row_dma_loops(n_rows, make_copy):
    _start_row_dmas(n_rows, make_copy)
    _wait_row_dmas(n_rows, make_copy)


def _dispatch_body(td, dest_ref, fill_ref, x3_ref, xg_out, zero_ref, sem, zsem):
    @pl.when(pl.program_id(0) == 0)
    def _():
        zero_ref[...] = jnp.zeros_like(zero_ref)

        def zero_copy(r):
            return pltpu.make_async_copy(zero_ref, xg_out.at[_row_slab(r)], zsem)

        def each_expert(fn):
            def expert(e, carry):
                lax.fori_loop(fill_ref[0, e], fill_ref[1, e], lambda r, c: (fn(r), c)[1], 0)
                return carry
            lax.fori_loop(0, N_EXPERTS, expert, 0)

        each_expert(lambda r: zero_copy(r).start())
        each_expert(lambda r: zero_copy(r).wait())

    _row_dma_loops(td, lambda k, r: pltpu.make_async_copy(
        x3_ref.at[_row_slab(r)], xg_out.at[_row_slab(dest_ref[k, r])], sem))


def _smem_rows(tile):
    return pl.BlockSpec((TOP_K, tile), lambda i: (0, i), memory_space=pltpu.SMEM)


def _dispatch(x3, dest, fill, n_rows, td=1024):
    n = x3.shape[0] // ROW_TILES
    td = min(td, n)
    return pl.pallas_call(
        functools.partial(_dispatch_body, td),
        grid=(n // td,),
        in_specs=[_smem_rows(td),
                  pl.BlockSpec(memory_space=pltpu.SMEM),
                  pl.BlockSpec((td * ROW_TILES, LANES), lambda i: (i, 0))],
        out_specs=pl.BlockSpec(memory_space=pl.ANY),
        out_shape=jax.ShapeDtypeStruct((n_rows * ROW_TILES, LANES), x3.dtype),
        scratch_shapes=[pltpu.VMEM((ROW_TILES, LANES), x3.dtype), pltpu.SemaphoreType.DMA(()),
                        pltpu.SemaphoreType.DMA(())],
        compiler_params=_cparams(("arbitrary",)),
        name="moe_dispatch",
    )(dest, fill, x3)


def _expert_body(be_ref, nu_ref, x_ref, wgu_ref, bgu_ref, wd_ref, bd_ref, o_ref, wgu_b, wd_b):
    i = pl.program_id(0)
    prev = be_ref[jnp.maximum(i - 1, 0)]
    fresh = jnp.logical_or(i == 0, be_ref[i] != prev)

    @pl.when(jnp.logical_and(i < nu_ref[0], fresh))
    def _():
        wgu_b[...] = wgu_ref[...].astype(BF16)
        wd_b[...] = wd_ref[...].astype(BF16)

    @pl.when(i < nu_ref[0])
    def _():
        x = _load_tile_rows(x_ref, MOE_ROWS).astype(BF16)
        hcat = _dot(x, wgu_b[...]) + bgu_ref[...]
        g = jnp.minimum(hcat[:, :D_FF], SWIGLU_LIMIT)
        u = jnp.clip(hcat[:, D_FF:], -SWIGLU_LIMIT, SWIGLU_LIMIT)
        act = g * _sigmoid(SWIGLU_ALPHA * g) * (u + 1.0)
        _store_tile_rows(o_ref, _dot(act.astype(BF16), wd_b[...]) + bd_ref[...])

    @pl.when(i >= nu_ref[0])
    def _():
        o_ref[...] = jnp.zeros_like(o_ref)


def _experts(xg, layer, block_expert, n_used, w_gu, b_gu, w_down, b_down):
    d = D_MODEL
    nb = xg.shape[0] // (MOE_ROWS * ROW_TILES)
    blk = pl.BlockSpec((MOE_ROWS * ROW_TILES, LANES), lambda i, be, nu: (i, 0))
    grid_spec = pltpu.PrefetchScalarGridSpec(
        num_scalar_prefetch=2,
        grid=(nb,),
        in_specs=[
            pl.BlockSpec((MOE_ROWS * ROW_TILES, LANES), lambda i, be, nu: (jnp.minimum(i, nu[0] - 1), 0)),
            pl.BlockSpec((None, None, d, 2 * D_FF), lambda i, be, nu: (layer, be[i], 0, 0)),
            pl.BlockSpec((None, None, 1, 2 * D_FF), lambda i, be, nu: (layer, be[i], 0, 0)),
            pl.BlockSpec((None, None, D_FF, d), lambda i, be, nu: (layer, be[i], 0, 0)),
            pl.BlockSpec((None, None, 1, d), lambda i, be, nu: (layer, be[i], 0, 0)),
        ],
        out_specs=blk,
        scratch_shapes=[pltpu.VMEM((d, 2 * D_FF), BF16), pltpu.VMEM((D_FF, d), BF16)],
    )
    return pl.pallas_call(
        _expert_body,
        grid_spec=grid_spec,
        out_shape=jax.ShapeDtypeStruct(xg.shape, F32),
        compiler_params=_cparams(("arbitrary",)),
        name="moe_experts",
    )(block_expert, n_used, xg, w_gu, b_gu[:, :, None, :], w_down, b_down[:, :, None, :])


def _combine_ln_body(tc, dest_ref, dest_next_ref, yg_hbm, gate_ref, x_ref, g_ref, b_ref, o_ref, ybuf, sems):
    i = pl.program_id(0)

    def gather(rows_ref, s):
        return lambda k, r: pltpu.make_async_copy(
            yg_hbm.at[_row_slab(rows_ref[k, r])], ybuf.at[s, k, _row_slab(r)], sems.at[s])

    def run(s):
        @pl.when(i == 0)
        def _():
            _start_row_dmas(tc, gather(dest_ref, s))

        @pl.when(i + 1 < pl.num_programs(0))
        def _():
            _start_row_dmas(tc, gather(dest_next_ref, 1 - s))

        _wait_row_dmas(tc, gather(dest_ref, s))
        gate = gate_ref[...]
        hmoe = DEEPNORM_ALPHA * x_ref[...]
        for k in range(TOP_K):
            hmoe = hmoe + gate[:, k:k + 1] * _load_tile_rows(ybuf, tc, (s, k))
        o_ref[...] = _layer_norm(hmoe, g_ref[...], b_ref[...])

    pl.when(i % 2 == 0)(functools.partial(run, 0))
    pl.when(i % 2 == 1)(functools.partial(run, 1))


def _combine_ln(yg, dest, gate_t, x, g, b, tc=512):
    n, d = x.shape
    tc = min(tc, n)
    steps = n // tc
    row = pl.BlockSpec((1, d), lambda i: (0, 0))
    return pl.pallas_call(
        functools.partial(_combine_ln_body, tc),
        grid=(steps,),
        in_specs=[_smem_rows(tc),
                  pl.BlockSpec((TOP_K, tc), lambda i: (0, jnp.minimum(i + 1, steps - 1)), memory_space=pltpu.SMEM),
                  pl.BlockSpec(memory_space=pl.ANY),
                  pl.BlockSpec((tc, TOP_K), lambda i: (i, 0)),
                  pl.BlockSpec((tc, d), lambda i: (i, 0)), row, row],
        out_specs=pl.BlockSpec((tc, d), lambda i: (i, 0)),
        out_shape=jax.ShapeDtypeStruct((n, d), F32),
        scratch_shapes=[pltpu.VMEM((2, TOP_K, tc * ROW_TILES, LANES), F32), pltpu.SemaphoreType.DMA((2,))],
        compiler_params=_cparams(("arbitrary",)),
        name="moe_combine_ln",
    )(dest, dest, yg, gate_t, x, g, b)


def _moe_ln(x, x3, layer, router_w, router_b, w_gu, b_gu, w_down, b_down, ln_g, ln_b):
    n, d = x.shape
    idx, gate, rank, cnt = _router(x, _pad_lanes(router_w), _pad_lanes(router_b[None, :]))
    counts = cnt[:, 0].astype(I32)
    padded = (counts + MOE_ROWS - 1) // MOE_ROWS * MOE_ROWS
    pad_end = jnp.cumsum(padded)
    pad_start = pad_end - padded
    hot = idx[:, :, None] == jnp.arange(N_EXPERTS, dtype=I32)[None, None, :]
    dest = jnp.sum(jnp.where(hot, pad_start[None, None, :], 0), axis=-1) + rank
    n_blocks = -(-(n * TOP_K) // MOE_ROWS) + N_EXPERTS
    block_row0 = jnp.arange(n_blocks, dtype=I32) * MOE_ROWS
    block_expert = jnp.minimum(jnp.sum((pad_end[None, :] <= block_row0[:, None]).astype(I32), axis=1),
                               N_EXPERTS - 1)
    n_used = (pad_end[-1:] // MOE_ROWS).astype(I32)
    fill = jnp.stack([pad_start + counts, pad_end])
    xg = _dispatch(x3, dest, fill, n_blocks * MOE_ROWS)
    yg = _experts(xg, layer, block_expert, n_used, w_gu, b_gu, w_down, b_down)
    return _combine_ln(yg, dest, gate.T, x, ln_g[None, :], ln_b[None, :])


def _even_mixer_ln(x, bsz, seq, w_in, w_out, lam_params, subln_w, conv_w, a_log, dt_bias, gdn_norm_w,
                   t5_bias, lam_init, ln_g, ln_b):
    main = 3 * A_HEADS * LANES + 4 * B_HEADS * B_HEAD_DIM
    n_q = A_HEADS * 2 * A_HEAD_DIM
    col_scale = jnp.concatenate([jnp.full((1, n_q), A_HEAD_DIM ** -0.5 * LOG2E, F32),
                                 jnp.ones((1, main - n_q), F32)], axis=1)
    proj = _proj(x, w_in[:, :main], col_scale)
    zeros4 = jnp.zeros((B_HEADS,), F32)
    p0 = _pad_lanes(jnp.concatenate([zeros4, a_log.astype(F32)])[None, :])
    p1 = _pad_lanes(jnp.concatenate([zeros4, dt_bias.astype(F32)])[None, :])
    gates = _gates("even", x, _pad_lanes(w_in[:, main:]), p0, p1, seq)
    t = min(ATTN_TILE, seq)
    strips = _t5_strips(t5_bias, seq, t)
    ao = _diff_attention(proj, bsz, seq, strips, lam_params.astype(F32), subln_w[None, :].astype(F32),
                         lam_init, t)
    g8 = gates[:, :2 * B_HEADS].reshape(bsz, seq, 2, B_HEADS)
    gates_row = g8.transpose(0, 3, 2, 1).reshape(bsz * B_HEADS, 2, seq)
    cw = conv_w.astype(F32).reshape(CONV_WIDTH, 3, B_HEADS, B_HEAD_DIM).transpose(2, 1, 0, 3)
    cw = cw.reshape(B_HEADS, 3 * CONV_WIDTH, B_HEAD_DIM)
    bo = _gdn(proj, gates, gates_row, cw, gdn_norm_w[None, :].astype(F32), bsz, seq, min(GDN_TILE, seq))
    return _outproj_ln(ao, 0, bo, 0, w_out, x, ln_g[None, :], ln_b[None, :])


def _odd_mixer_ln(x, bsz, seq, w_in, w_out, qk_norm_w, forget_b, ln_g, ln_b):
    main = 4 * C_HEADS * C_HEAD_DIM
    width = C_HEADS * C_HEAD_DIM
    scale = jnp.asarray([C_HEAD_DIM ** -0.5 * LOG2E, 1.0], F32)[:, None]
    wqk = qk_norm_w.astype(F32) * scale
    col_scale = jnp.concatenate([jnp.tile(wqk[0], C_HEADS), jnp.tile(wqk[1], C_HEADS),
                                 jnp.ones((main - 2 * width,), F32)])[None, :]
    proj = _proj(x, w_in[:, :main], col_scale, norm_chunks=range(2 * width // PROJ_CHUNK))
    fb =_pad_lanes(forget_b.astype(F32)[None, :])
    cum = _gates("odd", x, _pad_lanes(w_in[:, main:]), fb, fb, seq)
    cum_t = cum[:, :C_HEADS].reshape(bsz, seq, C_HEADS // 2, 2).transpose(0, 2, 3, 1)
    logit_bound = 1.02 * C_HEAD_DIM * jnp.max(jnp.abs(wqk[0])) * jnp.max(jnp.abs(wqk[1]))
    o = _fox_attention(proj, proj, cum_t, logit_bound, bsz, seq, min(ATTN_TILE, seq))
    return _outproj_ln(o, 0, o, 1, w_out, x, ln_g[None, :], ln_b[None, :])


def kernel(x, t5_bias, even_w_in, even_w_out, diff_lambda, diff_subln_w, gdn_conv_w, gdn_a_log, gdn_dt_bias, gdn_norm_w, odd_w_in, odd_w_out, fox_qk_norm_w, fox_forget_b, router_w, router_b, moe_w_gate_up, moe_b_gate_up, moe_w_down, moe_b_down, ln_mix_g, ln_mix_b, ln_ffn_g, ln_ffn_b):
    bsz, seq, d = x.shape
    xf = x.reshape(bsz * seq, d)
    for layer in range(DEPTH):
        i = layer // 2
        if layer % 2 == 0:
            lam_init = 0.8 - 0.6 * math.exp(-0.3 * layer)
            xf, x3 = _even_mixer_ln(xf, bsz, seq, even_w_in[i], even_w_out[i], diff_lambda[i], diff_subln_w[i],
                                    gdn_conv_w[i], gdn_a_log[i], gdn_dt_bias[i], gdn_norm_w[i], t5_bias, lam_init,
                                    ln_mix_g[layer], ln_mix_b[layer])
        else:
            xf, x3 = _odd_mixer_ln(xf, bsz, seq, odd_w_in[i], odd_w_out[i], fox_qk_norm_w[i], fox_forget_b[i],
                                   ln_mix_g[layer], ln_mix_b[layer])
        xf = _moe_ln(xf, x3, layer, router_w[layer], router_b[layer], moe_w_gate_up, moe_b_gate_up,
                     moe_w_down, moe_b_down, ln_ffn_g[layer], ln_ffn_b[layer])
    return xf.reshape(bsz, seq, d)
```

```python
import functools
import math

import numpy as np
import jax
import jax.numpy as jnp
from jax import lax
from jax.experimental import pallas as pl
from jax.experimental.pallas import tpu as pltpu

F32 = jnp.float32
BF16 = jnp.bfloat16
I32 = jnp.int32

D_MODEL = 1024
DEPTH = 4
A_HEADS = 4
A_HEAD_DIM = 64
B_HEADS = 4
B_HEAD_DIM = 128
CONV_WIDTH = 4
C_HEADS = 16
C_HEAD_DIM = 64
T5_BUCKETS = 32
T5_MAX_DISTANCE = 2048
N_EXPERTS = 32
TOP_K = 4
D_FF = D_MODEL
SWIGLU_LIMIT = 7.0
SWIGLU_ALPHA = 1.702
DEEPNORM_ALPHA = (2 * DEPTH) ** 0.25
LN_EPS = 1e-5
RMS_EPS = 1e-6

LANES = 128
SUBLANES = 8
VMEM_LIMIT = 56 * 1024 * 1024

ATTN_TILE = 1024
ATTN_ROWS = 1024
ATTN_KEYS = 512
GDN_CHUNK = 128
GDN_TILE = 512
GDN_HEADS_PER_STEP = 4
MOE_ROWS = 512
FOX_SKIP_MARGIN = 170.0
NEG_INF = float("-inf")
LOG2E = math.log2(math.e)


def _cparams(sem, vmem=VMEM_LIMIT):
    return pltpu.CompilerParams(dimension_semantics=sem, vmem_limit_bytes=vmem)


def _dot(a, b, **kw):
    return jnp.dot(a, b, preferred_element_type=F32, **kw)


def _dot_nt(a, b, **kw):
    return lax.dot_general(a, b, (((1,), (1,)), ((), ())), preferred_element_type=F32, **kw)


def _dot_tn(a, b, **kw):
    return lax.dot_general(a, b, (((0,), (0,)), ((), ())), preferred_element_type=F32, **kw)


def _sigmoid(x):
    return 1.0 / (1.0 + jnp.exp(-x))


def _softplus(x):
    return jnp.maximum(x, 0.0) + jnp.log(1.0 + jnp.exp(-jnp.abs(x)))


def _layer_norm(xf, g, b):
    mu = jnp.mean(xf, axis=-1, keepdims=True)
    xc = xf - mu
    var = jnp.mean(xc * xc, axis=-1, keepdims=True)
    return xc * lax.rsqrt(var + LN_EPS) * g + b


PROJ_CHUNK = 512


def _proj_body(norm_chunks, x_ref, w_ref, cs_ref, gsum_ref, o_ref):
    xb = x_ref[...].astype(BF16)
    for c in range(w_ref.shape[1] // PROJ_CHUNK):
        cols = slice(c * PROJ_CHUNK, (c + 1) * PROJ_CHUNK)
        acc = _dot(xb, w_ref[:, cols])
        if c in norm_chunks:
            ss = _dot((acc * acc).astype(BF16), gsum_ref[...])
            acc = acc * lax.rsqrt(ss * (1.0 / C_HEAD_DIM) + RMS_EPS)
        o_ref[:, cols] = (acc * cs_ref[:, cols]).astype(o_ref.dtype)


def _proj(x, w, col_scale, norm_chunks=(), tm=1024):
    n, k = x.shape
    m = w.shape[1]
    tm = min(tm, n)
    head = jnp.arange(PROJ_CHUNK, dtype=I32) // C_HEAD_DIM
    gsum = (head[:, None] == head[None, :]).astype(BF16)
    return pl.pallas_call(
        functools.partial(_proj_body, tuple(norm_chunks)),
        grid=(n // tm,),
        in_specs=[pl.BlockSpec((tm, k), lambda i: (i, 0)),
                  pl.BlockSpec((k, m), lambda i: (0, 0)),
                  pl.BlockSpec((1, m), lambda i: (0, 0)),
                  pl.BlockSpec((PROJ_CHUNK, PROJ_CHUNK), lambda i: (0, 0))],
        out_specs=pl.BlockSpec((tm, m), lambda i: (i, 0)),
        out_shape=jax.ShapeDtypeStruct((n, m), BF16),
        compiler_params=_cparams(("parallel",)),
        name="proj",
    )(x, w.astype(BF16), col_scale, gsum)


def _split_bf16(a, parts):
    out = []
    for _ in range(parts):
        hi = a.astype(BF16)
        out.append(hi)
        a = a - hi.astype(F32)
    return out


def _tri_sum(tri, a):
    tri = tri.astype(BF16)
    return sum(_dot(tri, part) for part in _split_bf16(a, 3))


def _dot_split(x, w):
    xh, xl = _split_bf16(x, 2)
    wh, wl = _split_bf16(w, 2)
    return _dot(xh, wh) + (_dot(xl, wh) + _dot(xh, wl))


def _gates_body(mode, tm, steps_per_seq, x_ref, w_ref, p0_ref, p1_ref, o_ref, carry_ref):
    z = _dot_split(x_ref[...], w_ref[...])
    ri = lax.broadcasted_iota(I32, (tm, tm), 0)
    ci = lax.broadcasted_iota(I32, (tm, tm), 1)
    if mode == "even":
        lane = lax.broadcasted_iota(I32, (tm, LANES), 1)
        beta = _sigmoid(z)
        g = -jnp.exp(p0_ref[...]) * _softplus(z + p1_ref[...])
        shift = int(math.log2(GDN_CHUNK))
        same_chunk = jnp.right_shift(ri, shift) == jnp.right_shift(ci, shift)
        tri = jnp.where(jnp.logical_and(same_chunk, ci <= ri), 1.0, 0.0)
        gc = _tri_sum(tri, g)
        o_ref[...] = jnp.where(lane < B_HEADS, beta, gc)
    else:
        @pl.when(pl.program_id(0) % steps_per_seq == 0)
        def _():
            carry_ref[...] = jnp.zeros_like(carry_ref)

        logf = -_softplus(-(z + p0_ref[...]))
        tri = jnp.where(ci <= ri, 1.0, 0.0)
        cum = _tri_sum(tri, logf) + carry_ref[0:1, :]
        o_ref[...] = cum
        carry_ref[...] = jnp.broadcast_to(cum[tm - 1:tm, :], carry_ref.shape)


def _gates(mode, x, w_small, p0, p1, seq_len, tm=512):
    n, k = x.shape
    tm = min(tm, seq_len)
    row = pl.BlockSpec((1, LANES), lambda i: (0, 0))
    return pl.pallas_call(
        functools.partial(_gates_body, mode, tm, seq_len // tm),
        grid=(n // tm,),
        in_specs=[pl.BlockSpec((tm, k), lambda i: (i, 0)),
                  pl.BlockSpec((k, LANES), lambda i: (0, 0)), row, row],
        out_specs=pl.BlockSpec((tm, LANES), lambda i: (i, 0)),
        out_shape=jax.ShapeDtypeStruct((n, LANES), F32),
        scratch_shapes=[pltpu.VMEM((SUBLANES, LANES), F32)],
        compiler_params=_cparams(("arbitrary",)),
        name="gates_" + mode,
    )(x, w_small, p0, p1)


def _pad_lanes(a):
    return jnp.pad(a, [(0, 0)] * (a.ndim - 1) + [(0, LANES - a.shape[-1])])


def _attn_body(mode, t, lam_init, *rest):
    if mode == "diff":
        q_ref, k_ref, v_ref, strip_ref, lam_ref, subln_ref, o_ref, qs_ref, m_ref, acc_ref, l_ref, tile_ref = rest
    else:
        jmin_ref, q_ref, k_ref, v_ref, ck_ref, gate_ref, o_ref, qs_ref, m_ref, acc_ref = rest
    i = pl.program_id(2)
    rc = min(ATTN_ROWS, t)

    q = q_ref[...]
    lane = lax.broadcasted_iota(I32, q.shape, 1)
    zero = jnp.zeros_like(q)
    qs_ref[0:t, :] = jnp.where(lane < 64, q, zero)
    qs_ref[t:2 * t, :] = jnp.where(lane < 64, zero, q)
    m_ref[...] = jnp.full(m_ref.shape, NEG_INF, F32)
    acc_ref[...] = jnp.zeros_like(acc_ref)
    if mode == "diff":
        l_ref[...] = jnp.zeros_like(l_ref)

        @pl.when(i == 0)
        def _():
            for d in range(strip_ref.shape[0]):
                x = jnp.broadcast_to(strip_ref[d], (t, 2 * t))
                tile_ref[d] = pltpu.roll(x, t + 1, 1, stride=1, stride_axis=0)[:, :t]
    else:
        c0 = ck_ref[:, pl.ds(pl.multiple_of(i * t, t), LANES)][:, 0:1]

    def step(kind, j, d=0):
        keys = pl.ds(pl.multiple_of(j * t, t), t)
        k = k_ref[keys, :]
        v = v_ref[keys, :]
        if mode == "fox":
            lane = lax.broadcasted_iota(I32, v.shape, 1)
            one = jnp.ones_like(v)
            rhs = (jnp.where(lane < 64, v, one), jnp.where(lane < 64, one, v))
            bias = (c0 - ck_ref[:, keys]) * LOG2E
        if kind == "diag" and t >= 2 * LANES:
            chunks = [(half, h0, t // 2, 0, h0 + t // 2) for half in range(2) for h0 in (0, t // 2)]
        else:
            kc = t if kind == "far" else min(ATTN_KEYS, t)
            chunks = [(r0 // t, r0 % t, rc, k0, kc) for r0 in range(0, 2 * t, rc) for k0 in range(0, t, kc)]
        for half, h0, nr, k0, nk in chunks:
            rows = slice(half * t + h0, half * t + h0 + nr)
            cols = slice(k0, k0 + nk)
            s = _dot_nt(qs_ref[rows, :], k[cols])
            if mode == "fox":
                s = s + bias[half:half + 1, cols]
            elif kind != "far":
                s = s + tile_ref[d, h0:h0 + nr, cols]
            if kind == "diag":
                rr = lax.broadcasted_iota(I32, (nr, nk), 0) + h0
                cc = lax.broadcasted_iota(I32, (nr, nk), 1)
                s = jnp.where(rr >= cc, s, NEG_INF)
            m_prev = m_ref[rows, :]
            m_new = jnp.maximum(m_prev, jnp.max(s, axis=1, keepdims=True))
            alpha = jnp.exp2(m_prev - m_new)
            pr = jnp.exp2(s - pltpu.repeat(m_new, nk // LANES, 1))
            if mode == "diff":
                l_ref[rows, :] = alpha * l_ref[rows, :] + jnp.sum(pr, axis=1, keepdims=True)
                pv = _dot(pr.astype(BF16), v[cols])
            else:
                pv = _dot(pr.astype(BF16), rhs[half][cols])
            acc_ref[rows, :] = alpha * acc_ref[rows, :] + pv
            m_ref[rows, :] = m_new

    def each_block(kind, lo, hi):
        lax.fori_loop(lo, hi, lambda j, carry: (step(kind, j), carry)[1], 0)

    if mode == "diff":
        n_near = strip_ref.shape[0] - 1
        each_block("far", 0, jnp.maximum(i - n_near, 0))
        for d in range(n_near, 0, -1):
            pl.when(i >= d)(functools.partial(step, "near", i - d, d))
    else:
        row = (pl.program_id(0) * pl.num_programs(1) + pl.program_id(1)) * pl.num_programs(2)
        each_block("off", jmin_ref[row + i], i)
    step("diag", i)

    acc = acc_ref[...]
    if mode == "diff":
        o = acc * (1.0 / l_ref[...])
        lp = lam_ref[...]
        lam = (jnp.exp(jnp.sum(lp[0:1] * lp[1:2], axis=-1, keepdims=True))
               - jnp.exp(jnp.sum(lp[2:3] * lp[3:4], axis=-1, keepdims=True)) + lam_init)
        dlt = o[0:t] - lam * o[t:2 * t]
        ms = jnp.mean(dlt * dlt, axis=-1, keepdims=True)
        out = dlt * lax.rsqrt(ms + RMS_EPS) * subln_ref[...] * (1.0 - lam_init)
    else:
        lane = lax.broadcasted_iota(I32, (t, LANES), 1)
        lo = acc[0:t]
        hi = acc[t:2 * t]
        out = jnp.where(lane < 64, lo * (1.0 / lo[:, 64:65]), hi * (1.0 / hi[:, 0:1]))
        out = out * _sigmoid(gate_ref[...].astype(F32))
    o_ref[...] = out.astype(o_ref.dtype)


def _attn_scratch(mode, t, n_tiles=0):
    base = [pltpu.VMEM((2 * t, LANES), BF16), pltpu.VMEM((2 * t, LANES), F32), pltpu.VMEM((2 * t, LANES), F32)]
    if mode == "diff":
        base += [pltpu.VMEM((2 * t, LANES), F32), pltpu.VMEM((n_tiles, t, t), F32)]
    return base


def _diff_attention(proj, bsz, seq, strips, lam_params, subln_w, lam_init, t):
    nq = seq // t
    nd = strips.shape[1]
    h_ = A_HEADS
    return pl.pallas_call(
        functools.partial(_attn_body, "diff", t, lam_init),
        grid=(bsz, h_, nq),
        in_specs=[
            pl.BlockSpec((t, LANES), lambda b, h, i: (b * nq + i, h)),
            pl.BlockSpec((seq, LANES), lambda b, h, i: (b, h_ + h)),
            pl.BlockSpec((seq, LANES), lambda b, h, i: (b, 2 * h_ + h)),
            pl.BlockSpec((None, nd, 1, 2 * t), lambda b, h, i: (h, 0, 0, 0)),
            pl.BlockSpec(lam_params.shape, lambda b, h, i: (0, 0)),
            pl.BlockSpec((1, LANES), lambda b, h, i: (0, 0)),
        ],
        out_specs=pl.BlockSpec((t, LANES), lambda b, h, i: (b * nq + i, h)),
        out_shape=jax.ShapeDtypeStruct((bsz * seq, h_ * LANES), BF16),
        scratch_shapes=_attn_scratch("diff", t, nd),
        compiler_params=_cparams(("parallel", "parallel", "arbitrary")),
        name="diff_attn",
    )(proj, proj, proj, strips, lam_params, subln_w)


def _fox_first_live_block(cum_t, logit_bound, t):
    bsz, hp, _, seq = cum_t.shape
    nq = seq // t
    c_start = cum_t[..., 0::t]
    c_end = cum_t[..., t - 1::t]
    gap = jnp.max(c_start[..., :, None] - c_end[..., None, :], axis=2) * LOG2E
    dead = gap < -(FOX_SKIP_MARGIN + 2.0 * logit_bound)
    dead = jnp.logical_and(dead, jnp.arange(nq)[None, :] < jnp.arange(nq)[:, None])
    return jnp.sum(jnp.cumprod(dead.astype(I32), axis=-1), axis=-1).reshape(-1)


def _fox_attention(qk, proj, cum_t, logit_bound, bsz, seq, t):
    nq = seq // t
    hp = C_HEADS // 2
    jmin = _fox_first_live_block(cum_t, logit_bound, t)
    grid_spec = pltpu.PrefetchScalarGridSpec(
        num_scalar_prefetch=1,
        grid=(bsz, hp, nq),
        in_specs=[
            pl.BlockSpec((t, LANES), lambda b, h, i, jm: (b * nq + i, h)),
            pl.BlockSpec((seq, LANES), lambda b, h, i, jm: (b, hp + h)),
            pl.BlockSpec((seq, LANES), lambda b, h, i, jm: (b, 2 * hp + h)),
            pl.BlockSpec((None, None, 2, seq), lambda b, h, i, jm: (b, h, 0, 0)),
            pl.BlockSpec((t, LANES), lambda b, h, i, jm: (b * nq + i, 3 * hp + h)),
        ],
        out_specs=pl.BlockSpec((t, LANES), lambda b, h, i, jm: (b * nq + i, h)),
        scratch_shapes=_attn_scratch("fox", t),
    )
    return pl.pallas_call(
        functools.partial(_attn_body, "fox", t, 0.0),
        grid_spec=grid_spec,
        out_shape=jax.ShapeDtypeStruct((bsz * seq, hp * LANES), BF16),
        compiler_params=_cparams(("parallel", "parallel", "arbitrary")),
        name="fox_attn",
    )(jmin, qk, qk, proj, cum_t, proj)


def _t5_bias_by_distance(t5_bias, seq):
    n = jnp.arange(seq, dtype=I32)
    max_exact = T5_BUCKETS // 2
    nf = jnp.maximum(n, 1).astype(F32)
    large = max_exact + (jnp.log(nf / max_exact) / math.log(T5_MAX_DISTANCE / max_exact)
                         * (T5_BUCKETS - max_exact)).astype(I32)
    large = jnp.minimum(large, T5_BUCKETS - 1)
    bucket = jnp.where(n < max_exact, n, large)
    return t5_bias.astype(F32).T[:, bucket]


def _t5_strips(t5_bias, seq, t):
    last = t5_bias.astype(F32).T[:, T5_BUCKETS - 1:T5_BUCKETS]
    vec = (_t5_bias_by_distance(t5_bias, seq) - last) * LOG2E
    nd = 1
    while nd * t - (t - 1) < T5_MAX_DISTANCE and nd < seq // t:
        nd += 1
    d = np.arange(nd)[:, None]
    c = np.arange(2 * t)[None, :]
    rel = d * t + t - 1 - c
    rel = np.clip(rel, 0, seq - 1)
    strips = vec[:, jnp.asarray(rel.astype(np.int32))]
    return strips[:, :, None, :]


def _gdn_body(tb, xq_ref, xk_ref, xv_ref, z_ref, gcol_ref, grow_ref, cw_ref, nw_ref, o_ref, s_ref, carry_ref):
    @pl.when(pl.program_id(2) == 0)
    def _():
        s_ref[...] = jnp.zeros_like(s_ref)
        carry_ref[...] = jnp.zeros_like(carry_ref)

    n_heads = GDN_HEADS_PER_STEP
    c_ = GDN_CHUNK
    nc = tb // c_
    gates = gcol_ref[...]
    lane = lax.broadcasted_iota(I32, gates.shape, 1)

    def conv_silu(g, idx, x_ref):
        x = x_ref[:, g * LANES:(g + 1) * LANES].astype(F32)
        xe = jnp.concatenate([carry_ref[g, idx], x], axis=0)
        taps = cw_ref[g, 4 * idx:4 * idx + 4, :]
        y = taps[3:4] * x
        for tap in range(CONV_WIDTH - 1):
            y = y + taps[tap:tap + 1] * pltpu.roll(xe, CONV_WIDTH - 1 - tap, 0)[SUBLANES:]
        carry_ref[g, idx] = x[tb - SUBLANES:]
        return y * _sigmoid(y)

    q_l, k_l, v_l, beta_l, gc_l, gcr_l = [], [], [], [], [], []
    for g in range(n_heads):
        h = pl.program_id(1) * n_heads + g
        q = conv_silu(g, 0, xq_ref)
        k = conv_silu(g, 1, xk_ref)
        v_l.append(conv_silu(g, 2, xv_ref))
        q_l.append(q * lax.rsqrt(jnp.sum(q * q, axis=-1, keepdims=True) + RMS_EPS) * (B_HEAD_DIM ** -0.5))
        k_l.append(k * lax.rsqrt(jnp.sum(k * k, axis=-1, keepdims=True) + RMS_EPS))
        beta_l.append(jnp.sum(jnp.where(lane == h, gates, 0.0), axis=1, keepdims=True))
        gc_l.append(jnp.sum(jnp.where(lane == B_HEADS + h, gates, 0.0), axis=1, keepdims=True))
        gc_row = grow_ref[g, 1:2, :]
        gcr_l.extend(gc_row[None, :, c * c_:(c + 1) * c_] for c in range(nc))

    def chunked(xs):
        return jnp.concatenate([x.reshape(nc, c_, x.shape[-1]) for x in xs], axis=0)

    def bdot(a, b):
        return lax.dot_general(a.astype(BF16), b.astype(BF16), (((2,), (1,)), ((0,), (0,))),
                               preferred_element_type=F32)

    def bdot_nt(a, b):
        return lax.dot_general(a.astype(BF16), b.astype(BF16), (((2,), (2,)), ((0,), (0,))),
                               preferred_element_type=F32)

    q3, k3, v3, beta3, gc3 = chunked(q_l), chunked(k_l), chunked(v_l), chunked(beta_l), chunked(gc_l)
    gcr3 = jnp.concatenate(gcr_l, axis=0)
    ri = lax.broadcasted_iota(I32, (1, c_, c_), 1)
    ci = lax.broadcasted_iota(I32, (1, c_, c_), 2)
    decay = jnp.exp(jnp.where(ri >= ci, gc3 - gcr3, NEG_INF))
    egc = jnp.exp(gc3)
    kb3 = k3 * beta3
    rhs3 = jnp.concatenate([v3 * beta3, kb3 * egc], axis=2)
    g_last = gc3[:, c_ - 1:c_, :]
    qd3 = q3 * egc
    kd3 = k3 * jnp.exp(g_last - gc3)
    gl3 = jnp.exp(g_last)
    kk = bdot_nt(kb3, k3)
    qk3 = bdot_nt(q3, k3) * decay
    x = -(kk * jnp.where(ri > ci, decay, 0.0))
    r = x
    pw = x
    for _ in range(int(math.log2(c_)) - 1):
        pw = bdot(pw, pw)
        r = r + pw + bdot(r, pw)
    sol = rhs3 + bdot(r, rhs3)

    states = [s_ref[g] for g in range(n_heads)]
    outs = [[] for _ in range(n_heads)]
    for c in range(nc):
        for g in range(n_heads):
            n = g * nc + c
            sb = states[g].astype(BF16)
            v_new = sol[n, :, :B_HEAD_DIM] - _dot(sol[n, :, B_HEAD_DIM:].astype(BF16), sb)
            vnb = v_new.astype(BF16)
            outs[g].append(_dot(qd3[n].astype(BF16), sb) + _dot(qk3[n].astype(BF16), vnb))
            states[g] = states[g] * gl3[n] + _dot_tn(kd3[n].astype(BF16), vnb)
    for g in range(n_heads):
        cols = slice(g * LANES, (g + 1) * LANES)
        s_ref[g] = states[g]
        o = jnp.concatenate(outs[g], axis=0)
        o = o * lax.rsqrt(jnp.mean(o * o, axis=-1, keepdims=True) + RMS_EPS) * nw_ref[...]
        z = z_ref[:, cols].astype(F32)
        o_ref[:, cols] = (o * (z * _sigmoid(z))).astype(o_ref.dtype)


def _gdn(proj, gates, gates_row, conv_w, norm_w, bsz, seq, tb):
    nt = seq // tb
    h_ = B_HEADS
    gh = GDN_HEADS_PER_STEP
    col0 = 3 * A_HEADS
    assert h_ % gh == 0 and col0 % gh == 0
    blk = lambda off: pl.BlockSpec((tb, gh * LANES), lambda b, h, s: (b * nt + s, (col0 + off * h_) // gh + h))
    return pl.pallas_call(
        functools.partial(_gdn_body, tb),
        grid=(bsz, h_ // gh, nt),
        in_specs=[blk(0), blk(1), blk(2), blk(3),
                  pl.BlockSpec((tb, LANES), lambda b, h, s: (b * nt + s, 0)),
                  pl.BlockSpec((gh, 2, tb), lambda b, h, s: (b * (h_ // gh) + h, 0, s)),
                  pl.BlockSpec((gh, 3 * CONV_WIDTH, LANES), lambda b, h, s: (h, 0, 0)),
                  pl.BlockSpec((1, LANES), lambda b, h, s: (0, 0))],
        out_specs=pl.BlockSpec((tb, gh * LANES), lambda b, h, s: (b * nt + s, h)),
        out_shape=jax.ShapeDtypeStruct((bsz * seq, h_ * LANES), BF16),
        scratch_shapes=[pltpu.VMEM((gh, B_HEAD_DIM, B_HEAD_DIM), F32),
                        pltpu.VMEM((gh, 3, SUBLANES, LANES), F32)],
        compiler_params=_cparams(("parallel", "parallel", "arbitrary")),
        name="gdn",
    )(proj, proj, proj, proj, gates, gates_row, conv_w, norm_w)


ROW_TILES = D_MODEL // LANES


def _store_tile_rows(ref, y):
    t = y.shape[0]
    for c in range(ROW_TILES):
        ref[pl.ds(c, t, stride=ROW_TILES), :] = y[:, c * LANES:(c + 1) * LANES]


def _load_tile_rows(ref, t, lead=()):
    return jnp.concatenate([ref[lead + (pl.ds(c, t, stride=ROW_TILES), slice(None))] for c in range(ROW_TILES)],
                           axis=1)


def _outproj_ln_body(a_ref, b_ref, w_ref, x_ref, g_ref, bb_ref, o_ref, o3_ref, wb_ref):
    @pl.when(pl.program_id(0) == 0)
    def _():
        wb_ref[...] = w_ref[...].astype(BF16)

    half = a_ref.shape[1]
    hmix = _dot(a_ref[...], wb_ref[0:half, :]) + _dot(b_ref[...], wb_ref[half:, :])
    out = _layer_norm(DEEPNORM_ALPHA * x_ref[...] + hmix, g_ref[...], bb_ref[...])
    o_ref[...] = out
    _store_tile_rows(o3_ref, out)


def _outproj_ln(a, a_blk, b, b_blk, w, x, g, bb, tm=1024):
    n, d = x.shape
    tm = min(tm, n)
    half = d // 2
    row = pl.BlockSpec((1, d), lambda i: (0, 0))
    return pl.pallas_call(
        _outproj_ln_body,
        grid=(n // tm,),
        in_specs=[pl.BlockSpec((tm, half), lambda i: (i, a_blk)),
                  pl.BlockSpec((tm, half), lambda i: (i, b_blk)),
                  pl.BlockSpec((d, d), lambda i: (0, 0)),
                  pl.BlockSpec((tm, d), lambda i: (i, 0)), row, row],
        out_specs=[pl.BlockSpec((tm, d), lambda i: (i, 0)),
                   pl.BlockSpec((tm * ROW_TILES, LANES), lambda i: (i, 0))],
        out_shape=[jax.ShapeDtypeStruct((n, d), F32), jax.ShapeDtypeStruct((n * ROW_TILES, LANES), F32)],
        scratch_shapes=[pltpu.VMEM((d, d), BF16)],
        compiler_params=_cparams(("arbitrary",)),
        name="outproj_ln",
    )(a, b, w, x, g, bb)


def _router_body(tm, x_ref, w_ref, b_ref, idx_ref, gate_ref, rank_ref, cnt_ref, carry_ref):
    @pl.when(pl.program_id(0) == 0)
    def _():
        carry_ref[...] = jnp.zeros_like(carry_ref)

    logits = _dot_split(x_ref[...], w_ref[...]) + b_ref[...]
    lg = jnp.transpose(logits)[0:N_EXPERTS, :]
    e_iota = lax.broadcasted_iota(I32, (N_EXPERTS, tm), 0).astype(F32)
    vals, idxs, hots = [], [], []
    for _ in range(TOP_K):
        m = jnp.max(lg, axis=0, keepdims=True)
        idx = jnp.min(jnp.where(lg == m, e_iota, float(N_EXPERTS)), axis=0, keepdims=True)
        hot = e_iota == idx
        lg = jnp.where(hot, NEG_INF, lg)
        vals.append(m)
        idxs.append(idx)
        hots.append(hot)
    es = [jnp.exp(v - vals[0]) for v in vals]
    den = es[0] + es[1] + es[2] + es[3]
    sel = jnp.zeros((N_EXPERTS, tm), F32)
    for hot in hots:
        sel = sel + jnp.where(hot, 1.0, 0.0)
    before = jnp.where(lax.broadcasted_iota(I32, (tm, tm), 0) < lax.broadcasted_iota(I32, (tm, tm), 1),
                       1.0, 0.0).astype(BF16)
    cum = _dot(sel.astype(BF16), before) + carry_ref[:, 0:1]
    ranks = [jnp.sum(jnp.where(hot, cum, 0.0), axis=0, keepdims=True) for hot in hots]
    total = carry_ref[...] + jnp.sum(sel, axis=1, keepdims=True)
    carry_ref[...] = total
    idx_ref[...] = jnp.concatenate(idxs, axis=0).astype(I32)
    gate_ref[...] = jnp.concatenate([e / den for e in es], axis=0)
    rank_ref[...] = jnp.concatenate(ranks, axis=0).astype(I32)
    cnt_ref[...] = total


def _router(x, w_pad, b_pad, tm=512):
    n, d = x.shape
    tm = min(tm, n)
    out4 = lambda dt: jax.ShapeDtypeStruct((TOP_K, n), dt)
    blk4 = pl.BlockSpec((TOP_K, tm), lambda i: (0, i))
    return pl.pallas_call(
        functools.partial(_router_body, tm),
        grid=(n // tm,),
        in_specs=[pl.BlockSpec((tm, d), lambda i: (i, 0)),
                  pl.BlockSpec((d, LANES), lambda i: (0, 0)),
                  pl.BlockSpec((1, LANES), lambda i: (0, 0))],
        out_specs=[blk4, blk4, blk4, pl.BlockSpec((N_EXPERTS, LANES), lambda i: (0, 0))],
        out_shape=[out4(I32), out4(F32), out4(I32), jax.ShapeDtypeStruct((N_EXPERTS, LANES), F32)],
        scratch_shapes=[pltpu.VMEM((N_EXPERTS, LANES), F32)],
        compiler_params=_cparams(("arbitrary",)),
        name="moe_router",
    )(x, w_pad, b_pad)


def _row_slab(row):
    return pl.ds(pl.multiple_of(row * ROW_TILES, ROW_TILES), ROW_TILES)


def _start_row_dma(copy, k):
    copy.start(priority=k % 2)


def _start_row_dmas(n_rows, make_copy):
    def issue(r, carry):
        for k in range(TOP_K):
            _start_row_dma(make_copy(k, r), k)
        return carry

    lax.fori_loop(0, n_rows, issue, 0, unroll=4)


def _wait_row_dmas(n_rows, make_copy):
    def drain(r, carry):
        for k in range(TOP_K):
            make_copy(k, r).wait()
        return carry

    lax.fori_loop(0, n_rows, drain, 0, unroll=8)


def _row_dma_loops(n_rows, make_copy):
    _start_row_dmas(n_rows, make_copy)
    _wait_row_dmas(n_rows, make_copy)


def _dispatch_body(td, dest_ref, fill_ref, x3_ref, xg_out, zero_ref, sem, zsem):
    @pl.when(pl.program_id(0) == 0)
    def _():
        zero_ref[...] = jnp.zeros_like(zero_ref)

        def zero_copy(r):
            return pltpu.make_async_copy(zero_ref, xg_out.at[_row_slab(r)], zsem)

        def each_expert(fn):
            def expert(e, carry):
                lax.fori_loop(fill_ref[0, e], fill_ref[1, e], lambda r, c: (fn(r), c)[1], 0)
                return carry
            lax.fori_loop(0, N_EXPERTS, expert, 0)

        each_expert(lambda r: zero_copy(r).start())
        each_expert(lambda r: zero_copy(r).wait())

    _row_dma_loops(td, lambda k, r: pltpu.make_async_copy(
        x3_ref.at[_row_slab(r)], xg_out.at[_row_slab(dest_ref[k, r])], sem))


def _smem_rows(tile):
    return pl.BlockSpec((TOP_K, tile), lambda i: (0, i), memory_space=pltpu.SMEM)


def _dispatch(x3, dest, fill, n_rows, td=2048):
    n = x3.shape[0] // ROW_TILES
    td = min(td, n)
    return pl.pallas_call(
        functools.partial(_dispatch_body, td),
        grid=(n // td,),
        in_specs=[_smem_rows(td),
                  pl.BlockSpec(memory_space=pltpu.SMEM),
                  pl.BlockSpec((td * ROW_TILES, LANES), lambda i: (i, 0))],
        out_specs=pl.BlockSpec(memory_space=pl.ANY),
        out_shape=jax.ShapeDtypeStruct((n_rows * ROW_TILES, LANES), x3.dtype),
        scratch_shapes=[pltpu.VMEM((ROW_TILES, LANES), x3.dtype), pltpu.SemaphoreType.DMA(()),
                        pltpu.SemaphoreType.DMA(())],
        compiler_params=_cparams(("arbitrary",)),
        name="moe_dispatch",
    )(dest, fill, x3)


def _expert_body(be_ref, nu_ref, x_ref, wgu_ref, bgu_ref, wd_ref, bd_ref, o_ref, wgu_b, wd_b):
    i = pl.program_id(0)
    prev = be_ref[jnp.maximum(i - 1, 0)]
    fresh = jnp.logical_or(i == 0, be_ref[i] != prev)

    @pl.when(jnp.logical_and(i < nu_ref[0], fresh))
    def _():
        wgu_b[...] = wgu_ref[...].astype(BF16)
        wd_b[...] = wd_ref[...].astype(BF16)

    @pl.when(i < nu_ref[0])
    def _():
        x = _load_tile_rows(x_ref, MOE_ROWS).astype(BF16)
        hcat = _dot(x, wgu_b[...]) + bgu_ref[...]
        g = jnp.minimum(hcat[:, :D_FF], SWIGLU_LIMIT)
        u = jnp.clip(hcat[:, D_FF:], -SWIGLU_LIMIT, SWIGLU_LIMIT)
        act = g * _sigmoid(SWIGLU_ALPHA * g) * (u + 1.0)
        _store_tile_rows(o_ref, _dot(act.astype(BF16), wd_b[...]) + bd_ref[...])

    @pl.when(i >= nu_ref[0])
    def _():
        o_ref[...] = jnp.zeros_like(o_ref)


def _experts(xg, layer, block_expert, n_used, w_gu, b_gu, w_down, b_down):
    d = D_MODEL
    nb = xg.shape[0] // (MOE_ROWS * ROW_TILES)
    blk = pl.BlockSpec((MOE_ROWS * ROW_TILES, LANES), lambda i, be, nu: (i, 0))
    grid_spec = pltpu.PrefetchScalarGridSpec(
        num_scalar_prefetch=2,
        grid=(nb,),
        in_specs=[
            pl.BlockSpec((MOE_ROWS * ROW_TILES, LANES), lambda i, be, nu: (jnp.minimum(i, nu[0] - 1), 0)),
            pl.BlockSpec((None, None, d, 2 * D_FF), lambda i, be, nu: (layer, be[i], 0, 0)),
            pl.BlockSpec((None, None, 1, 2 * D_FF), lambda i, be, nu: (layer, be[i], 0, 0)),
            pl.BlockSpec((None, None, D_FF, d), lambda i, be, nu: (layer, be[i], 0, 0)),
            pl.BlockSpec((None, None, 1, d), lambda i, be, nu: (layer, be[i], 0, 0)),
        ],
        out_specs=blk,
        scratch_shapes=[pltpu.VMEM((d, 2 * D_FF), BF16), pltpu.VMEM((D_FF, d), BF16)],
    )
    return pl.pallas_call(
        _expert_body,
        grid_spec=grid_spec,
        out_shape=jax.ShapeDtypeStruct(xg.shape, F32),
        compiler_params=_cparams(("arbitrary",)),
        name="moe_experts",
    )(block_expert, n_used, xg, w_gu, b_gu[:, :, None, :], w_down, b_down[:, :, None, :])


def _combine_ln_body(tc, dest_ref, dest_next_ref, yg_hbm, gate_ref, x_ref, g_ref, b_ref, o_ref, ybuf, sems):
    i = pl.program_id(0)

    def gather(rows_ref, s):
        return lambda k, r: pltpu.make_async_copy(
            yg_hbm.at[_row_slab(rows_ref[k, r])], ybuf.at[s, k, _row_slab(r)], sems.at[s])

    def run(s):
        @pl.when(i == 0)
        def _():
            _start_row_dmas(tc, gather(dest_ref, s))

        @pl.when(i + 1 < pl.num_programs(0))
        def _():
            _start_row_dmas(tc, gather(dest_next_ref, 1 - s))

        _wait_row_dmas(tc, gather(dest_ref, s))
        gate = gate_ref[...]
        hmoe = DEEPNORM_ALPHA * x_ref[...]
        for k in range(TOP_K):
            hmoe = hmoe + gate[:, k:k + 1] * _load_tile_rows(ybuf, tc, (s, k))
        o_ref[...] = _layer_norm(hmoe, g_ref[...], b_ref[...])

    pl.when(i % 2 == 0)(functools.partial(run, 0))
    pl.when(i % 2 == 1)(functools.partial(run, 1))


def _combine_ln(yg, dest, gate_t, x, g, b, tc=512):
    n, d = x.shape
    tc = min(tc, n)
    steps = n // tc
    row = pl.BlockSpec((1, d), lambda i: (0, 0))
    return pl.pallas_call(
        functools.partial(_combine_ln_body, tc),
        grid=(steps,),
        in_specs=[_smem_rows(tc),
                  pl.BlockSpec((TOP_K, tc), lambda i: (0, jnp.minimum(i + 1, steps - 1)), memory_space=pltpu.SMEM),
                  pl.BlockSpec(memory_space=pl.ANY),
                  pl.BlockSpec((tc, TOP_K), lambda i: (i, 0)),
                  pl.BlockSpec((tc, d), lambda i: (i, 0)), row, row],
        out_specs=pl.BlockSpec((tc, d), lambda i: (i, 0)),
        out_shape=jax.ShapeDtypeStruct((n, d), F32),
        scratch_shapes=[pltpu.VMEM((2, TOP_K, tc * ROW_TILES, LANES), F32), pltpu.SemaphoreType.DMA((2,))],
        compiler_params=_cparams(("arbitrary",)),
        name="moe_combine_ln",
    )(dest, dest, yg, gate_t, x, g, b)


def _moe_ln(x, x3, layer, router_w, router_b, w_gu, b_gu, w_down, b_down, ln_g, ln_b):
    n, d = x.shape
    idx, gate, rank, cnt = _router(x, _pad_lanes(router_w), _pad_lanes(router_b[None, :]))
    counts = cnt[:, 0].astype(I32)
    padded = (counts + MOE_ROWS - 1) // MOE_ROWS * MOE_ROWS
    pad_end = jnp.cumsum(padded)
    pad_start = pad_end - padded
    hot = idx[:, :, None] == jnp.arange(N_EXPERTS, dtype=I32)[None, None, :]
    dest = jnp.sum(jnp.where(hot, pad_start[None, None, :], 0), axis=-1) + rank
    n_blocks = -(-(n * TOP_K) // MOE_ROWS) + N_EXPERTS
    block_row0 = jnp.arange(n_blocks, dtype=I32) * MOE_ROWS
    block_expert = jnp.minimum(jnp.sum((pad_end[None, :] <= block_row0[:, None]).astype(I32), axis=1),
                               N_EXPERTS - 1)
    n_used = (pad_end[-1:] // MOE_ROWS).astype(I32)
    fill = jnp.stack([pad_start + counts, pad_end])
    xg = _dispatch(x3, dest, fill, n_blocks * MOE_ROWS)
    yg = _experts(xg, layer, block_expert, n_used, w_gu, b_gu, w_down, b_down)
    return _combine_ln(yg, dest, gate.T, x, ln_g[None, :], ln_b[None, :])


def _even_mixer_ln(x, bsz, seq, w_in, w_out, lam_params, subln_w, conv_w, a_log, dt_bias, gdn_norm_w,
                   t5_bias, lam_init, ln_g, ln_b):
    main = 3 * A_HEADS * LANES + 4 * B_HEADS * B_HEAD_DIM
    n_q = A_HEADS * 2 * A_HEAD_DIM
    col_scale = jnp.concatenate([jnp.full((1, n_q), A_HEAD_DIM ** -0.5 * LOG2E, F32),
                                 jnp.ones((1, main - n_q), F32)], axis=1)
    proj = _proj(x, w_in[:, :main], col_scale)
    zeros4 = jnp.zeros((B_HEADS,), F32)
    p0 = _pad_lanes(jnp.concatenate([zeros4, a_log.astype(F32)])[None, :])
    p1 = _pad_lanes(jnp.concatenate([zeros4, dt_bias.astype(F32)])[None, :])
    gates = _gates("even", x, _pad_lanes(w_in[:, main:]), p0, p1, seq)
    t = min(ATTN_TILE, seq)
    strips = _t5_strips(t5_bias, seq, t)
    ao = _diff_attention(proj, bsz, seq, strips, lam_params.astype(F32), subln_w[None, :].astype(F32),
                         lam_init, t)
    g8 = gates[:, :2 * B_HEADS].reshape(bsz, seq, 2, B_HEADS)
    gates_row = g8.transpose(0, 3, 2, 1).reshape(bsz * B_HEADS, 2, seq)
    cw = conv_w.astype(F32).reshape(CONV_WIDTH, 3, B_HEADS, B_HEAD_DIM).transpose(2, 1, 0, 3)
    cw = cw.reshape(B_HEADS, 3 * CONV_WIDTH, B_HEAD_DIM)
    bo = _gdn(proj, gates, gates_row, cw, gdn_norm_w[None, :].astype(F32), bsz, seq, min(GDN_TILE, seq))
    return _outproj_ln(ao, 0, bo, 0, w_out, x, ln_g[None, :], ln_b[None, :])


def _odd_mixer_ln(x, bsz, seq, w_in, w_out, qk_norm_w, forget_b, ln_g, ln_b):
    main = 4 * C_HEADS * C_HEAD_DIM
    width = C_HEADS * C_HEAD_DIM
    scale = jnp.asarray([C_HEAD_DIM ** -0.5 * LOG2E, 1.0], F32)[:, None]
    wqk = qk_norm_w.astype(F32) * scale
    col_scale = jnp.concatenate([jnp.tile(wqk[0], C_HEADS), jnp.tile(wqk[1], C_HEADS),
                                 jnp.ones((main - 2 * width,), F32)])[None, :]
    proj = _proj(x, w_in[:, :main], col_scale, norm_chunks=range(2 * width // PROJ_CHUNK))
    fb =_pad_lanes(forget_b.astype(F32)[None, :])
    cum = _gates("odd", x, _pad_lanes(w_in[:, main:]), fb, fb, seq)
    cum_t = cum[:, :C_HEADS].reshape(bsz, seq, C_HEADS // 2, 2).transpose(0, 2, 3, 1)
    logit_bound = 1.02 * C_HEAD_DIM * jnp.max(jnp.abs(wqk[0])) * jnp.max(jnp.abs(wqk[1]))
    o = _fox_attention(proj, proj, cum_t, logit_bound, bsz, seq, min(ATTN_TILE, seq))
    return _outproj_ln(o, 0, o, 1, w_out, x, ln_g[None, :], ln_b[None, :])


def kernel(x, t5_bias, even_w_in, even_w_out, diff_lambda, diff_subln_w, gdn_conv_w, gdn_a_log, gdn_dt_bias, gdn_norm_w, odd_w_in, odd_w_out, fox_qk_norm_w, fox_forget_b, router_w, router_b, moe_w_gate_up, moe_b_gate_up, moe_w_down, moe_b_down, ln_mix_g, ln_mix_b, ln_ffn_g, ln_ffn_b):
    bsz, seq, d = x.shape
    xf = x.reshape(bsz * seq, d)
    for layer in range(DEPTH):
        i = layer // 2
        if layer % 2 == 0:
            lam_init = 0.8 - 0.6 * math.exp(-0.3 * layer)
            xf, x3 = _even_mixer_ln(xf, bsz, seq, even_w_in[i], even_w_out[i], diff_lambda[i], diff_subln_w[i],
                                    gdn_conv_w[i], gdn_a_log[i], gdn_dt_bias[i], gdn_norm_w[i], t5_bias, lam_init,
                                    ln_mix_g[layer], ln_mix_b[layer])
        else:
            xf, x3 = _odd_mixer_ln(xf, bsz, seq, odd_w_in[i], odd_w_out[i], fox_qk_norm_w[i], fox_forget_b[i],
                                   ln_mix_g[layer], ln_mix_b[layer])
        xf = _moe_ln(xf, x3, layer, router_w[layer], router_b[layer], moe_w_gate_up, moe_b_gate_up,
                     moe_w_down, moe_b_down, ln_ffn_g[layer], ln_ffn_b[layer])
    return xf.reshape(bsz, seq, d)
```
